```python
import math
import jax
import jax.numpy as jnp
from jax import lax
import numpy as np

D_MODEL = 2048
BATCH = 4
SEQ = 2048
DEPTH = 2
DEC_BATCH = 128
DEC_SEQ = 8
PAST_LEN = 16384
PAGE_SIZE = 128

D_MIX = D_MODEL
SSD_WIDTH = D_MIX // 2
SSD_HEAD_DIM = 64
SSD_HEADS = SSD_WIDTH // SSD_HEAD_DIM
SSD_GROUPS = 2
SSD_HPG = SSD_HEADS // SSD_GROUPS
SSD_STATE = 128
SSD_CONV = 4
SSD_CONV_DIM = SSD_WIDTH + 2 * SSD_GROUPS * SSD_STATE
SSD_CHUNK = 64
RWKV_WIDTH = D_MIX // 4
RWKV_HEAD_DIM = 64
RWKV_HEADS = RWKV_WIDTH // RWKV_HEAD_DIM
RWKV_W_LORA = 64
RWKV_A_LORA = 64
RWKV_G_LORA = 128
RWKV_SHIFT_DIM = 3 * RWKV_WIDTH + RWKV_W_LORA + RWKV_A_LORA + RWKV_G_LORA
RWKV_GN_EPS = 64e-5
S5_WIDTH = D_MIX - SSD_WIDTH - RWKV_WIDTH
S5_GROUP = 16
S5_GROUPS = S5_WIDTH // S5_GROUP
S5_STATE = 64
D_IN = SSD_WIDTH + SSD_CONV_DIM + SSD_HEADS + RWKV_SHIFT_DIM + S5_WIDTH
MEM_TOKENS = 256
MEM_HEADS = 4
MEM_HEAD_DIM = 128
MEM_WIDTH = MEM_HEADS * MEM_HEAD_DIM
D_FF = 5632
N_EXPERTS = 8
TOP_K = 2
D_FF_EXPERT = 5632
MOE_BLOCK = 128
N_DENSE = (DEPTH + 1) // 2
N_MOE = DEPTH // 2
RMS_EPS = 1e-6

kernel_name = 'hymba_ssd_rwkv7_s5_mem_moe_step'

F32 = jnp.float32


def rmsnorm(x, w):
    xf = x.astype(F32)
    y = xf * lax.rsqrt(jnp.mean(xf * xf, axis=-1, keepdims=True) + RMS_EPS)
    return (y * w.astype(F32)).astype(x.dtype)


def ssd_chunked(xdt, a_dt, bm, cm, s0):
    b, L = xdt.shape[0], xdt.shape[1]
    q = math.gcd(L, SSD_CHUNK)
    nc = L // q
    xc = xdt.reshape(b, nc, q, SSD_GROUPS, SSD_HPG, SSD_HEAD_DIM)
    bc = bm.reshape(b, nc, q, SSD_GROUPS, SSD_STATE)
    cc = cm.reshape(b, nc, q, SSD_GROUPS, SSD_STATE)
    ac = jnp.transpose(a_dt.reshape(b, nc, q, SSD_GROUPS, SSD_HPG), (0, 3, 4, 1, 2))
    a_cs = jnp.cumsum(ac, axis=-1)
    causal = jnp.tril(jnp.ones((q, q), dtype=bool))
    seg = a_cs[..., :, None] - a_cs[..., None, :]
    lmat = jnp.exp(jnp.where(causal, seg, -jnp.inf))
    cb = jnp.einsum('bclgn,bcsgn->bgcls', cc, bc)
    y_diag = jnp.einsum('bgcls,bgecls,bcsgep->bclgep', cb, lmat, xc)
    decay_in = jnp.exp(a_cs[..., -1:] - a_cs)
    chunk_states = jnp.einsum('bclgn,bgecl,bclgep->cbgepn', bc, decay_in, xc)
    chunk_decay = jnp.moveaxis(jnp.exp(a_cs[..., -1]), -1, 0)

    def carry_step(s, inp):
        st, dec = inp
        return dec[..., None, None] * s + st, s

    s_fin, s_in = lax.scan(carry_step, s0, (chunk_states, chunk_decay))
    y_off = jnp.einsum('bclgn,cbgepn,bgecl->bclgep', cc, s_in, jnp.exp(a_cs))
    return (y_diag + y_off).reshape(b, L, SSD_GROUPS, SSD_HPG, SSD_HEAD_DIM), s_fin


def ssd_mixer(z, xbc, dt, ssm_state, conv_state, conv_w, conv_b, dt_bias, a_log, d_skip, norm_w):
    b, L, _ = xbc.shape
    full = jnp.concatenate([conv_state.astype(xbc.dtype), xbc], axis=1)
    conv = lax.conv_general_dilated(full, conv_w[:, None, :].astype(full.dtype), (1,), 'VALID',
                                    dimension_numbers=('NWC', 'WIO', 'NWC'),
                                    feature_group_count=SSD_CONV_DIM)
    xbc_c = jax.nn.silu((conv + conv_b).astype(F32))
    xs, bm, cm = jnp.split(xbc_c, [SSD_WIDTH, SSD_WIDTH + SSD_GROUPS * SSD_STATE], axis=-1)
    xs = xs.reshape(b, L, SSD_GROUPS, SSD_HPG, SSD_HEAD_DIM)
    bm = bm.reshape(b, L, SSD_GROUPS, SSD_STATE)
    cm = cm.reshape(b, L, SSD_GROUPS, SSD_STATE)
    step = jax.nn.softplus(dt.astype(F32) + dt_bias.astype(F32)).reshape(b, L, SSD_GROUPS, SSD_HPG)
    a = -jnp.exp(a_log.astype(F32)).reshape(SSD_GROUPS, SSD_HPG)
    s0 = ssm_state.astype(F32).reshape(b, SSD_GROUPS, SSD_HPG, SSD_HEAD_DIM, SSD_STATE)
    y, s_fin = ssd_chunked(xs * step[..., None], step * a, bm, cm, s0)
    y = y + d_skip.astype(F32).reshape(SSD_GROUPS, SSD_HPG, 1) * xs
    gsz = SSD_WIDTH // SSD_GROUPS
    y = y.reshape(b, L, SSD_GROUPS, gsz) * jax.nn.silu(z.astype(F32)).reshape(b, L, SSD_GROUPS, gsz)
    y = y * lax.rsqrt(jnp.mean(y * y, axis=-1, keepdims=True) + RMS_EPS)
    y = y.reshape(b, L, SSD_WIDTH) * norm_w.astype(F32)
    new_ssm = s_fin.reshape(b, SSD_HEADS, SSD_HEAD_DIM, SSD_STATE).astype(z.dtype)
    return y.astype(z.dtype), new_ssm, full[:, -(SSD_CONV - 1):]


def rwkv7_mixer(h, wkv_state, shift_state, mu, w0, w2, a0, a2, g2, k_k, k_a, r_k, gn_w, gn_b):
    b, L, _ = h.shape
    prev = jnp.concatenate([shift_state[:, None].astype(h.dtype), h[:, :-1]], axis=1)
    hs = (h + (prev - h) * mu).astype(F32)
    W, WL, AL = RWKV_WIDTH, RWKV_W_LORA, RWKV_A_LORA
    r, k, v, wlo, alo, glo = jnp.split(hs, [W, 2 * W, 3 * W, 3 * W + WL, 3 * W + WL + AL], axis=-1)
    w_log = -jax.nn.softplus(-(w0.astype(F32) + jnp.tanh(wlo) @ w2.astype(F32))) - 0.5
    decay = jnp.exp(-jnp.exp(w_log))
    a = jax.nn.sigmoid(a0.astype(F32) + alo @ a2.astype(F32))
    g = jax.nn.sigmoid(glo) @ g2.astype(F32)

    def heads(t):
        return t.reshape(b, L, RWKV_HEADS, RWKV_HEAD_DIM)

    kk = heads(k * k_k.astype(F32))
    kk = kk / jnp.maximum(jnp.sqrt(jnp.sum(kk * kk, axis=-1, keepdims=True)), 1e-12)
    k = k * (1.0 + (a - 1.0) * k_a.astype(F32))
    r_h, k_h, v_h, w_h, a_h = heads(r), heads(k), heads(v), heads(decay), heads(a)
    b_h = kk * a_h

    def step(S, inp):
        rt, wt, kt, vt, kkt, bt = inp
        sa = jnp.einsum('bhvk,bhk->bhv', S, -kkt)
        S = S * wt[:, :, None, :] + sa[..., None] * bt[:, :, None, :] + vt[..., None] * kt[:, :, None, :]
        return S, jnp.einsum('bhvk,bhk->bhv', S, rt)

    seq = (jnp.moveaxis(r_h, 1, 0), jnp.moveaxis(w_h, 1, 0), jnp.moveaxis(k_h, 1, 0),
           jnp.moveaxis(v_h, 1, 0), jnp.moveaxis(kk, 1, 0), jnp.moveaxis(b_h, 1, 0))
    S_fin, ys = lax.scan(step, wkv_state.astype(F32), seq)
    y = jnp.moveaxis(ys, 0, 1)
    m = jnp.mean(y, axis=-1, keepdims=True)
    var = jnp.mean(jnp.square(y - m), axis=-1, keepdims=True)
    y = ((y - m) * lax.rsqrt(var + RWKV_GN_EPS)).reshape(b, L, W) * gn_w.astype(F32) + gn_b.astype(F32)
    bonus = jnp.sum(r_h * k_h * r_k.astype(F32).reshape(RWKV_HEADS, RWKV_HEAD_DIM), axis=-1, keepdims=True) * v_h
    y = (y + bonus.reshape(b, L, W)) * g
    return y.astype(h.dtype), S_fin.astype(h.dtype), h[:, -1]


def s5_mixer(u, st_re, st_im, lam_re, lam_im, b_re, b_im, c_re, c_im, d_skip, log_dt, glu_w, glu_b, norm_w):
    b, L, _ = u.shape
    uf = u.astype(F32).reshape(b, L, S5_GROUPS, S5_GROUP)
    delta = jnp.exp(log_dt.astype(F32))[:, None]
    lr, li = lam_re.astype(F32), lam_im.astype(F32)
    mag = jnp.exp(lr * delta)
    ab_re, ab_im = mag * jnp.cos(li * delta), mag * jnp.sin(li * delta)
    den = lr * lr + li * li
    q_re = ((ab_re - 1.0) * lr + ab_im * li) / den
    q_im = (ab_im * lr - (ab_re - 1.0) * li) / den
    br, bi = b_re.astype(F32), b_im.astype(F32)
    bb_re = q_re[..., None] * br - q_im[..., None] * bi
    bb_im = q_re[..., None] * bi + q_im[..., None] * br
    bu_re = jnp.einsum('bljc,jpc->bljp', uf, bb_re)
    bu_im = jnp.einsum('bljc,jpc->bljp', uf, bb_im)
    x0r, x0i = st_re.astype(F32), st_im.astype(F32)
    bu_re = bu_re.at[:, 0].add(ab_re * x0r - ab_im * x0i)
    bu_im = bu_im.at[:, 0].add(ab_re * x0i + ab_im * x0r)
    a_re = jnp.broadcast_to(ab_re, bu_re.shape)
    a_im = jnp.broadcast_to(ab_im, bu_im.shape)

    def combine(e1, e2):
        a1r, a1i, b1r, b1i = e1
        a2r, a2i, b2r, b2i = e2
        return (a2r * a1r - a2i * a1i, a2r * a1i + a2i * a1r,
                a2r * b1r - a2i * b1i + b2r, a2r * b1i + a2i * b1r + b2i)

    _, _, xr, xi = lax.associative_scan(combine, (a_re, a_im, bu_re, bu_im), axis=1)
    y = jnp.einsum('bljp,jcp->bljc', xr, c_re.astype(F32)) - jnp.einsum('bljp,jcp->bljc', xi, c_im.astype(F32))
    y = y.reshape(b, L, S5_WIDTH) + d_skip.astype(F32) * u.astype(F32)
    gy = jax.nn.gelu(y)
    y = gy * jax.nn.sigmoid(gy @ glu_w.astype(F32) + glu_b.astype(F32))
    y = rmsnorm(y, norm_w)
    return y.astype(u.dtype), xr[:, -1].astype(u.dtype), xi[:, -1].astype(u.dtype)


def mem_kv(mem, norm_w, wk, wv):
    m = rmsnorm(mem, norm_w)
    b = mem.shape[0]
    k = (m @ wk).reshape(b, MEM_TOKENS, MEM_HEADS, MEM_HEAD_DIM)
    v = (m @ wv).reshape(b, MEM_TOKENS, MEM_HEADS, MEM_HEAD_DIM)
    return k, v


def mem_attend(h, k, v, wq, wo):
    b, L, _ = h.shape
    q = (h @ wq).reshape(b, L, MEM_HEADS, MEM_HEAD_DIM)
    s = jnp.einsum('blhd,bmhd->bhlm', q, k).astype(F32) * (MEM_HEAD_DIM ** -0.5)
    p = jax.nn.softmax(s, axis=-1).astype(v.dtype)
    o = jnp.einsum('bhlm,bmhd->blhd', p, v).reshape(b, L, MEM_WIDTH)
    return o @ wo


def swiglu(h, w1, w3, w2):
    return (jax.nn.silu(h @ w1) * (h @ w3)) @ w2


def moe_swiglu(h, w_router, b_router, w1, w3, w2):
    shp = h.shape
    t = h.reshape(-1, shp[-1])
    T = t.shape[0]
    logits = t.astype(F32) @ w_router.astype(F32) + b_router.astype(F32)
    top_v, top_i = lax.top_k(logits, TOP_K)
    gates = jax.nn.softmax(top_v, axis=-1)
    n_slots = T * TOP_K
    flat_e = top_i.reshape(-1)
    flat_tok = jnp.repeat(jnp.arange(T, dtype=jnp.int32), TOP_K)
    flat_g = gates.reshape(-1)
    order = jnp.argsort(flat_e)
    se = flat_e[order]
    counts = jnp.bincount(flat_e, length=N_EXPERTS)
    starts = jnp.cumsum(counts) - counts
    padded = (counts + MOE_BLOCK - 1) // MOE_BLOCK * MOE_BLOCK
    pad_ends = jnp.cumsum(padded)
    pad_starts = pad_ends - padded
    dest = pad_starts[se] + jnp.arange(n_slots) - starts[se]
    n_blocks = -(-n_slots // MOE_BLOCK) + N_EXPERTS
    cap = n_blocks * MOE_BLOCK
    buf_tok = jnp.zeros((cap,), jnp.int32).at[dest].set(flat_tok[order])
    buf_gate = jnp.zeros((cap,), F32).at[dest].set(flat_g[order])
    block_e = jnp.minimum(jnp.searchsorted(pad_ends, jnp.arange(n_blocks) * MOE_BLOCK, side='right'), N_EXPERTS - 1)
    xb = t[buf_tok].reshape(n_blocks, MOE_BLOCK, shp[-1])

    def expert_block(args):
        xblk, e = args
        return swiglu(xblk, w1[e], w3[e], w2[e])

    yb = lax.map(expert_block, (xb, block_e))
    y = jnp.zeros((T, shp[-1]), F32).at[buf_tok].add(yb.reshape(cap, shp[-1]).astype(F32) * buf_gate[:, None])
    return y.astype(h.dtype).reshape(shp)


def decoder_layer(x, mem_k, mem_v, ssd_st, conv_st, wkv_st, shift_st, s5re_st, s5im_st, lw, ffn):
    h = rmsnorm(x, lw['norm_mix'])
    proj = h @ lw['w_in']
    c0 = SSD_WIDTH
    c1 = c0 + SSD_CONV_DIM
    c2 = c1 + SSD_HEADS
    c3 = c2 + RWKV_SHIFT_DIM
    z, xbc, dt, hr, u = jnp.split(proj, [c0, c1, c2, c3], axis=-1)
    y_ssd, ssd_n, conv_n = ssd_mixer(z, xbc, dt, ssd_st, conv_st, lw['ssd_conv_w'], lw['ssd_conv_b'],
                                     lw['ssd_dt_bias'], lw['ssd_a_log'], lw['ssd_d'], lw['ssd_norm_w'])
    y_rw, wkv_n, shift_n = rwkv7_mixer(hr, wkv_st, shift_st, lw['rwkv_mu'], lw['rwkv_w0'], lw['rwkv_w2'],
                                       lw['rwkv_a0'], lw['rwkv_a2'], lw['rwkv_g2'], lw['rwkv_k_k'],
                                       lw['rwkv_k_a'], lw['rwkv_r_k'], lw['rwkv_gn_w'], lw['rwkv_gn_b'])
    y_s5, s5r_n, s5i_n = s5_mixer(u, s5re_st, s5im_st, lw['s5_lam_re'], lw['s5_lam_im'], lw['s5_b_re'],
                                  lw['s5_b_im'], lw['s5_c_re'], lw['s5_c_im'], lw['s5_d'], lw['s5_log_dt'],
                                  lw['s5_glu_w'], lw['s5_glu_b'], lw['s5_norm_w'])
    x = x + jnp.concatenate([y_ssd, y_rw, y_s5], axis=-1) @ lw['w_out']
    x = x + mem_attend(rmsnorm(x, lw['norm_mem']), mem_k, mem_v, lw['wq_mem'], lw['wo_mem'])
    x = x + ffn(rmsnorm(x, lw['norm_ffn']))
    return x, (ssd_n, conv_n, wkv_n, shift_n, s5r_n, s5i_n)


def setup_inputs(seed: int = 0) -> dict:
    key = jax.random.key(seed)
    ks = iter(jax.random.split(key, 128))

    def nrm(shape, scale):
        return jax.random.normal(next(ks), shape, F32) * scale

    def unif(shape, lo, hi):
        return jax.random.uniform(next(ks), shape, F32, lo, hi)

    def gain(shape):
        return 1.0 + nrm(shape, 0.02)

    inp = {}
    inp['x_prompt'] = nrm((BATCH, SEQ, D_MODEL), 1.0)
    inp['x_sample'] = nrm((DEC_BATCH, DEC_SEQ, D_MODEL), 1.0)
    inp['mem_prompt'] = nrm((BATCH, MEM_TOKENS, D_MODEL), 1.0)
    inp['cache_mem_k'] = nrm((DEPTH, DEC_BATCH, MEM_TOKENS, MEM_HEADS, MEM_HEAD_DIM), 1.0)
    inp['cache_mem_v'] = nrm((DEPTH, DEC_BATCH, MEM_TOKENS, MEM_HEADS, MEM_HEAD_DIM), 1.0)
    inp['state_ssd'] = nrm((DEPTH, DEC_BATCH, SSD_HEADS, SSD_HEAD_DIM, SSD_STATE), 0.3)
    inp['state_ssd_conv'] = nrm((DEPTH, DEC_BATCH, SSD_CONV - 1, SSD_CONV_DIM), 1.0)
    inp['state_rwkv'] = nrm((DEPTH, DEC_BATCH, RWKV_HEADS, RWKV_HEAD_DIM, RWKV_HEAD_DIM), 0.3)
    inp['state_rwkv_shift'] = nrm((DEPTH, DEC_BATCH, RWKV_SHIFT_DIM), 1.0)
    inp['state_s5_re'] = nrm((DEPTH, DEC_BATCH, S5_GROUPS, S5_STATE), 0.5)
    inp['state_s5_im'] = nrm((DEPTH, DEC_BATCH, S5_GROUPS, S5_STATE), 0.5)
    inp['norm_mix'] = gain((DEPTH, D_MODEL))
    inp['w_in'] = nrm((DEPTH, D_MODEL, D_IN), D_MODEL ** -0.5)
    inp['ssd_conv_w'] = nrm((DEPTH, SSD_CONV, SSD_CONV_DIM), SSD_CONV ** -0.5)
    inp['ssd_conv_b'] = nrm((DEPTH, SSD_CONV_DIM), 0.02)
    dt0 = jnp.exp(unif((DEPTH, SSD_HEADS), math.log(1e-3), math.log(1e-1)))
    inp['ssd_dt_bias'] = dt0 + jnp.log(-jnp.expm1(-dt0))
    inp['ssd_a_log'] = jnp.log(unif((DEPTH, SSD_HEADS), 1.0, 16.0))
    inp['ssd_d'] = 1.0 + nrm((DEPTH, SSD_HEADS), 0.1)
    inp['ssd_norm_w'] = gain((DEPTH, SSD_WIDTH))
    inp['rwkv_mu'] = unif((DEPTH, RWKV_SHIFT_DIM), 0.0, 1.0)
    inp['rwkv_w0'] = unif((DEPTH, RWKV_WIDTH), -2.0, 1.0)
    inp['rwkv_w2'] = nrm((DEPTH, RWKV_W_LORA, RWKV_WIDTH), 0.1)
    inp['rwkv_a0'] = nrm((DEPTH, RWKV_WIDTH), 0.1)
    inp['rwkv_a2'] = nrm((DEPTH, RWKV_A_LORA, RWKV_WIDTH), 0.1)
    inp['rwkv_g2'] = nrm((DEPTH, RWKV_G_LORA, RWKV_WIDTH), RWKV_G_LORA ** -0.5)
    inp['rwkv_k_k'] = 0.85 + nrm((DEPTH, RWKV_WIDTH), 0.02)
    inp['rwkv_k_a'] = 1.0 + nrm((DEPTH, RWKV_WIDTH), 0.02)
    inp['rwkv_r_k'] = nrm((DEPTH, RWKV_WIDTH), 0.1)
    inp['rwkv_gn_w'] = gain((DEPTH, RWKV_WIDTH))
    inp['rwkv_gn_b'] = nrm((DEPTH, RWKV_WIDTH), 0.02)
    n_idx = jnp.arange(S5_STATE, dtype=F32)
    inp['s5_lam_re'] = -0.5 + nrm((DEPTH, S5_GROUPS, S5_STATE), 0.01)
    inp['s5_lam_im'] = math.pi * n_idx + nrm((DEPTH, S5_GROUPS, S5_STATE), 0.01)
    inp['s5_b_re'] = nrm((DEPTH, S5_GROUPS, S5_STATE, S5_GROUP), (2 * S5_GROUP) ** -0.5)
    inp['s5_b_im'] = nrm((DEPTH, S5_GROUPS, S5_STATE, S5_GROUP), (2 * S5_GROUP) ** -0.5)
    inp['s5_c_re'] = nrm((DEPTH, S5_GROUPS, S5_GROUP, S5_STATE), (2 * S5_STATE) ** -0.5)
    inp['s5_c_im'] = nrm((DEPTH, S5_GROUPS, S5_GROUP, S5_STATE), (2 * S5_STATE) ** -0.5)
    inp['s5_d'] = nrm((DEPTH, S5_WIDTH), 0.5)
    inp['s5_log_dt'] = unif((DEPTH, S5_GROUPS), math.log(1e-3), math.log(1e-1))
    inp['s5_glu_w'] = nrm((DEPTH, S5_WIDTH, S5_WIDTH), S5_WIDTH ** -0.5)
    inp['s5_glu_b'] = nrm((DEPTH, S5_WIDTH), 0.02)
    inp['s5_norm_w'] = gain((DEPTH, S5_WIDTH))
    inp['w_out'] = nrm((DEPTH, D_MIX, D_MODEL), D_MIX ** -0.5)
    inp['norm_mem'] = gain((DEPTH, D_MODEL))
    inp['mem_norm_w'] = gain((DEPTH, D_MODEL))
    inp['wq_mem'] = nrm((DEPTH, D_MODEL, MEM_WIDTH), D_MODEL ** -0.5)
    inp['wk_mem'] = nrm((DEPTH, D_MODEL, MEM_WIDTH), D_MODEL ** -0.5)
    inp['wv_mem'] = nrm((DEPTH, D_MODEL, MEM_WIDTH), D_MODEL ** -0.5)
    inp['wo_mem'] = nrm((DEPTH, MEM_WIDTH, D_MODEL), MEM_WIDTH ** -0.5)
    inp['norm_ffn'] = gain((DEPTH, D_MODEL))
    inp['ffn_w1'] = nrm((N_DENSE, D_MODEL, D_FF), D_MODEL ** -0.5)
    inp['ffn_w3'] = nrm((N_DENSE, D_MODEL, D_FF), D_MODEL ** -0.5)
    inp['ffn_w2'] = nrm((N_DENSE, D_FF, D_MODEL), D_FF ** -0.5)
    inp['moe_router_w'] = nrm((N_MOE, D_MODEL, N_EXPERTS), D_MODEL ** -0.5)
    inp['moe_router_b'] = nrm((N_MOE, N_EXPERTS), 0.01)
    inp['moe_w1'] = nrm((N_MOE, N_EXPERTS, D_MODEL, D_FF_EXPERT), D_MODEL ** -0.5)
    inp['moe_w3'] = nrm((N_MOE, N_EXPERTS, D_MODEL, D_FF_EXPERT), D_MODEL ** -0.5)
    inp['moe_w2'] = nrm((N_MOE, N_EXPERTS, D_FF_EXPERT, D_MODEL), D_FF_EXPERT ** -0.5)
    inp['final_norm_w'] = gain((D_MODEL,))
    return inp


def reference(x_prompt, x_sample, mem_prompt, cache_mem_k, cache_mem_v, state_ssd, state_ssd_conv,
              state_rwkv, state_rwkv_shift, state_s5_re, state_s5_im,
              norm_mix, w_in, ssd_conv_w, ssd_conv_b, ssd_dt_bias, ssd_a_log, ssd_d, ssd_norm_w,
              rwkv_mu, rwkv_w0, rwkv_w2, rwkv_a0, rwkv_a2, rwkv_g2, rwkv_k_k, rwkv_k_a, rwkv_r_k,
              rwkv_gn_w, rwkv_gn_b, s5_lam_re, s5_lam_im, s5_b_re, s5_b_im, s5_c_re, s5_c_im, s5_d,
              s5_log_dt, s5_glu_w, s5_glu_b, s5_norm_w, w_out, norm_mem, mem_norm_w, wq_mem, wk_mem,
              wv_mem, wo_mem, norm_ffn, ffn_w1, ffn_w3, ffn_w2, moe_router_w, moe_router_b,
              moe_w1, moe_w3, moe_w2, final_norm_w):
    def layer_weights(i):
        return {
            'norm_mix': norm_mix[i], 'w_in': w_in[i],
            'ssd_conv_w': ssd_conv_w[i], 'ssd_conv_b': ssd_conv_b[i], 'ssd_dt_bias': ssd_dt_bias[i],
            'ssd_a_log': ssd_a_log[i], 'ssd_d': ssd_d[i], 'ssd_norm_w': ssd_norm_w[i],
            'rwkv_mu': rwkv_mu[i], 'rwkv_w0': rwkv_w0[i], 'rwkv_w2': rwkv_w2[i], 'rwkv_a0': rwkv_a0[i],
            'rwkv_a2': rwkv_a2[i], 'rwkv_g2': rwkv_g2[i], 'rwkv_k_k': rwkv_k_k[i], 'rwkv_k_a': rwkv_k_a[i],
            'rwkv_r_k': rwkv_r_k[i], 'rwkv_gn_w': rwkv_gn_w[i], 'rwkv_gn_b': rwkv_gn_b[i],
            's5_lam_re': s5_lam_re[i], 's5_lam_im': s5_lam_im[i], 's5_b_re': s5_b_re[i], 's5_b_im': s5_b_im[i],
            's5_c_re': s5_c_re[i], 's5_c_im': s5_c_im[i], 's5_d': s5_d[i], 's5_log_dt': s5_log_dt[i],
            's5_glu_w': s5_glu_w[i], 's5_glu_b': s5_glu_b[i], 's5_norm_w': s5_norm_w[i],
            'w_out': w_out[i], 'norm_mem': norm_mem[i], 'wq_mem': wq_mem[i], 'wo_mem': wo_mem[i],
            'norm_ffn': norm_ffn[i],
        }

    def layer_ffn(i):
        j = i // 2
        if i % 2 == 0:
            return lambda t: swiglu(t, ffn_w1[j], ffn_w3[j], ffn_w2[j])
        return lambda t: moe_swiglu(t, moe_router_w[j], moe_router_b[j], moe_w1[j], moe_w3[j], moe_w2[j])

    bp = x_prompt.shape[0]
    dtp = x_prompt.dtype
    xp = x_prompt
    p_mk, p_mv, p_st = [], [], []
    for i in range(DEPTH):
        mk, mv = mem_kv(mem_prompt, mem_norm_w[i], wk_mem[i], wv_mem[i])
        xp, st = decoder_layer(
            xp, mk, mv,
            jnp.zeros((bp, SSD_HEADS, SSD_HEAD_DIM, SSD_STATE), dtp),
            jnp.zeros((bp, SSD_CONV - 1, SSD_CONV_DIM), dtp),
            jnp.zeros((bp, RWKV_HEADS, RWKV_HEAD_DIM, RWKV_HEAD_DIM), dtp),
            jnp.zeros((bp, RWKV_SHIFT_DIM), dtp),
            jnp.zeros((bp, S5_GROUPS, S5_STATE), dtp),
            jnp.zeros((bp, S5_GROUPS, S5_STATE), dtp),
            layer_weights(i), layer_ffn(i))
        p_mk.append(mk)
        p_mv.append(mv)
        p_st.append(st)
    y_prompt = rmsnorm(xp, final_norm_w)

    xs = x_sample
    s_st = []
    for i in range(DEPTH):
        xs, st = decoder_layer(xs, cache_mem_k[i], cache_mem_v[i], state_ssd[i], state_ssd_conv[i],
                               state_rwkv[i], state_rwkv_shift[i], state_s5_re[i], state_s5_im[i],
                               layer_weights(i), layer_ffn(i))
        s_st.append(st)
    y_sample = rmsnorm(xs, final_norm_w)

    def stk(lst, j):
        return jnp.stack([s[j] for s in lst])

    return (y_prompt, y_sample, jnp.stack(p_mk), jnp.stack(p_mv),
            stk(p_st, 0), stk(p_st, 1), stk(p_st, 2), stk(p_st, 3), stk(p_st, 4), stk(p_st, 5),
            stk(s_st, 0), stk(s_st, 1), stk(s_st, 2), stk(s_st, 3), stk(s_st, 4), stk(s_st, 5))
```

```python
import functools
import math

import jax
import jax.numpy as jnp
from jax import lax
from jax.experimental import pallas as pl
from jax.experimental.pallas import tpu as pltpu

D_MODEL = 2048
DEPTH = 2
SSD_WIDTH = 1024
SSD_HEAD_DIM = 64
SSD_HEADS = 16
SSD_GROUPS = 2
SSD_HPG = 8
SSD_STATE = 128
SSD_CONV = 4
SSD_CONV_DIM = 1536
SSD_CHUNK = 64
RWKV_WIDTH = 512
RWKV_HEAD_DIM = 64
RWKV_HEADS = 8
RWKV_W_LORA = 64
RWKV_A_LORA = 64
RWKV_G_LORA = 128
RWKV_SHIFT_DIM = 1792
RWKV_GN_EPS = 64e-5
S5_WIDTH = 512
S5_GROUP = 16
S5_GROUPS = 32
S5_STATE = 64
MEM_TOKENS = 256
MEM_HEADS = 4
MEM_HEAD_DIM = 128
MEM_WIDTH = 512
D_FF = 5632
N_EXPERTS = 8
TOP_K = 2
RMS_EPS = 1e-6

F32 = jnp.float32
BF16 = jnp.bfloat16

PROJ_Z = 0
PROJ_XBC = PROJ_Z + SSD_WIDTH
PROJ_HR = PROJ_XBC + SSD_CONV_DIM
PROJ_U = PROJ_HR + RWKV_SHIFT_DIM
PROJ_DT = PROJ_U + S5_WIDTH
PROJ_WIDTH = PROJ_DT + 128

VMEM_LIMIT = 48 * 1024 * 1024
ROW_TILE = 512
MOE_ROW_TILE = 512


def _rms_rows(x, w):
    return x * lax.rsqrt(jnp.mean(x * x, axis=-1, keepdims=True) + RMS_EPS) * w


def _mm_kernel(*refs, has_norm, has_res):
    refs = list(refs)
    a_ref = refs.pop(0)
    nw_ref = refs.pop(0) if has_norm else None
    w_ref = refs.pop(0)
    res_ref = refs.pop(0) if has_res else None
    o_ref = refs.pop(0)
    abf_ref = refs.pop(0)

    @pl.when(pl.program_id(1) == 0)
    def _():
        a = a_ref[...]
        if has_norm:
            a = _rms_rows(a, nw_ref[...])
        abf_ref[...] = a.astype(BF16)

    acc = jnp.dot(abf_ref[...], w_ref[...], preferred_element_type=F32)
    if has_res:
        acc = acc + res_ref[...]
    o_ref[...] = acc


def _col_tile(n):
    for t in (512, 384, 256, 128):
        if n % t == 0:
            return t
    raise ValueError(f"unsupported matmul width {n}")


def _mm(a, w, norm_w=None, residual=None):
    m, k = a.shape
    n = w.shape[1]
    tm, tn = ROW_TILE, _col_tile(n)
    assert m % tm == 0
    has_norm, has_res = norm_w is not None, residual is not None
    in_specs = [pl.BlockSpec((tm, k), lambda i, j: (i, 0))]
    args = [a]
    if has_norm:
        in_specs.append(pl.BlockSpec((1, k), lambda i, j: (0, 0)))
        args.append(norm_w.reshape(1, k))
    in_specs.append(pl.BlockSpec((k, tn), lambda i, j: (0, j)))
    args.append(w)
    if has_res:
        in_specs.append(pl.BlockSpec((tm, tn), lambda i, j: (i, j)))
        args.append(residual)
    return pl.pallas_call(
        functools.partial(_mm_kernel, has_norm=has_norm, has_res=has_res),
        out_shape=jax.ShapeDtypeStruct((m, n), F32),
        grid=(m // tm, n // tn),
        in_specs=in_specs,
        out_specs=pl.BlockSpec((tm, tn), lambda i, j: (i, j)),
        scratch_shapes=[pltpu.VMEM((tm, k), BF16)],
        compiler_params=pltpu.CompilerParams(
            dimension_semantics=("parallel", "arbitrary"), vmem_limit_bytes=VMEM_LIMIT),
        name="matmul",
    )(*args)


def _ffn_kernel(x_ref, nw_ref, w1_ref, w3_ref, w2_ref, o_ref, h_ref, acc_ref):
    f = pl.program_id(1)

    @pl.when(f == 0)
    def _():
        h_ref[...] = _rms_rows(x_ref[...], nw_ref[...]).astype(BF16)
        acc_ref[...] = jnp.zeros_like(acc_ref)

    h = h_ref[...]
    g = jnp.dot(h, w1_ref[...], preferred_element_type=F32)
    u = jnp.dot(h, w3_ref[...], preferred_element_type=F32)
    a = (g * jax.nn.sigmoid(g) * u).astype(BF16)
    acc_ref[...] += jnp.dot(a, w2_ref[...], preferred_element_type=F32)

    @pl.when(f == pl.num_programs(1) - 1)
    def _():
        o_ref[...] = x_ref[...] + acc_ref[...]


def _ffn(x, norm_w, w1, w3, w2):
    m, d = x.shape
    dff = w1.shape[1]
    tm, tf = ROW_TILE, 512
    return pl.pallas_call(
        _ffn_kernel,
        out_shape=jax.ShapeDtypeStruct((m, d), F32),
        grid=(m // tm, dff // tf),
        in_specs=[
            pl.BlockSpec((tm, d), lambda i, f: (i, 0)),
            pl.BlockSpec((1, d), lambda i, f: (0, 0)),
            pl.BlockSpec((d, tf), lambda i, f: (0, f)),
            pl.BlockSpec((d, tf), lambda i, f: (0, f)),
            pl.BlockSpec((tf, d), lambda i, f: (f, 0)),
        ],
        out_specs=pl.BlockSpec((tm, d), lambda i, f: (i, 0)),
        scratch_shapes=[pltpu.VMEM((tm, d), BF16), pltpu.VMEM((tm, d), F32)],
        compiler_params=pltpu.CompilerParams(
            dimension_semantics=("parallel", "arbitrary"), vmem_limit_bytes=VMEM_LIMIT),
        name="ffn_swiglu",
    )(x, norm_w.reshape(1, d), w1, w3, w2)


def _moe_kernel(be_ref, nv_ref, x_ref, w1_ref, w3_ref, w2_ref, o_ref, acc_ref):
    b, f = pl.program_id(0), pl.program_id(1)
    valid = b < nv_ref[0]

    @pl.when(f == 0)
    def _():
        acc_ref[...] = jnp.zeros_like(acc_ref)

    @pl.when(valid)
    def _():
        h = x_ref[...]
        g = jnp.dot(h, w1_ref[0].astype(BF16), preferred_element_type=F32)
        u = jnp.dot(h, w3_ref[0].astype(BF16), preferred_element_type=F32)
        a = (g * jax.nn.sigmoid(g) * u).astype(BF16)
        acc_ref[...] += jnp.dot(a, w2_ref[0].astype(BF16), preferred_element_type=F32)

    @pl.when(f == pl.num_programs(1) - 1)
    def _():
        o_ref[...] = acc_ref[...]


def _moe_experts(xb, block_e, n_valid, w1, w3, w2):
    cap, d = xb.shape
    dff = w1.shape[2]
    tm, tf = MOE_ROW_TILE, 256
    nb, nf = cap // tm, dff // tf

    def w13_map(b, f, be, nv):
        ok = b < nv[0]
        return (be[b], 0, jnp.where(ok, f, nf - 1))

    def w2_map(b, f, be, nv):
        ok = b < nv[0]
        return (be[b], jnp.where(ok, f, nf - 1), 0)

    grid_spec = pltpu.PrefetchScalarGridSpec(
        num_scalar_prefetch=2,
        grid=(nb, nf),
        in_specs=[
            pl.BlockSpec((tm, d), lambda b, f, be, nv: (b, 0)),
            pl.BlockSpec((1, d, tf), w13_map),
            pl.BlockSpec((1, d, tf), w13_map),
            pl.BlockSpec((1, tf, d), w2_map),
        ],
        out_specs=pl.BlockSpec((tm, d), lambda b, f, be, nv: (b, 0)),
        scratch_shapes=[pltpu.VMEM((tm, d), F32)],
    )
    return pl.pallas_call(
        _moe_kernel,
        out_shape=jax.ShapeDtypeStruct((cap, d), F32),
        grid_spec=grid_spec,
        compiler_params=pltpu.CompilerParams(
            dimension_semantics=("arbitrary", "arbitrary"), vmem_limit_bytes=VMEM_LIMIT),
        name="moe_swiglu",
    )(block_e, n_valid, xb, w1, w3, w2)


def _moe(x, norm_w, w_router, b_router, w1, w3, w2):
    t, d = x.shape
    tm = MOE_ROW_TILE
    h = _rms_rows(x, norm_w)
    logits = jnp.dot(h, w_router, precision=lax.Precision.HIGHEST) + b_router
    top_v, top_i = lax.top_k(logits, TOP_K)
    gates = jax.nn.softmax(top_v, axis=-1)
    n_slots = t * TOP_K
    flat_e = top_i.reshape(-1).astype(jnp.int32)
    order = jnp.argsort(flat_e)
    se = flat_e[order]
    counts = jnp.bincount(flat_e, length=N_EXPERTS).astype(jnp.int32)
    starts = jnp.cumsum(counts) - counts
    padded = (counts + tm - 1) // tm * tm
    pad_ends = jnp.cumsum(padded)
    pad_starts = pad_ends - padded
    dest_sorted = pad_starts[se] + jnp.arange(n_slots, dtype=jnp.int32) - starts[se]
    nb = n_slots // tm + N_EXPERTS
    cap = nb * tm
    slot_pos = jnp.zeros((n_slots,), jnp.int32).at[order].set(dest_sorted)
    buf_tok = jnp.zeros((cap,), jnp.int32).at[dest_sorted].set(order // TOP_K)
    block_e = jnp.minimum(
        jnp.searchsorted(pad_ends, jnp.arange(nb, dtype=jnp.int32) * tm, side='right'),
        N_EXPERTS - 1).astype(jnp.int32)
    n_valid = (pad_ends[-1] // tm).astype(jnp.int32).reshape(1)
    xb = h.astype(BF16)[buf_tok]
    yb = _moe_experts(xb, block_e, n_valid, w1, w3, w2)
    pos = slot_pos.reshape(t, TOP_K)
    y = yb[pos[:, 0]] * gates[:, 0:1] + yb[pos[:, 1]] * gates[:, 1:2]
    return x + y


def _ssd_chunked(xdt, a_dt, bm, cm, s0):
    b, L = xdt.shape[0], xdt.shape[1]
    q = math.gcd(L, SSD_CHUNK)
    nc = L // q
    xc = xdt.reshape(b, nc, q, SSD_GROUPS, SSD_HPG, SSD_HEAD_DIM)
    bc = bm.reshape(b, nc, q, SSD_GROUPS, SSD_STATE)
    cc = cm.reshape(b, nc, q, SSD_GROUPS, SSD_STATE)
    ac = jnp.transpose(a_dt.reshape(b, nc, q, SSD_GROUPS, SSD_HPG), (0, 3, 4, 1, 2))
    a_cs = jnp.cumsum(ac, axis=-1)
    causal = jnp.tril(jnp.ones((q, q), dtype=bool))
    seg = a_cs[..., :, None] - a_cs[..., None, :]
    lmat = jnp.exp(jnp.where(causal, seg, -jnp.inf))
    cb = jnp.einsum('bclgn,bcsgn->bgcls', cc, bc)
    y_diag = jnp.einsum('bgcls,bgecls,bcsgep->bclgep', cb, lmat, xc)
    decay_in = jnp.exp(a_cs[..., -1:] - a_cs)
    chunk_states = jnp.einsum('bclgn,bgecl,bclgep->cbgepn', bc, decay_in, xc)
    chunk_decay = jnp.moveaxis(jnp.exp(a_cs[..., -1]), -1, 0)

    def carry_step(s, inp):
        st, dec = inp
        return dec[..., None, None] * s + st, s

    s_fin, s_in = lax.scan(carry_step, s0, (chunk_states, chunk_decay))
    y_off = jnp.einsum('bclgn,cbgepn,bgecl->bclgep', cc, s_in, jnp.exp(a_cs))
    return (y_diag + y_off).reshape(b, L, SSD_GROUPS, SSD_HPG, SSD_HEAD_DIM), s_fin


def _ssd_mixer(z, xbc, dt, ssm_state, conv_state, conv_w, conv_b, dt_bias, a_log, d_skip, norm_w):
    b, L, _ = xbc.shape
    full = jnp.concatenate([conv_state, xbc], axis=1)
    conv = sum(full[:, j:j + L] * conv_w[j] for j in range(SSD_CONV))
    xbc_c = jax.nn.silu(conv + conv_b)
    xs, bm, cm = jnp.split(xbc_c, [SSD_WIDTH, SSD_WIDTH + SSD_GROUPS * SSD_STATE], axis=-1)
    xs = xs.reshape(b, L, SSD_GROUPS, SSD_HPG, SSD_HEAD_DIM)
    bm = bm.reshape(b, L, SSD_GROUPS, SSD_STATE)
    cm = cm.reshape(b, L, SSD_GROUPS, SSD_STATE)
    step = jax.nn.softplus(dt + dt_bias).reshape(b, L, SSD_GROUPS, SSD_HPG)
    a = -jnp.exp(a_log).reshape(SSD_GROUPS, SSD_HPG)
    s0 = ssm_state.reshape(b, SSD_GROUPS, SSD_HPG, SSD_HEAD_DIM, SSD_STATE)
    y, s_fin = _ssd_chunked(xs * step[..., None], step * a, bm, cm, s0)
    y = y + d_skip.reshape(SSD_GROUPS, SSD_HPG, 1) * xs
    gsz = SSD_WIDTH // SSD_GROUPS
    y = y.reshape(b, L, SSD_GROUPS, gsz) * jax.nn.silu(z).reshape(b, L, SSD_GROUPS, gsz)
    y = y * lax.rsqrt(jnp.mean(y * y, axis=-1, keepdims=True) + RMS_EPS)
    y = y.reshape(b, L, SSD_WIDTH) * norm_w
    new_ssm = s_fin.reshape(b, SSD_HEADS, SSD_HEAD_DIM, SSD_STATE)
    return y, new_ssm, full[:, -(SSD_CONV - 1):]


def _rwkv7_mixer(h, wkv_state, shift_state, mu, w0, w2, a0, a2, g2, k_k, k_a, r_k, gn_w, gn_b):
    b, L, _ = h.shape
    prev = jnp.concatenate([shift_state[:, None], h[:, :-1]], axis=1)
    hs = h + (prev - h) * mu
    W, WL, AL = RWKV_WIDTH, RWKV_W_LORA, RWKV_A_LORA
    r, k, v, wlo, alo, glo = jnp.split(hs, [W, 2 * W, 3 * W, 3 * W + WL, 3 * W + WL + AL], axis=-1)
    w_log = -jax.nn.softplus(-(w0 + jnp.tanh(wlo) @ w2)) - 0.5
    decay = jnp.exp(-jnp.exp(w_log))
    a = jax.nn.sigmoid(a0 + alo @ a2)
    g = jax.nn.sigmoid(glo) @ g2

    def heads(t):
        return t.reshape(b, L, RWKV_HEADS, RWKV_HEAD_DIM)

    kk = heads(k * k_k)
    kk = kk / jnp.maximum(jnp.sqrt(jnp.sum(kk * kk, axis=-1, keepdims=True)), 1e-12)
    k = k * (1.0 + (a - 1.0) * k_a)
    r_h, k_h, v_h, w_h, a_h = heads(r), heads(k), heads(v), heads(decay), heads(a)
    b_h = kk * a_h

    def step(S, inp):
        rt, wt, kt, vt, kkt, bt = inp
        sa = jnp.einsum('bhvk,bhk->bhv', S, -kkt)
        S = S * wt[:, :, None, :] + sa[..., None] * bt[:, :, None, :] + vt[..., None] * kt[:, :, None, :]
        return S, jnp.einsum('bhvk,bhk->bhv', S, rt)

    seq = (jnp.moveaxis(r_h, 1, 0), jnp.moveaxis(w_h, 1, 0), jnp.moveaxis(k_h, 1, 0),
           jnp.moveaxis(v_h, 1, 0), jnp.moveaxis(kk, 1, 0), jnp.moveaxis(b_h, 1, 0))
    S_fin, ys = lax.scan(step, wkv_state, seq)
    y = jnp.moveaxis(ys, 0, 1)
    m = jnp.mean(y, axis=-1, keepdims=True)
    var = jnp.mean(jnp.square(y - m), axis=-1, keepdims=True)
    y = ((y - m) * lax.rsqrt(var + RWKV_GN_EPS)).reshape(b, L, W) * gn_w + gn_b
    bonus = jnp.sum(r_h * k_h * r_k.reshape(RWKV_HEADS, RWKV_HEAD_DIM), axis=-1, keepdims=True) * v_h
    y = (y + bonus.reshape(b, L, W)) * g
    return y, S_fin, h[:, -1]


def _s5_mixer(u, st_re, st_im, lam_re, lam_im, b_re, b_im, c_re, c_im, d_skip, log_dt, glu_w, glu_b, norm_w):
    b, L, _ = u.shape
    uf = u.reshape(b, L, S5_GROUPS, S5_GROUP)
    delta = jnp.exp(log_dt)[:, None]
    lr, li = lam_re, lam_im
    mag = jnp.exp(lr * delta)
    ab_re, ab_im = mag * jnp.cos(li * delta), mag * jnp.sin(li * delta)
    den = lr * lr + li * li
    q_re = ((ab_re - 1.0) * lr + ab_im * li) / den
    q_im = (ab_im * lr - (ab_re - 1.0) * li) / den
    bb_re = q_re[..., None] * b_re - q_im[..., None] * b_im
    bb_im = q_re[..., None] * b_im + q_im[..., None] * b_re
    bu_re = jnp.einsum('bljc,jpc->bljp', uf, bb_re)
    bu_im = jnp.einsum('bljc,jpc->bljp', uf, bb_im)
    bu_re = bu_re.at[:, 0].add(ab_re * st_re - ab_im * st_im)
    bu_im = bu_im.at[:, 0].add(ab_re * st_im + ab_im * st_re)
    a_re = jnp.broadcast_to(ab_re, bu_re.shape)
    a_im = jnp.broadcast_to(ab_im, bu_im.shape)

    def combine(e1, e2):
        a1r, a1i, b1r, b1i = e1
        a2r, a2i, b2r, b2i = e2
        return (a2r * a1r - a2i * a1i, a2r * a1i + a2i * a1r,
                a2r * b1r - a2i * b1i + b2r, a2r * b1i + a2i * b1r + b2i)

    _, _, xr, xi = lax.associative_scan(combine, (a_re, a_im, bu_re, bu_im), axis=1)
    y = jnp.einsum('bljp,jcp->bljc', xr, c_re) - jnp.einsum('bljp,jcp->bljc', xi, c_im)
    y = y.reshape(b, L, S5_WIDTH) + d_skip * u
    gy = jax.nn.gelu(y)
    y = gy * jax.nn.sigmoid(gy @ glu_w + glu_b)
    y = _rms_rows(y, norm_w)
    return y, xr[:, -1], xi[:, -1]


def _mem_attend_core(q, k, v):
    b, L, _ = q.shape
    q = q.reshape(b, L, MEM_HEADS, MEM_HEAD_DIM)
    s = jnp.einsum('blhd,bmhd->bhlm', q, k) * (MEM_HEAD_DIM ** -0.5)
    p = jax.nn.softmax(s, axis=-1)
    return jnp.einsum('bhlm,bmhd->blhd', p, v).reshape(b, L, MEM_WIDTH)


def kernel(x_prompt, x_sample, mem_prompt, cache_mem_k, cache_mem_v, state_ssd, state_ssd_conv, state_rwkv, state_rwkv_shift, state_s5_re, state_s5_im, norm_mix, w_in, ssd_conv_w, ssd_conv_b, ssd_dt_bias, ssd_a_log, ssd_d, ssd_norm_w, rwkv_mu, rwkv_w0, rwkv_w2, rwkv_a0, rwkv_a2, rwkv_g2, rwkv_k_k, rwkv_k_a, rwkv_r_k, rwkv_gn_w, rwkv_gn_b, s5_lam_re, s5_lam_im, s5_b_re, s5_b_im, s5_c_re, s5_c_im, s5_d, s5_log_dt, s5_glu_w, s5_glu_b, s5_norm_w, w_out, norm_mem, mem_norm_w, wq_mem, wk_mem, wv_mem, wo_mem, norm_ffn, ffn_w1, ffn_w3, ffn_w2, moe_router_w, moe_router_b, moe_w1, moe_w3, moe_w2, final_norm_w):
    bp, lp, d = x_prompt.shape
    bs, ls, _ = x_sample.shape
    tp, ts = bp * lp, bs * ls
    x = jnp.concatenate([x_prompt.reshape(tp, d), x_sample.reshape(ts, d)], axis=0)
    mem_rows = mem_prompt.reshape(bp * MEM_TOKENS, d)

    zeros_states = (
        jnp.zeros((bp, SSD_HEADS, SSD_HEAD_DIM, SSD_STATE), F32),
        jnp.zeros((bp, SSD_CONV - 1, SSD_CONV_DIM), F32),
        jnp.zeros((bp, RWKV_HEADS, RWKV_HEAD_DIM, RWKV_HEAD_DIM), F32),
        jnp.zeros((bp, RWKV_SHIFT_DIM), F32),
        jnp.zeros((bp, S5_GROUPS, S5_STATE), F32),
        jnp.zeros((bp, S5_GROUPS, S5_STATE), F32),
    )

    p_mk, p_mv, p_st, s_st = [], [], [], []
    for i in range(DEPTH):
        c0 = SSD_WIDTH
        c1 = c0 + SSD_CONV_DIM
        c2 = c1 + SSD_HEADS
        c3 = c2 + RWKV_SHIFT_DIM
        wi = w_in[i]
        w_in_packed = jnp.concatenate(
            [wi[:, :c1], wi[:, c2:], wi[:, c1:c2], jnp.zeros((d, PROJ_WIDTH - PROJ_DT - SSD_HEADS), F32)],
            axis=1).astype(BF16)
        proj = _mm(x, w_in_packed, norm_w=norm_mix[i])

        wkv = jnp.concatenate([wk_mem[i], wv_mem[i]], axis=1).astype(BF16)
        kv = _mm(mem_rows, wkv, norm_w=mem_norm_w[i])
        mk = kv[:, :MEM_WIDTH].reshape(bp, MEM_TOKENS, MEM_HEADS, MEM_HEAD_DIM)
        mv = kv[:, MEM_WIDTH:].reshape(bp, MEM_TOKENS, MEM_HEADS, MEM_HEAD_DIM)
        p_mk.append(mk)
        p_mv.append(mv)

        ys = []
        for grp, (r0, nb_, L, states) in enumerate((
                (0, bp, lp, zeros_states),
                (tp, bs, ls, (state_ssd[i], state_ssd_conv[i], state_rwkv[i], state_rwkv_shift[i],
                              state_s5_re[i], state_s5_im[i])))):
            pr = proj[r0:r0 + nb_ * L].reshape(nb_, L, PROJ_WIDTH)
            z = pr[..., PROJ_Z:PROJ_XBC]
            xbc = pr[..., PROJ_XBC:PROJ_HR]
            hr = pr[..., PROJ_HR:PROJ_U]
            u = pr[..., PROJ_U:PROJ_DT]
            dt = pr[..., PROJ_DT:PROJ_DT + SSD_HEADS]
            y_ssd, ssd_n, conv_n = _ssd_mixer(z, xbc, dt, states[0], states[1], ssd_conv_w[i], ssd_conv_b[i],
                                              ssd_dt_bias[i], ssd_a_log[i], ssd_d[i], ssd_norm_w[i])
            y_rw, wkv_n, shift_n = _rwkv7_mixer(hr, states[2], states[3], rwkv_mu[i], rwkv_w0[i], rwkv_w2[i],
                                                rwkv_a0[i], rwkv_a2[i], rwkv_g2[i], rwkv_k_k[i], rwkv_k_a[i],
                                                rwkv_r_k[i], rwkv_gn_w[i], rwkv_gn_b[i])
            y_s5, s5r_n, s5i_n = _s5_mixer(u, states[4], states[5], s5_lam_re[i], s5_lam_im[i], s5_b_re[i],
                                           s5_b_im[i], s5_c_re[i], s5_c_im[i], s5_d[i], s5_log_dt[i],
                                           s5_glu_w[i], s5_glu_b[i], s5_norm_w[i])
            ys.append(jnp.concatenate([y_ssd, y_rw, y_s5], axis=-1).reshape(nb_ * L, d))
            (p_st if grp == 0 else s_st).append((ssd_n, conv_n, wkv_n, shift_n, s5r_n, s5i_n))
        ymix = jnp.concatenate(ys, axis=0)
        x = _mm(ymix, w_out[i].astype(BF16), residual=x)

        q = _mm(x, wq_mem[i].astype(BF16), norm_w=norm_mem[i])
        o = jnp.concatenate([
            _mem_attend_core(q[:tp].reshape(bp, lp, MEM_WIDTH), mk, mv).reshape(tp, MEM_WIDTH),
            _mem_attend_core(q[tp:].reshape(bs, ls, MEM_WIDTH), cache_mem_k[i], cache_mem_v[i]).reshape(ts, MEM_WIDTH),
        ], axis=0)
        x = _mm(o, wo_mem[i].astype(BF16), residual=x)

        j = i // 2
        if i % 2 == 0:
            x = _ffn(x, norm_ffn[i], ffn_w1[j].astype(BF16), ffn_w3[j].astype(BF16), ffn_w2[j].astype(BF16))
        else:
            x = _moe(x, norm_ffn[i], moe_router_w[j], moe_router_b[j], moe_w1[j], moe_w3[j], moe_w2[j])

    y = _rms_rows(x, final_norm_w)
    y_prompt = y[:tp].reshape(bp, lp, d)
    y_sample = y[tp:].reshape(bs, ls, d)

    def stk(lst, j):
        return jnp.stack([s[j] for s in lst])

    return (y_prompt, y_sample, jnp.stack(p_mk), jnp.stack(p_mv),
            stk(p_st, 0), stk(p_st, 1), stk(p_st, 2), stk(p_st, 3), stk(p_st, 4), stk(p_st, 5),
            stk(s_st, 0), stk(s_st, 1), stk(s_st, 2), stk(s_st, 3), stk(s_st, 4), stk(s_st, 5))
```

```python
import functools

import jax
import jax.numpy as jnp
from jax import lax
from jax.experimental import pallas as pl
from jax.experimental.pallas import tpu as pltpu

D_MODEL = 2048
DEPTH = 2
SSD_WIDTH = 1024
SSD_HEAD_DIM = 64
SSD_HEADS = 16
SSD_GROUPS = 2
SSD_STATE = 128
SSD_CONV = 4
SSD_CONV_DIM = 1536
SSD_CHUNK = 64
RWKV_WIDTH = 512
RWKV_HEAD_DIM = 64
RWKV_HEADS = 8
RWKV_W_LORA = 64
RWKV_A_LORA = 64
RWKV_G_LORA = 128
RWKV_SHIFT_DIM = 1792
RWKV_GN_EPS = 64e-5
S5_WIDTH = 512
S5_GROUP = 16
S5_GROUPS = 32
S5_STATE = 64
S5_CHANNELS = S5_GROUPS * S5_STATE
MEM_TOKENS = 256
MEM_HEADS = 4
MEM_HEAD_DIM = 128
MEM_WIDTH = 512
N_EXPERTS = 8
TOP_K = 2
RMS_EPS = 1e-6

F32 = jnp.float32
BF16 = jnp.bfloat16
HIGHEST = lax.Precision.HIGHEST

PROJ_Z = 0
PROJ_XBC = PROJ_Z + SSD_WIDTH
PROJ_HR = PROJ_XBC + SSD_CONV_DIM
PROJ_U = PROJ_HR + RWKV_SHIFT_DIM
PROJ_DT = PROJ_U + S5_WIDTH
LANES = 128
SUBLANES = 8
PROJ_WIDTH = PROJ_DT + LANES

VMEM_LIMIT = 48 * 1024 * 1024
ROW_TILE = 512
MOE_ROW_TILE = 512


def _rms_rows(x, w):
    return x * lax.rsqrt(jnp.mean(x * x, axis=-1, keepdims=True) + RMS_EPS) * w


def _sigmoid(x):
    return 1.0 / (1.0 + jnp.exp(-x))


def _silu(x):
    return x * _sigmoid(x)


def _softplus(x):
    return jnp.maximum(x, 0.0) + jnp.log1p(jnp.exp(-jnp.abs(x)))


def _dot(a, b):
    return jnp.dot(a, b, preferred_element_type=F32)


def _dot_nt(a, b):
    return lax.dot_general(a, b, (((1,), (1,)), ((), ())), preferred_element_type=F32)


def _dot_exact(a, b):
    return jnp.dot(a, b, precision=HIGHEST, preferred_element_type=F32)


def _params(*sem):
    return pltpu.CompilerParams(dimension_semantics=sem, vmem_limit_bytes=VMEM_LIMIT)


def _mm_kernel(*refs, has_norm, has_res):
    refs = list(refs)
    a_ref = refs.pop(0)
    nw_ref = refs.pop(0) if has_norm else None
    w_ref = refs.pop(0)
    res_ref = refs.pop(0) if has_res else None
    o_ref = refs.pop(0)
    abf_ref = refs.pop(0)

    @pl.when(pl.program_id(1) == 0)
    def _():
        a = a_ref[...]
        if has_norm:
            a = _rms_rows(a, nw_ref[...])
        abf_ref[...] = a.astype(BF16)

    acc = _dot(abf_ref[...], w_ref[...])
    if has_res:
        acc = acc + res_ref[...]
    o_ref[...] = acc


def _col_tile(n):
    for t in (512, 384, 256, 128):
        if n % t == 0:
            return t
    raise ValueError(f"unsupported matmul width {n}")


def _mm(a, w, norm_w=None, residual=None):
    m, k = a.shape
    n = w.shape[1]
    tm, tn = ROW_TILE, _col_tile(n)
    assert m % tm == 0
    has_norm, has_res = norm_w is not None, residual is not None
    in_specs = [pl.BlockSpec((tm, k), lambda i, j: (i, 0))]
    args = [a]
    if has_norm:
        in_specs.append(pl.BlockSpec((1, k), lambda i, j: (0, 0)))
        args.append(norm_w.reshape(1, k))
    in_specs.append(pl.BlockSpec((k, tn), lambda i, j: (0, j)))
    args.append(w)
    if has_res:
        in_specs.append(pl.BlockSpec((tm, tn), lambda i, j: (i, j)))
        args.append(residual)
    return pl.pallas_call(
        functools.partial(_mm_kernel, has_norm=has_norm, has_res=has_res),
        out_shape=jax.ShapeDtypeStruct((m, n), F32),
        grid=(m // tm, n // tn),
        in_specs=in_specs,
        out_specs=pl.BlockSpec((tm, tn), lambda i, j: (i, j)),
        scratch_shapes=[pltpu.VMEM((tm, k), BF16)],
        compiler_params=_params("parallel", "arbitrary"),
        name="matmul",
    )(*args)


def _ffn_kernel(x_ref, nw_ref, w1_ref, w3_ref, w2_ref, o_ref, h_ref, acc_ref):
    f = pl.program_id(1)

    @pl.when(f == 0)
    def _():
        h_ref[...] = _rms_rows(x_ref[...], nw_ref[...]).astype(BF16)
        acc_ref[...] = jnp.zeros_like(acc_ref)

    h = h_ref[...]
    g = _dot(h, w1_ref[...])
    u = _dot(h, w3_ref[...])
    a = (g * jax.nn.sigmoid(g) * u).astype(BF16)
    acc_ref[...] += _dot(a, w2_ref[...])

    @pl.when(f == pl.num_programs(1) - 1)
    def _():
        o_ref[...] = x_ref[...] + acc_ref[...]


def _ffn(x, norm_w, w1, w3, w2):
    m, d = x.shape
    dff = w1.shape[1]
    tm, tf = ROW_TILE, 512
    return pl.pallas_call(
        _ffn_kernel,
        out_shape=jax.ShapeDtypeStruct((m, d), F32),
        grid=(m // tm, dff // tf),
        in_specs=[
            pl.BlockSpec((tm, d), lambda i, f: (i, 0)),
            pl.BlockSpec((1, d), lambda i, f: (0, 0)),
            pl.BlockSpec((d, tf), lambda i, f: (0, f)),
            pl.BlockSpec((d, tf), lambda i, f: (0, f)),
            pl.BlockSpec((tf, d), lambda i, f: (f, 0)),
        ],
        out_specs=pl.BlockSpec((tm, d), lambda i, f: (i, 0)),
        scratch_shapes=[pltpu.VMEM((tm, d), BF16), pltpu.VMEM((tm, d), F32)],
        compiler_params=_params("parallel", "arbitrary"),
        name="ffn_swiglu",
    )(x, norm_w.reshape(1, d), w1, w3, w2)


def _moe_kernel(be_ref, nv_ref, x_ref, w1_ref, w3_ref, w2_ref, o_ref, acc_ref):
    b, f = pl.program_id(0), pl.program_id(1)
    valid = b < nv_ref[0]

    @pl.when(f == 0)
    def _():
        acc_ref[...] = jnp.zeros_like(acc_ref)

    @pl.when(valid)
    def _():
        h = x_ref[...]
        g = _dot(h, w1_ref[0].astype(BF16))
        u = _dot(h, w3_ref[0].astype(BF16))
        a = (g * jax.nn.sigmoid(g) * u).astype(BF16)
        acc_ref[...] += _dot(a, w2_ref[0].astype(BF16))

    @pl.when(f == pl.num_programs(1) - 1)
    def _():
        o_ref[...] = acc_ref[...]


def _moe_experts(xb, block_e, n_valid, w1, w3, w2):
    cap, d = xb.shape
    dff = w1.shape[2]
    tm, tf = MOE_ROW_TILE, 256
    nb, nf = cap // tm, dff // tf

    def w13_map(b, f, be, nv):
        ok = b < nv[0]
        return (be[b], 0, jnp.where(ok, f, nf - 1))

    def w2_map(b, f, be, nv):
        ok = b < nv[0]
        return (be[b], jnp.where(ok, f, nf - 1), 0)

    grid_spec = pltpu.PrefetchScalarGridSpec(
        num_scalar_prefetch=2,
        grid=(nb, nf),
        in_specs=[
            pl.BlockSpec((tm, d), lambda b, f, be, nv: (b, 0)),
            pl.BlockSpec((1, d, tf), w13_map),
            pl.BlockSpec((1, d, tf), w13_map),
            pl.BlockSpec((1, tf, d), w2_map),
        ],
        out_specs=pl.BlockSpec((tm, d), lambda b, f, be, nv: (b, 0)),
        scratch_shapes=[pltpu.VMEM((tm, d), F32)],
    )
    return pl.pallas_call(
        _moe_kernel,
        out_shape=jax.ShapeDtypeStruct((cap, d), F32),
        grid_spec=grid_spec,
        compiler_params=_params("arbitrary", "arbitrary"),
        name="moe_swiglu",
    )(block_e, n_valid, xb, w1, w3, w2)


def _moe(x, norm_w, w_router, b_router, w1, w3, w2):
    t, d = x.shape
    tm = MOE_ROW_TILE
    h = _rms_rows(x, norm_w)
    logits = jnp.dot(h, w_router, precision=HIGHEST) + b_router
    top_v, top_i = lax.top_k(logits, TOP_K)
    gates = jax.nn.softmax(top_v, axis=-1)
    n_slots = t * TOP_K
    flat_e = top_i.reshape(-1).astype(jnp.int32)
    order = jnp.argsort(flat_e)
    se = flat_e[order]
    counts = jnp.bincount(flat_e, length=N_EXPERTS).astype(jnp.int32)
    starts = jnp.cumsum(counts) - counts
    padded = (counts + tm - 1) // tm * tm
    pad_ends = jnp.cumsum(padded)
    pad_starts = pad_ends - padded
    dest_sorted = pad_starts[se] + jnp.arange(n_slots, dtype=jnp.int32) - starts[se]
    nb = n_slots // tm + N_EXPERTS
    cap = nb * tm
    slot_pos = jnp.zeros((n_slots,), jnp.int32).at[order].set(dest_sorted)
    buf_tok = jnp.zeros((cap,), jnp.int32).at[dest_sorted].set(order // TOP_K)
    block_e = jnp.minimum(
        jnp.searchsorted(pad_ends, jnp.arange(nb, dtype=jnp.int32) * tm, side='right'),
        N_EXPERTS - 1).astype(jnp.int32)
    n_valid = (pad_ends[-1] // tm).astype(jnp.int32).reshape(1)
    xb = h.astype(BF16)[buf_tok]
    yb = _moe_experts(xb, block_e, n_valid, w1, w3, w2)
    pos = slot_pos.reshape(t, TOP_K)
    y = yb[pos[:, 0]] * gates[:, 0:1] + yb[pos[:, 1]] * gates[:, 1:2]
    return x + y


def _s5_kernel(u_ref, sre_ref, sim_ref, are_ref, aim_ref, bbre_ref, bbim_ref, cre_ref, cim_ref, d_ref,
               gw_ref, gb_ref, nw_ref, y_ref, ore_ref, oim_ref, xr_ref, xi_ref, st_ref, *, tc, nbb):
    c = pl.program_id(1)
    rows = tc * nbb
    u = u_ref[...].reshape(rows, S5_WIDTH)
    ub = u.astype(BF16)
    xr_ref[...] = _dot(ub, bbre_ref[...])
    xi_ref[...] = _dot(ub, bbim_ref[...])

    @pl.when(c == 0)
    def _():
        st_ref[0] = sre_ref[...]
        st_ref[1] = sim_ref[...]

    ar = are_ref[...]
    ai = aim_ref[...]

    def step(t, carry):
        for g in range(nbb // SUBLANES):
            r0 = pl.multiple_of(t * nbb + g * SUBLANES, SUBLANES)
            sl = slice(g * SUBLANES, (g + 1) * SUBLANES)
            pr = st_ref[0, sl, :]
            pi = st_ref[1, sl, :]
            nr = ar * pr - ai * pi + xr_ref[pl.ds(r0, SUBLANES), :]
            ni = ar * pi + ai * pr + xi_ref[pl.ds(r0, SUBLANES), :]
            xr_ref[pl.ds(r0, SUBLANES), :] = nr
            xi_ref[pl.ds(r0, SUBLANES), :] = ni
            st_ref[0, sl, :] = nr
            st_ref[1, sl, :] = ni
        return carry

    lax.fori_loop(0, tc, step, 0)

    y = _dot(xr_ref[...].astype(BF16), cre_ref[...]) - _dot(xi_ref[...].astype(BF16), cim_ref[...])
    y = y + d_ref[...] * u
    gy = 0.5 * y * (1.0 + jnp.tanh(0.7978845608028654 * (y + 0.044715 * (y * y * y))))
    y = gy * _sigmoid(_dot(gy.astype(BF16), gw_ref[...]) + gb_ref[...])
    y_ref[...] = _rms_rows(y, nw_ref[...]).reshape(tc, nbb, S5_WIDTH)

    @pl.when(c == pl.num_programs(1) - 1)
    def _():
        ore_ref[...] = st_ref[0]
        oim_ref[...] = st_ref[1]


def _s5(u_tm, st_re, st_im, prm, *, tc, nbb):
    L, n, _ = u_tm.shape
    ch = S5_CHANNELS
    vec = lambda w: pl.BlockSpec((1, w), lambda s, c: (0, 0))
    mat = lambda a, b: pl.BlockSpec((a, b), lambda s, c: (0, 0))
    st_spec = pl.BlockSpec((nbb, ch), lambda s, c: (s, 0))
    return pl.pallas_call(
        functools.partial(_s5_kernel, tc=tc, nbb=nbb),
        out_shape=(jax.ShapeDtypeStruct((L, n, S5_WIDTH), F32),
                   jax.ShapeDtypeStruct((n, ch), F32), jax.ShapeDtypeStruct((n, ch), F32)),
        grid=(n // nbb, L // tc),
        in_specs=[pl.BlockSpec((tc, nbb, S5_WIDTH), lambda s, c: (c, s, 0)), st_spec, st_spec,
                  vec(ch), vec(ch), mat(S5_WIDTH, ch), mat(S5_WIDTH, ch), mat(ch, S5_WIDTH), mat(ch, S5_WIDTH),
                  vec(S5_WIDTH), mat(S5_WIDTH, S5_WIDTH), vec(S5_WIDTH), vec(S5_WIDTH)],
        out_specs=(pl.BlockSpec((tc, nbb, S5_WIDTH), lambda s, c: (c, s, 0)), st_spec, st_spec),
        scratch_shapes=[pltpu.VMEM((tc * nbb, ch), F32), pltpu.VMEM((tc * nbb, ch), F32),
                        pltpu.VMEM((2, nbb, ch), F32)],
        compiler_params=_params("parallel", "arbitrary"),
        name="s5_mixer",
    )(u_tm, st_re, st_im, *prm)


def _s5_params(lam_re, lam_im, b_re, b_im, c_re, c_im, d_skip, log_dt, glu_w, glu_b, norm_w):
    delta = jnp.exp(log_dt)[:, None]
    mag = jnp.exp(lam_re * delta)
    ab_re, ab_im = mag * jnp.cos(lam_im * delta), mag * jnp.sin(lam_im * delta)
    den = lam_re * lam_re + lam_im * lam_im
    q_re = ((ab_re - 1.0) * lam_re + ab_im * lam_im) / den
    q_im = (ab_im * lam_re - (ab_re - 1.0) * lam_im) / den
    bb_re = q_re[..., None] * b_re - q_im[..., None] * b_im
    bb_im = q_re[..., None] * b_im + q_im[..., None] * b_re
    eye = jnp.eye(S5_GROUPS, dtype=F32)

    def in_blockdiag(bb):
        t = jnp.swapaxes(bb, 1, 2)
        return (eye[:, None, :, None] * t[:, :, None, :]).reshape(S5_WIDTH, S5_CHANNELS).astype(BF16)

    def out_blockdiag(cc):
        t = jnp.swapaxes(cc, 1, 2)
        return (eye[:, None, :, None] * t[:, :, None, :]).reshape(S5_CHANNELS, S5_WIDTH).astype(BF16)

    return (ab_re.reshape(1, S5_CHANNELS), ab_im.reshape(1, S5_CHANNELS), in_blockdiag(bb_re), in_blockdiag(bb_im),
            out_blockdiag(c_re), out_blockdiag(c_im), d_skip.reshape(1, S5_WIDTH), glu_w.astype(BF16),
            glu_b.reshape(1, S5_WIDTH), norm_w.reshape(1, S5_WIDTH))


def _ssd_kernel(p_ref, st_ref, cs_ref, cw_ref, cb_ref, dtb_ref, a_ref, dsk_ref, nw_ref,
                y_ref, ost_ref, ocs_ref, ext_ref, win_ref, s_ref, *, q, sb):
    c = pl.program_id(1)
    last = c == pl.num_programs(1) - 1
    hd, nh, gw = SSD_HEAD_DIM, SSD_HEADS, SSD_WIDTH // SSD_GROUPS
    pad_rows = hd - q

    lane = lax.broadcasted_iota(jnp.int32, (q, LANES), 1)
    row = lax.broadcasted_iota(jnp.int32, (q, LANES), 0)
    causal2 = row >= (lane % hd)
    lane64 = lax.broadcasted_iota(jnp.int32, (hd, LANES), 1)
    tri = (lax.broadcasted_iota(jnp.int32, (q, q), 0) >= lax.broadcasted_iota(jnp.int32, (q, q), 1)).astype(F32)
    e_h = lax.broadcasted_iota(jnp.int32, (LANES, SSD_WIDTH), 0)
    e_c = lax.broadcasted_iota(jnp.int32, (LANES, SSD_WIDTH), 1)
    expand = (e_h == e_c // hd).astype(F32)
    i_s = lax.broadcasted_iota(jnp.int32, (q, SSD_WIDTH), 0)
    i_c = lax.broadcasted_iota(jnp.int32, (q, SSD_WIDTH), 1)
    eye_x = (i_s == i_c % hd).astype(F32)

    for s in range(sb):
        rs = slice(s * q, (s + 1) * q)

        @pl.when(c == 0)
        def _():
            win_ref[s, 0:5, :] = jnp.zeros((5, SSD_CONV_DIM), F32)
            win_ref[s, 5:8, :] = cs_ref[s]
            for g in range(SSD_GROUPS):
                for k in range(gw // LANES):
                    r0 = g * gw + k * LANES
                    s_ref[s, g, :, k * LANES:(k + 1) * LANES] = st_ref[s, r0:r0 + LANES, :].T

        z = p_ref[rs, PROJ_Z:PROJ_XBC]
        xbc = p_ref[rs, PROJ_XBC:PROJ_HR]
        dt = p_ref[rs, PROJ_DT:PROJ_WIDTH]
        ext_ref[0:8, :] = win_ref[s]
        ext_ref[8:8 + q, :] = xbc
        conv = cb_ref[...]
        for j in range(SSD_CONV):
            conv = conv + cw_ref[j:j + 1, :] * ext_ref[pl.ds(5 + j, q), :]
        win_ref[s] = ext_ref[q:q + 8, :]
        xc = _silu(conv)
        xs = xc[:, :SSD_WIDTH]
        bm = xc[:, SSD_WIDTH:SSD_WIDTH + SSD_GROUPS * SSD_STATE]
        cm = xc[:, SSD_WIDTH + SSD_GROUPS * SSD_STATE:]

        step = _softplus(dt + dtb_ref[...])
        adt = step * a_ref[...]
        step_x = _dot_exact(step, expand)
        acs_x = _dot_exact(tri, _dot_exact(adt, expand))
        diag = jnp.sum(acs_x * eye_x, axis=0, keepdims=True)
        acs_last = acs_x[q - 1:q, :]
        xdt = xs * step_x
        exp_acs = jnp.exp(acs_x)
        xw = xdt * jnp.exp(acs_last - acs_x)
        dec = jnp.exp(acs_last)

        for g in range(SSD_GROUPS):
            bg = bm[:, g * SSD_STATE:(g + 1) * SSD_STATE]
            cg = cm[:, g * SSD_STATE:(g + 1) * SSD_STATE].astype(BF16)
            gl = slice(g * gw, (g + 1) * gw)
            b64 = bg if pad_rows == 0 else jnp.concatenate([bg, jnp.zeros((pad_rows, SSD_STATE), F32)], axis=0)
            cb2 = _dot_nt(cg, jnp.concatenate([b64, b64], axis=0).astype(BF16))
            sg = s_ref[s, g]
            yoff = _dot(cg, sg.astype(BF16)) * exp_acs[:, gl]
            for pr in range(gw // LANES):
                l0 = g * gw + pr * LANES
                seg = acs_x[:, l0:l0 + LANES] - diag[:, l0:l0 + LANES]
                m = cb2 * jnp.exp(jnp.where(causal2, seg, -jnp.inf))
                xd = xdt[:, l0:l0 + LANES]
                xd64 = xd if pad_rows == 0 else jnp.concatenate([xd, jnp.zeros((pad_rows, LANES), F32)], axis=0)
                rhs = jnp.concatenate([jnp.where(lane64 < hd, xd64, 0.0), jnp.where(lane64 >= hd, xd64, 0.0)], axis=0)
                ydiag = _dot(m.astype(BF16), rhs.astype(BF16))
                y_ref[rs, l0:l0 + LANES] = ydiag + yoff[:, pr * LANES:(pr + 1) * LANES]
            bpad = jnp.concatenate([bg, jnp.zeros((LANES - q, SSD_STATE), F32)], axis=0)
            xwpad = jnp.concatenate([xw[:, gl], jnp.zeros((LANES - q, gw), F32)], axis=0)
            s_ref[s, g] = dec[:, gl] * sg + _dot(bpad.T.astype(BF16), xwpad.astype(BF16))

        y = y_ref[rs, :] + dsk_ref[...] * xs
        y = y * _silu(z)
        halves = []
        for g in range(SSD_GROUPS):
            yg = y[:, g * gw:(g + 1) * gw]
            halves.append(yg * lax.rsqrt(jnp.mean(yg * yg, axis=-1, keepdims=True) + RMS_EPS))
        y_ref[rs, :] = jnp.concatenate(halves, axis=1) * nw_ref[...]

        @pl.when(last)
        def _():
            ocs_ref[s] = ext_ref[q + 5:q + 8, :]
            for g in range(SSD_GROUPS):
                for k in range(gw // LANES):
                    r0 = g * gw + k * LANES
                    ost_ref[s, r0:r0 + LANES, :] = s_ref[s, g, :, k * LANES:(k + 1) * LANES].T


def _ssd(proj, row0, nseq, L, st, cs, prm, *, sb):
    q = min(L, SSD_CHUNK)
    nchunk = L // q
    rows = sb * q
    base = row0 // rows
    assert row0 % rows == 0 and nseq % sb == 0
    vec = lambda w: pl.BlockSpec((1, w), lambda s, c: (0, 0))
    st_spec = pl.BlockSpec((sb, SSD_WIDTH, SSD_STATE), lambda s, c: (s, 0, 0))
    cs_spec = pl.BlockSpec((sb, SSD_CONV - 1, SSD_CONV_DIM), lambda s, c: (s, 0, 0))
    return pl.pallas_call(
        functools.partial(_ssd_kernel, q=q, sb=sb),
        out_shape=(jax.ShapeDtypeStruct((nseq * L, SSD_WIDTH), F32),
                   jax.ShapeDtypeStruct((nseq, SSD_WIDTH, SSD_STATE), F32),
                   jax.ShapeDtypeStruct((nseq, SSD_CONV - 1, SSD_CONV_DIM), F32)),
        grid=(nseq // sb, nchunk),
        in_specs=[pl.BlockSpec((rows, PROJ_WIDTH), lambda s, c: (base + s * nchunk + c, 0)), st_spec, cs_spec,
                  pl.BlockSpec((SSD_CONV, SSD_CONV_DIM), lambda s, c: (0, 0)), vec(SSD_CONV_DIM),
                  vec(LANES), vec(LANES), vec(SSD_WIDTH), vec(SSD_WIDTH)],
        out_specs=(pl.BlockSpec((rows, SSD_WIDTH), lambda s, c: (s * nchunk + c, 0)), st_spec, cs_spec),
        scratch_shapes=[pltpu.VMEM((q + 8, SSD_CONV_DIM), F32), pltpu.VMEM((sb, 8, SSD_CONV_DIM), F32),
                        pltpu.VMEM((sb, SSD_GROUPS, SSD_STATE, SSD_WIDTH // SSD_GROUPS), F32)],
        compiler_params=_params("parallel", "arbitrary"),
        name="ssd_mixer",
    )(proj, st, cs, *prm)


def _ssd_params(conv_w, conv_b, dt_bias, a_log, d_skip, norm_w):
    pad = jnp.zeros((LANES - SSD_HEADS,), F32)
    return (conv_w, conv_b.reshape(1, SSD_CONV_DIM), jnp.concatenate([dt_bias, pad]).reshape(1, LANES),
            jnp.concatenate([-jnp.exp(a_log), pad]).reshape(1, LANES),
            jnp.repeat(d_skip, SSD_HEAD_DIM).reshape(1, SSD_WIDTH), norm_w.reshape(1, SSD_WIDTH))


def _attn_kernel(q_ref, k_ref, v_ref, o_ref, *, lq, sb):
    scale = MEM_HEAD_DIM ** -0.5
    for s in range(sb):
        rq = slice(s * lq, (s + 1) * lq)
        rk = slice(s * MEM_TOKENS, (s + 1) * MEM_TOKENS)
        for h in range(MEM_HEADS):
            cl = slice(h * MEM_HEAD_DIM, (h + 1) * MEM_HEAD_DIM)
            sc = _dot_nt(q_ref[rq, cl].astype(BF16), k_ref[rk, cl].astype(BF16)) * scale
            sc = sc - jnp.max(sc, axis=-1, keepdims=True)
            p = jnp.exp(sc)
            p = p / jnp.sum(p, axis=-1, keepdims=True)
            o_ref[rq, cl] = _dot(p.astype(BF16), v_ref[rk, cl].astype(BF16))


def _attend(q, row0, nseq, L, k2d, v2d, kcol, vcol, *, lq, sb):
    nl = L // lq
    rows = sb * lq
    base = row0 // rows
    assert row0 % rows == 0 and (sb == 1 or nl == 1)
    return pl.pallas_call(
        functools.partial(_attn_kernel, lq=lq, sb=sb),
        out_shape=jax.ShapeDtypeStruct((nseq * L, MEM_WIDTH), F32),
        grid=(nseq // sb, nl),
        in_specs=[pl.BlockSpec((rows, MEM_WIDTH), lambda s, l: (base + s * nl + l, 0)),
                  pl.BlockSpec((sb * MEM_TOKENS, MEM_WIDTH), lambda s, l: (s, kcol)),
                  pl.BlockSpec((sb * MEM_TOKENS, MEM_WIDTH), lambda s, l: (s, vcol))],
        out_specs=pl.BlockSpec((rows, MEM_WIDTH), lambda s, l: (s * nl + l, 0)),
        compiler_params=_params("parallel", "arbitrary"),
        name="mem_attention",
    )(q, k2d, v2d)


def _head_sum(x, ones_bd):
    return _dot_exact(x, ones_bd)


def _rwkv_prep_kernel(h_ref, p_ref, mu_ref, wl_ref, w0_ref, a0_ref, kk_ref, ka_ref, rk_ref, ones_ref,
                      r_ref, w_ref, k_ref, v_ref, n_ref, b_ref, g_ref, bo_ref):
    W = RWKV_WIDTH
    h = h_ref[...]
    hs = h + (p_ref[...] - h) * mu_ref[...]
    r, k, v = hs[:, :W], hs[:, W:2 * W], hs[:, 2 * W:3 * W]
    lo = hs[:, 3 * W:]
    lane = lax.broadcasted_iota(jnp.int32, lo.shape, 1)
    act = jnp.where(lane < RWKV_W_LORA, jnp.tanh(lo),
                    jnp.where(lane < RWKV_W_LORA + RWKV_A_LORA, lo, _sigmoid(lo)))
    lora = _dot(act.astype(BF16), wl_ref[...])
    w_log = -_softplus(-(w0_ref[...] + lora[:, :W])) - 0.5
    a = _sigmoid(a0_ref[...] + lora[:, W:2 * W])
    ones_bd = ones_ref[...]
    kk = k * kk_ref[...]
    kk = kk / jnp.maximum(jnp.sqrt(_head_sum(kk * kk, ones_bd)), 1e-12)
    k2 = k * (1.0 + (a - 1.0) * ka_ref[...])
    r_ref[...] = r
    w_ref[...] = jnp.exp(-jnp.exp(w_log))
    k_ref[...] = k2
    v_ref[...] = v
    n_ref[...] = kk
    b_ref[...] = kk * a
    g_ref[...] = lora[:, 2 * W:]
    bo_ref[...] = _head_sum(r * k2 * rk_ref[...], ones_bd) * v


def _rwkv_prep(hr, prev, prm):
    m = hr.shape[0]
    tm, W = ROW_TILE, RWKV_WIDTH
    mu, wl, w0, a0, k_k, k_a, r_k, ones_bd = prm
    vec = lambda w: pl.BlockSpec((1, w), lambda i: (0, 0))
    row = lambda w: pl.BlockSpec((tm, w), lambda i: (i, 0))
    return pl.pallas_call(
        _rwkv_prep_kernel,
        out_shape=tuple(jax.ShapeDtypeStruct((m, W), F32) for _ in range(8)),
        grid=(m // tm,),
        in_specs=[row(RWKV_SHIFT_DIM), row(RWKV_SHIFT_DIM), vec(RWKV_SHIFT_DIM),
                  pl.BlockSpec(wl.shape, lambda i: (0, 0)), vec(W), vec(W), vec(W), vec(W), vec(W),
                  pl.BlockSpec((W, W), lambda i: (0, 0))],
        out_specs=tuple(row(W) for _ in range(8)),
        compiler_params=_params("parallel"),
        name="rwkv_prep",
    )(hr, prev, mu, wl, w0, a0, k_k, k_a, r_k, ones_bd)


def _rwkv_post_kernel(y_ref, bo_ref, g_ref, gw_ref, gb_ref, ones_ref, o_ref):
    ones_bd = ones_ref[...]
    y = y_ref[...]
    inv = 1.0 / RWKV_HEAD_DIM
    d = y - _head_sum(y, ones_bd) * inv
    var = _head_sum(d * d, ones_bd) * inv
    yn = d * lax.rsqrt(var + RWKV_GN_EPS) * gw_ref[...] + gb_ref[...]
    o_ref[...] = (yn + bo_ref[...]) * g_ref[...]


def _rwkv_post(y, bonus, g, gn_w, gn_b, ones_bd):
    m, W = y.shape
    tm = ROW_TILE
    vec = pl.BlockSpec((1, W), lambda i: (0, 0))
    row = pl.BlockSpec((tm, W), lambda i: (i, 0))
    return pl.pallas_call(
        _rwkv_post_kernel,
        out_shape=jax.ShapeDtypeStruct((m, W), F32),
        grid=(m // tm,),
        in_specs=[row, row, row, vec, vec, pl.BlockSpec((W, W), lambda i: (0, 0))],
        out_specs=row,
        compiler_params=_params("parallel"),
        name="rwkv_post",
    )(y, bonus, g, gn_w.reshape(1, W), gn_b.reshape(1, W), ones_bd)


def _rwkv_params(mu, w0, w2, a0, a2, g2, k_k, k_a, r_k):
    W = RWKV_WIDTH
    nl = RWKV_W_LORA + RWKV_A_LORA + RWKV_G_LORA
    wl = jnp.zeros((nl, 3 * W), F32)
    wl = wl.at[:RWKV_W_LORA, :W].set(w2)
    wl = wl.at[RWKV_W_LORA:RWKV_W_LORA + RWKV_A_LORA, W:2 * W].set(a2)
    wl = wl.at[RWKV_W_LORA + RWKV_A_LORA:, 2 * W:].set(g2)
    head = jnp.arange(W) // RWKV_HEAD_DIM
    ones_bd = (head[:, None] == head[None, :]).astype(F32)
    v = lambda t: t.reshape(1, -1)
    return (v(mu), wl.astype(BF16), v(w0), v(a0), v(k_k), v(k_a), v(r_k), ones_bd)


def _rwkv_scan(r, w, k, v, kk, b, state):
    nseq, L, _ = r.shape

    def heads(t):
        return jnp.moveaxis(t.reshape(nseq, L, RWKV_HEADS, RWKV_HEAD_DIM), 1, 0)

    def step(S, inp):
        rt, wt, kt, vt, kkt, bt = inp
        sa = jnp.einsum('bhvk,bhk->bhv', S, -kkt)
        S = S * wt[:, :, None, :] + sa[..., None] * bt[:, :, None, :] + vt[..., None] * kt[:, :, None, :]
        return S, jnp.einsum('bhvk,bhk->bhv', S, rt)

    S_fin, ys = lax.scan(step, state, (heads(r), heads(w), heads(k), heads(v), heads(kk), heads(b)))
    return jnp.moveaxis(ys, 0, 1).reshape(nseq * L, RWKV_WIDTH), S_fin


def kernel(x_prompt, x_sample, mem_prompt, cache_mem_k, cache_mem_v, state_ssd, state_ssd_conv, state_rwkv, state_rwkv_shift, state_s5_re, state_s5_im, norm_mix, w_in, ssd_conv_w, ssd_conv_b, ssd_dt_bias, ssd_a_log, ssd_d, ssd_norm_w, rwkv_mu, rwkv_w0, rwkv_w2, rwkv_a0, rwkv_a2, rwkv_g2, rwkv_k_k, rwkv_k_a, rwkv_r_k, rwkv_gn_w, rwkv_gn_b, s5_lam_re, s5_lam_im, s5_b_re, s5_b_im, s5_c_re, s5_c_im, s5_d, s5_log_dt, s5_glu_w, s5_glu_b, s5_norm_w, w_out, norm_mem, mem_norm_w, wq_mem, wk_mem, wv_mem, wo_mem, norm_ffn, ffn_w1, ffn_w3, ffn_w2, moe_router_w, moe_router_b, moe_w1, moe_w3, moe_w2, final_norm_w):
    bp, lp, d = x_prompt.shape
    bs, ls, _ = x_sample.shape
    tp, ts = bp * lp, bs * ls
    x = jnp.concatenate([x_prompt.reshape(tp, d), x_sample.reshape(ts, d)], axis=0)
    mem_rows = mem_prompt.reshape(bp * MEM_TOKENS, d)
    s5_pad = SUBLANES - bp

    p_mk, p_mv, p_st, s_st = [], [], [], []
    for i in range(DEPTH):
        c0 = SSD_WIDTH
        c1 = c0 + SSD_CONV_DIM
        c2 = c1 + SSD_HEADS
        wi = w_in[i]
        w_in_packed = jnp.concatenate(
            [wi[:, :c1], wi[:, c2:], wi[:, c1:c2], jnp.zeros((d, PROJ_WIDTH - PROJ_DT - SSD_HEADS), F32)],
            axis=1).astype(BF16)
        proj = _mm(x, w_in_packed, norm_w=norm_mix[i])

        wkv = jnp.concatenate([wk_mem[i], wv_mem[i]], axis=1).astype(BF16)
        kv = _mm(mem_rows, wkv, norm_w=mem_norm_w[i])
        p_mk.append(kv[:, :MEM_WIDTH].reshape(bp, MEM_TOKENS, MEM_HEADS, MEM_HEAD_DIM))
        p_mv.append(kv[:, MEM_WIDTH:].reshape(bp, MEM_TOKENS, MEM_HEADS, MEM_HEAD_DIM))

        ssd_prm = _ssd_params(ssd_conv_w[i], ssd_conv_b[i], ssd_dt_bias[i], ssd_a_log[i], ssd_d[i], ssd_norm_w[i])
        y_ssd_p, ssd_p, conv_p = _ssd(
            proj, 0, bp, lp, jnp.zeros((bp, SSD_WIDTH, SSD_STATE), F32),
            jnp.zeros((bp, SSD_CONV - 1, SSD_CONV_DIM), F32), ssd_prm, sb=1)
        y_ssd_s, ssd_s, conv_s = _ssd(
            proj, tp, bs, ls, state_ssd[i].reshape(bs, SSD_WIDTH, SSD_STATE), state_ssd_conv[i], ssd_prm, sb=8)

        s5_prm = _s5_params(s5_lam_re[i], s5_lam_im[i], s5_b_re[i], s5_b_im[i], s5_c_re[i], s5_c_im[i], s5_d[i],
                            s5_log_dt[i], s5_glu_w[i], s5_glu_b[i], s5_norm_w[i])
        u = proj[:, PROJ_U:PROJ_DT]
        u_p = jnp.pad(jnp.swapaxes(u[:tp].reshape(bp, lp, S5_WIDTH), 0, 1), ((0, 0), (0, s5_pad), (0, 0)))
        zst = jnp.zeros((SUBLANES, S5_CHANNELS), F32)
        y5_p, s5r_p, s5i_p = _s5(u_p, zst, zst, s5_prm, tc=64, nbb=SUBLANES)
        u_s = jnp.swapaxes(u[tp:].reshape(bs, ls, S5_WIDTH), 0, 1)
        y5_s, s5r_s, s5i_s = _s5(u_s, state_s5_re[i].reshape(bs, S5_CHANNELS),
                                 state_s5_im[i].reshape(bs, S5_CHANNELS), s5_prm, tc=ls, nbb=64)
        y_s5 = jnp.concatenate([jnp.swapaxes(y5_p[:, :bp], 0, 1).reshape(tp, S5_WIDTH),
                                jnp.swapaxes(y5_s, 0, 1).reshape(ts, S5_WIDTH)], axis=0)

        rw_prm = _rwkv_params(rwkv_mu[i], rwkv_w0[i], rwkv_w2[i], rwkv_a0[i], rwkv_a2[i], rwkv_g2[i],
                              rwkv_k_k[i], rwkv_k_a[i], rwkv_r_k[i])
        hr = proj[:, PROJ_HR:PROJ_U]
        hr_p = hr[:tp].reshape(bp, lp, RWKV_SHIFT_DIM)
        hr_s = hr[tp:].reshape(bs, ls, RWKV_SHIFT_DIM)
        prev = jnp.concatenate([
            jnp.concatenate([jnp.zeros((bp, 1, RWKV_SHIFT_DIM), F32), hr_p[:, :-1]], axis=1).reshape(tp, -1),
            jnp.concatenate([state_rwkv_shift[i][:, None], hr_s[:, :-1]], axis=1).reshape(ts, -1)], axis=0)
        r_, w_, k_, v_, kk_, b_, g_, bonus = _rwkv_prep(hr, prev, rw_prm)
        W = RWKV_WIDTH
        yp_, wkv_p = _rwkv_scan(*(t[:tp].reshape(bp, lp, W) for t in (r_, w_, k_, v_, kk_, b_)),
                                jnp.zeros((bp, RWKV_HEADS, RWKV_HEAD_DIM, RWKV_HEAD_DIM), F32))
        ys_, wkv_s = _rwkv_scan(*(t[tp:].reshape(bs, ls, W) for t in (r_, w_, k_, v_, kk_, b_)), state_rwkv[i])
        y_rw = _rwkv_post(jnp.concatenate([yp_, ys_], axis=0), bonus, g_, rwkv_gn_w[i], rwkv_gn_b[i], rw_prm[-1])

        p_st.append((ssd_p.reshape(bp, SSD_HEADS, SSD_HEAD_DIM, SSD_STATE), conv_p, wkv_p, hr_p[:, -1],
                     s5r_p[:bp].reshape(bp, S5_GROUPS, S5_STATE), s5i_p[:bp].reshape(bp, S5_GROUPS, S5_STATE)))
        s_st.append((ssd_s.reshape(bs, SSD_HEADS, SSD_HEAD_DIM, SSD_STATE), conv_s, wkv_s, hr_s[:, -1],
                     s5r_s.reshape(bs, S5_GROUPS, S5_STATE), s5i_s.reshape(bs, S5_GROUPS, S5_STATE)))

        ymix = jnp.concatenate([jnp.concatenate([y_ssd_p, y_ssd_s], axis=0), y_rw, y_s5], axis=1)
        x = _mm(ymix, w_out[i].astype(BF16), residual=x)

        q = _mm(x, wq_mem[i].astype(BF16), norm_w=norm_mem[i])
        o = jnp.concatenate([
            _attend(q, 0, bp, lp, kv, kv, 0, 1, lq=512, sb=1),
            _attend(q, tp, bs, ls, cache_mem_k[i].reshape(bs * MEM_TOKENS, MEM_WIDTH),
                    cache_mem_v[i].reshape(bs * MEM_TOKENS, MEM_WIDTH), 0, 0, lq=ls, sb=8)], axis=0)
        x = _mm(o, wo_mem[i].astype(BF16), residual=x)

        j = i // 2
        if i % 2 == 0:
            x = _ffn(x, norm_ffn[i], ffn_w1[j].astype(BF16), ffn_w3[j].astype(BF16), ffn_w2[j].astype(BF16))
        else:
            x = _moe(x, norm_ffn[i], moe_router_w[j], moe_router_b[j], moe_w1[j], moe_w3[j], moe_w2[j])

    y = _rms_rows(x, final_norm_w)
    y_prompt = y[:tp].reshape(bp, lp, d)
    y_sample = y[tp:].reshape(bs, ls, d)

    def stk(lst, j):
        return jnp.stack([s[j] for s in lst])

    return (y_prompt, y_sample, jnp.stack(p_mk), jnp.stack(p_mv),
            stk(p_st, 0), stk(p_st, 1), stk(p_st, 2), stk(p_st, 3), stk(p_st, 4), stk(p_st, 5),
            stk(s_st, 0), stk(s_st, 1), stk(s_st, 2), stk(s_st, 3), stk(s_st, 4), stk(s_st, 5))
```

```python
import functools

import jax
import jax.numpy as jnp
from jax import lax
from jax.experimental import pallas as pl
from jax.experimental.pallas import tpu as pltpu

D_MODEL = 2048
DEPTH = 2
SSD_WIDTH = 1024
SSD_HEAD_DIM = 64
SSD_HEADS = 16
SSD_GROUPS = 2
SSD_STATE = 128
SSD_CONV = 4
SSD_CONV_DIM = 1536
SSD_CHUNK = 64
RWKV_WIDTH = 512
RWKV_HEAD_DIM = 64
RWKV_HEADS = 8
RWKV_W_LORA = 64
RWKV_A_LORA = 64
RWKV_G_LORA = 128
RWKV_SHIFT_DIM = 1792
RWKV_GN_EPS = 64e-5
S5_WIDTH = 512
S5_GROUP = 16
S5_GROUPS = 32
S5_STATE = 64
S5_CHANNELS = S5_GROUPS * S5_STATE
MEM_TOKENS = 256
MEM_HEADS = 4
MEM_HEAD_DIM = 128
MEM_WIDTH = 512
N_EXPERTS = 8
TOP_K = 2
RMS_EPS = 1e-6

F32 = jnp.float32
BF16 = jnp.bfloat16
HIGHEST = lax.Precision.HIGHEST

PROJ_Z = 0
PROJ_XBC = PROJ_Z + SSD_WIDTH
PROJ_HR = PROJ_XBC + SSD_CONV_DIM
PROJ_U = PROJ_HR + RWKV_SHIFT_DIM
PROJ_DT = PROJ_U + S5_WIDTH
LANES = 128
SUBLANES = 8
PROJ_WIDTH = PROJ_DT + LANES

VMEM_LIMIT = 48 * 1024 * 1024
ROW_TILE = 512
MOE_ROW_TILE = 512


def _rms_rows(x, w):
    return x * lax.rsqrt(jnp.mean(x * x, axis=-1, keepdims=True) + RMS_EPS) * w


def _sigmoid(x):
    return 1.0 / (1.0 + jnp.exp(-x))


def _silu(x):
    return x * _sigmoid(x)


def _softplus(x):
    return jnp.maximum(x, 0.0) + jnp.log1p(jnp.exp(-jnp.abs(x)))


def _dot(a, b):
    return jnp.dot(a, b, preferred_element_type=F32)


def _dot_nt(a, b):
    return lax.dot_general(a, b, (((1,), (1,)), ((), ())), preferred_element_type=F32)


def _dot_exact(a, b):
    return jnp.dot(a, b, precision=HIGHEST, preferred_element_type=F32)


def _params(*sem):
    return pltpu.CompilerParams(dimension_semantics=sem, vmem_limit_bytes=VMEM_LIMIT)


def _mm_kernel(*refs, has_norm, has_res):
    refs = list(refs)
    a_ref = refs.pop(0)
    nw_ref = refs.pop(0) if has_norm else None
    w_ref = refs.pop(0)
    res_ref = refs.pop(0) if has_res else None
    o_ref = refs.pop(0)
    abf_ref = refs.pop(0)

    @pl.when(pl.program_id(1) == 0)
    def _():
        a = a_ref[...]
        if has_norm:
            a = _rms_rows(a, nw_ref[...])
        abf_ref[...] = a.astype(BF16)

    acc = _dot(abf_ref[...], w_ref[...])
    if has_res:
        acc = acc + res_ref[...]
    o_ref[...] = acc


def _col_tile(n):
    for t in (512, 384, 256, 128):
        if n % t == 0:
            return t
    raise ValueError(f"unsupported matmul width {n}")


def _mm(a, w, norm_w=None, residual=None):
    m, k = a.shape
    n = w.shape[1]
    tm, tn = ROW_TILE, _col_tile(n)
    assert m % tm == 0
    has_norm, has_res = norm_w is not None, residual is not None
    in_specs = [pl.BlockSpec((tm, k), lambda i, j: (i, 0))]
    args = [a]
    if has_norm:
        in_specs.append(pl.BlockSpec((1, k), lambda i, j: (0, 0)))
        args.append(norm_w.reshape(1, k))
    in_specs.append(pl.BlockSpec((k, tn), lambda i, j: (0, j)))
    args.append(w)
    if has_res:
        in_specs.append(pl.BlockSpec((tm, tn), lambda i, j: (i, j)))
        args.append(residual)
    return pl.pallas_call(
        functools.partial(_mm_kernel, has_norm=has_norm, has_res=has_res),
        out_shape=jax.ShapeDtypeStruct((m, n), F32),
        grid=(m // tm, n // tn),
        in_specs=in_specs,
        out_specs=pl.BlockSpec((tm, tn), lambda i, j: (i, j)),
        scratch_shapes=[pltpu.VMEM((tm, k), BF16)],
        compiler_params=_params("parallel", "arbitrary"),
        name="matmul",
    )(*args)


def _ffn_kernel(x_ref, nw_ref, w1_ref, w3_ref, w2_ref, o_ref, h_ref, acc_ref):
    f = pl.program_id(1)

    @pl.when(f == 0)
    def _():
        h_ref[...] = _rms_rows(x_ref[...], nw_ref[...]).astype(BF16)
        acc_ref[...] = jnp.zeros_like(acc_ref)

    h = h_ref[...]
    g = _dot(h, w1_ref[...])
    u = _dot(h, w3_ref[...])
    a = (g * jax.nn.sigmoid(g) * u).astype(BF16)
    acc_ref[...] += _dot(a, w2_ref[...])

    @pl.when(f == pl.num_programs(1) - 1)
    def _():
        o_ref[...] = x_ref[...] + acc_ref[...]


def _ffn(x, norm_w, w1, w3, w2):
    m, d = x.shape
    dff = w1.shape[1]
    tm, tf = ROW_TILE, 512
    return pl.pallas_call(
        _ffn_kernel,
        out_shape=jax.ShapeDtypeStruct((m, d), F32),
        grid=(m // tm, dff // tf),
        in_specs=[
            pl.BlockSpec((tm, d), lambda i, f: (i, 0)),
            pl.BlockSpec((1, d), lambda i, f: (0, 0)),
            pl.BlockSpec((d, tf), lambda i, f: (0, f)),
            pl.BlockSpec((d, tf), lambda i, f: (0, f)),
            pl.BlockSpec((tf, d), lambda i, f: (f, 0)),
        ],
        out_specs=pl.BlockSpec((tm, d), lambda i, f: (i, 0)),
        scratch_shapes=[pltpu.VMEM((tm, d), BF16), pltpu.VMEM((tm, d), F32)],
        compiler_params=_params("parallel", "arbitrary"),
        name="ffn_swiglu",
    )(x, norm_w.reshape(1, d), w1, w3, w2)


def _moe_kernel(be_ref, nv_ref, x_ref, w1_ref, w3_ref, w2_ref, o_ref, acc_ref):
    b, f = pl.program_id(0), pl.program_id(1)
    valid = b < nv_ref[0]

    @pl.when(f == 0)
    def _():
        acc_ref[...] = jnp.zeros_like(acc_ref)

    @pl.when(valid)
    def _():
        h = x_ref[...]
        g = _dot(h, w1_ref[0].astype(BF16))
        u = _dot(h, w3_ref[0].astype(BF16))
        a = (g * jax.nn.sigmoid(g) * u).astype(BF16)
        acc_ref[...] += _dot(a, w2_ref[0].astype(BF16))

    @pl.when(f == pl.num_programs(1) - 1)
    def _():
        o_ref[...] = acc_ref[...]


def _moe_experts(xb, block_e, n_valid, w1, w3, w2):
    cap, d = xb.shape
    dff = w1.shape[2]
    tm, tf = MOE_ROW_TILE, 256
    nb, nf = cap // tm, dff // tf

    def w13_map(b, f, be, nv):
        ok = b < nv[0]
        return (be[b], 0, jnp.where(ok, f, nf - 1))

    def w2_map(b, f, be, nv):
        ok = b < nv[0]
        return (be[b], jnp.where(ok, f, nf - 1), 0)

    grid_spec = pltpu.PrefetchScalarGridSpec(
        num_scalar_prefetch=2,
        grid=(nb, nf),
        in_specs=[
            pl.BlockSpec((tm, d), lambda b, f, be, nv: (b, 0)),
            pl.BlockSpec((1, d, tf), w13_map),
            pl.BlockSpec((1, d, tf), w13_map),
            pl.BlockSpec((1, tf, d), w2_map),
        ],
        out_specs=pl.BlockSpec((tm, d), lambda b, f, be, nv: (b, 0)),
        scratch_shapes=[pltpu.VMEM((tm, d), F32)],
    )
    return pl.pallas_call(
        _moe_kernel,
        out_shape=jax.ShapeDtypeStruct((cap, d), F32),
        grid_spec=grid_spec,
        compiler_params=_params("arbitrary", "arbitrary"),
        name="moe_swiglu",
    )(block_e, n_valid, xb, w1, w3, w2)


def _moe(x, norm_w, w_router, b_router, w1, w3, w2):
    t, d = x.shape
    tm = MOE_ROW_TILE
    h = _rms_rows(x, norm_w)
    logits = jnp.dot(h, w_router, precision=HIGHEST) + b_router
    lj, le = logits[:, None, :], logits[:, :, None]
    eidx = jnp.arange(N_EXPERTS, dtype=jnp.int32)
    beats = (lj > le) | ((lj == le) & (eidx[None, None, :] < eidx[None, :, None]))
    rank = jnp.sum(beats.astype(jnp.int32), axis=-1)
    sel = jnp.stack([rank == k for k in range(TOP_K)], axis=1)
    top_v = jnp.sum(jnp.where(sel, logits[:, None, :], 0.0), axis=-1)
    gates = jax.nn.softmax(top_v, axis=-1)
    n_slots = t * TOP_K
    oh = sel.reshape(n_slots, N_EXPERTS).astype(jnp.int32)
    counts = jnp.sum(oh, axis=0)
    padded = (counts + tm - 1) // tm * tm
    pad_ends = jnp.cumsum(padded)
    pad_starts = pad_ends - padded
    within = jnp.cumsum(oh, axis=0) - oh
    slot_pos = jnp.sum(oh * (within + pad_starts[None, :]), axis=-1)
    nb = n_slots // tm + N_EXPERTS
    cap = nb * tm
    buf_tok = jnp.zeros((cap,), jnp.int32).at[slot_pos].set(jnp.arange(n_slots, dtype=jnp.int32) // TOP_K)
    block_start = jnp.arange(nb, dtype=jnp.int32) * tm
    block_e = jnp.minimum(jnp.sum((pad_ends[None, :] <= block_start[:, None]).astype(jnp.int32), axis=-1),
                          N_EXPERTS - 1)
    n_valid = (pad_ends[-1] // tm).astype(jnp.int32).reshape(1)
    xb = h.astype(BF16)[buf_tok]
    yb = _moe_experts(xb, block_e, n_valid, w1, w3, w2)
    pos = slot_pos.reshape(t, TOP_K)
    y = yb[pos[:, 0]] * gates[:, 0:1] + yb[pos[:, 1]] * gates[:, 1:2]
    return x + y


def _s5_kernel(u_ref, sre_ref, sim_ref, are_ref, aim_ref, bbre_ref, bbim_ref, cre_ref, cim_ref, d_ref,
               gw_ref, gb_ref, nw_ref, y_ref, ore_ref, oim_ref, xr_ref, xi_ref, st_ref, *, tc, nbb):
    c = pl.program_id(1)
    rows = tc * nbb
    u = u_ref[...].reshape(rows, S5_WIDTH)
    ub = u.astype(BF16)
    xr_ref[...] = _dot(ub, bbre_ref[...])
    xi_ref[...] = _dot(ub, bbim_ref[...])

    @pl.when(c == 0)
    def _():
        st_ref[0] = sre_ref[...]
        st_ref[1] = sim_ref[...]

    ar = are_ref[...]
    ai = aim_ref[...]

    def step(t, carry):
        for g in range(nbb // SUBLANES):
            r0 = pl.multiple_of(t * nbb + g * SUBLANES, SUBLANES)
            sl = slice(g * SUBLANES, (g + 1) * SUBLANES)
            pr = st_ref[0, sl, :]
            pi = st_ref[1, sl, :]
            nr = ar * pr - ai * pi + xr_ref[pl.ds(r0, SUBLANES), :]
            ni = ar * pi + ai * pr + xi_ref[pl.ds(r0, SUBLANES), :]
            xr_ref[pl.ds(r0, SUBLANES), :] = nr
            xi_ref[pl.ds(r0, SUBLANES), :] = ni
            st_ref[0, sl, :] = nr
            st_ref[1, sl, :] = ni
        return carry

    lax.fori_loop(0, tc, step, 0)

    y = _dot(xr_ref[...].astype(BF16), cre_ref[...]) - _dot(xi_ref[...].astype(BF16), cim_ref[...])
    y = y + d_ref[...] * u
    gy = 0.5 * y * (1.0 + jnp.tanh(0.7978845608028654 * (y + 0.044715 * (y * y * y))))
    y = gy * _sigmoid(_dot(gy.astype(BF16), gw_ref[...]) + gb_ref[...])
    y_ref[...] = _rms_rows(y, nw_ref[...]).reshape(tc, nbb, S5_WIDTH)

    @pl.when(c == pl.num_programs(1) - 1)
    def _():
        ore_ref[...] = st_ref[0]
        oim_ref[...] = st_ref[1]


def _s5(u_tm, st_re, st_im, prm, *, tc, nbb):
    L, n, _ = u_tm.shape
    ch = S5_CHANNELS
    vec = lambda w: pl.BlockSpec((1, w), lambda s, c: (0, 0))
    mat = lambda a, b: pl.BlockSpec((a, b), lambda s, c: (0, 0))
    st_spec = pl.BlockSpec((nbb, ch), lambda s, c: (s, 0))
    return pl.pallas_call(
        functools.partial(_s5_kernel, tc=tc, nbb=nbb),
        out_shape=(jax.ShapeDtypeStruct((L, n, S5_WIDTH), F32),
                   jax.ShapeDtypeStruct((n, ch), F32), jax.ShapeDtypeStruct((n, ch), F32)),
        grid=(n // nbb, L // tc),
        in_specs=[pl.BlockSpec((tc, nbb, S5_WIDTH), lambda s, c: (c, s, 0)), st_spec, st_spec,
                  vec(ch), vec(ch), mat(S5_WIDTH, ch), mat(S5_WIDTH, ch), mat(ch, S5_WIDTH), mat(ch, S5_WIDTH),
                  vec(S5_WIDTH), mat(S5_WIDTH, S5_WIDTH), vec(S5_WIDTH), vec(S5_WIDTH)],
        out_specs=(pl.BlockSpec((tc, nbb, S5_WIDTH), lambda s, c: (c, s, 0)), st_spec, st_spec),
        scratch_shapes=[pltpu.VMEM((tc * nbb, ch), F32), pltpu.VMEM((tc * nbb, ch), F32),
                        pltpu.VMEM((2, nbb, ch), F32)],
        compiler_params=_params("parallel", "arbitrary"),
        name="s5_mixer",
    )(u_tm, st_re, st_im, *prm)


def _s5_params(lam_re, lam_im, b_re, b_im, c_re, c_im, d_skip, log_dt, glu_w, glu_b, norm_w):
    delta = jnp.exp(log_dt)[:, None]
    mag = jnp.exp(lam_re * delta)
    ab_re, ab_im = mag * jnp.cos(lam_im * delta), mag * jnp.sin(lam_im * delta)
    den = lam_re * lam_re + lam_im * lam_im
    q_re = ((ab_re - 1.0) * lam_re + ab_im * lam_im) / den
    q_im = (ab_im * lam_re - (ab_re - 1.0) * lam_im) / den
    bb_re = q_re[..., None] * b_re - q_im[..., None] * b_im
    bb_im = q_re[..., None] * b_im + q_im[..., None] * b_re
    eye = jnp.eye(S5_GROUPS, dtype=F32)

    def in_blockdiag(bb):
        t = jnp.swapaxes(bb, 1, 2)
        return (eye[:, None, :, None] * t[:, :, None, :]).reshape(S5_WIDTH, S5_CHANNELS).astype(BF16)

    def out_blockdiag(cc):
        t = jnp.swapaxes(cc, 1, 2)
        return (eye[:, None, :, None] * t[:, :, None, :]).reshape(S5_CHANNELS, S5_WIDTH).astype(BF16)

    return (ab_re.reshape(1, S5_CHANNELS), ab_im.reshape(1, S5_CHANNELS), in_blockdiag(bb_re), in_blockdiag(bb_im),
            out_blockdiag(c_re), out_blockdiag(c_im), d_skip.reshape(1, S5_WIDTH), glu_w.astype(BF16),
            glu_b.reshape(1, S5_WIDTH), norm_w.reshape(1, S5_WIDTH))


def _ssd_kernel(p_ref, st_ref, cs_ref, cw_ref, cb_ref, dtb_ref, a_ref, dsk_ref, nw_ref,
                y_ref, ost_ref, ocs_ref, ext_ref, win_ref, s_ref, *, q, sb):
    c = pl.program_id(1)
    last = c == pl.num_programs(1) - 1
    hd, nh, gw = SSD_HEAD_DIM, SSD_HEADS, SSD_WIDTH // SSD_GROUPS
    pad_rows = hd - q

    lane = lax.broadcasted_iota(jnp.int32, (q, LANES), 1)
    row = lax.broadcasted_iota(jnp.int32, (q, LANES), 0)
    causal2 = row >= (lane % hd)
    lane64 = lax.broadcasted_iota(jnp.int32, (hd, LANES), 1)
    tri = (lax.broadcasted_iota(jnp.int32, (q, q), 0) >= lax.broadcasted_iota(jnp.int32, (q, q), 1)).astype(F32)
    e_h = lax.broadcasted_iota(jnp.int32, (LANES, SSD_WIDTH), 0)
    e_c = lax.broadcasted_iota(jnp.int32, (LANES, SSD_WIDTH), 1)
    expand = (e_h == e_c // hd).astype(F32)
    i_s = lax.broadcasted_iota(jnp.int32, (q, SSD_WIDTH), 0)
    i_c = lax.broadcasted_iota(jnp.int32, (q, SSD_WIDTH), 1)
    eye_x = (i_s == i_c % hd).astype(F32)

    for s in range(sb):
        rs = slice(s * q, (s + 1) * q)

        @pl.when(c == 0)
        def _():
            win_ref[s, 0:5, :] = jnp.zeros((5, SSD_CONV_DIM), F32)
            win_ref[s, 5:8, :] = cs_ref[s]
            for g in range(SSD_GROUPS):
                for k in range(gw // LANES):
                    r0 = g * gw + k * LANES
                    s_ref[s, g, :, k * LANES:(k + 1) * LANES] = st_ref[s, r0:r0 + LANES, :].T

        z = p_ref[rs, PROJ_Z:PROJ_XBC]
        xbc = p_ref[rs, PROJ_XBC:PROJ_HR]
        dt = p_ref[rs, PROJ_DT:PROJ_WIDTH]
        ext_ref[0:8, :] = win_ref[s]
        ext_ref[8:8 + q, :] = xbc
        conv = cb_ref[...]
        for j in range(SSD_CONV):
            conv = conv + cw_ref[j:j + 1, :] * ext_ref[pl.ds(5 + j, q), :]
        win_ref[s] = ext_ref[q:q + 8, :]
        xc = _silu(conv)
        xs = xc[:, :SSD_WIDTH]
        bm = xc[:, SSD_WIDTH:SSD_WIDTH + SSD_GROUPS * SSD_STATE]
        cm = xc[:, SSD_WIDTH + SSD_GROUPS * SSD_STATE:]

        step = _softplus(dt + dtb_ref[...])
        adt = step * a_ref[...]
        step_x = _dot_exact(step, expand)
        acs_x = _dot_exact(tri, _dot_exact(adt, expand))
        diag = jnp.sum(acs_x * eye_x, axis=0, keepdims=True)
        acs_last = acs_x[q - 1:q, :]
        xdt = xs * step_x
        exp_acs = jnp.exp(acs_x)
        xw = xdt * jnp.exp(acs_last - acs_x)
        dec = jnp.exp(acs_last)

        for g in range(SSD_GROUPS):
            bg = bm[:, g * SSD_STATE:(g + 1) * SSD_STATE]
            cg = cm[:, g * SSD_STATE:(g + 1) * SSD_STATE].astype(BF16)
            gl = slice(g * gw, (g + 1) * gw)
            b64 = bg if pad_rows == 0 else jnp.concatenate([bg, jnp.zeros((pad_rows, SSD_STATE), F32)], axis=0)
            cb2 = _dot_nt(cg, jnp.concatenate([b64, b64], axis=0).astype(BF16))
            sg = s_ref[s, g]
            yoff = _dot(cg, sg.astype(BF16)) * exp_acs[:, gl]
            for pr in range(gw // LANES):
                l0 = g * gw + pr * LANES
                seg = acs_x[:, l0:l0 + LANES] - diag[:, l0:l0 + LANES]
                m = cb2 * jnp.exp(jnp.where(causal2, seg, -jnp.inf))
                xd = xdt[:, l0:l0 + LANES]
                xd64 = xd if pad_rows == 0 else jnp.concatenate([xd, jnp.zeros((pad_rows, LANES), F32)], axis=0)
                rhs = jnp.concatenate([jnp.where(lane64 < hd, xd64, 0.0), jnp.where(lane64 >= hd, xd64, 0.0)], axis=0)
                ydiag = _dot(m.astype(BF16), rhs.astype(BF16))
                y_ref[rs, l0:l0 + LANES] = ydiag + yoff[:, pr * LANES:(pr + 1) * LANES]
            bpad = jnp.concatenate([bg, jnp.zeros((LANES - q, SSD_STATE), F32)], axis=0)
            xwpad = jnp.concatenate([xw[:, gl], jnp.zeros((LANES - q, gw), F32)], axis=0)
            s_ref[s, g] = dec[:, gl] * sg + _dot(bpad.T.astype(BF16), xwpad.astype(BF16))

        y = y_ref[rs, :] + dsk_ref[...] * xs
        y = y * _silu(z)
        halves = []
        for g in range(SSD_GROUPS):
            yg = y[:, g * gw:(g + 1) * gw]
            halves.append(yg * lax.rsqrt(jnp.mean(yg * yg, axis=-1, keepdims=True) + RMS_EPS))
        y_ref[rs, :] = jnp.concatenate(halves, axis=1) * nw_ref[...]

        @pl.when(last)
        def _():
            ocs_ref[s] = ext_ref[q + 5:q + 8, :]
            for g in range(SSD_GROUPS):
                for k in range(gw // LANES):
                    r0 = g * gw + k * LANES
                    ost_ref[s, r0:r0 + LANES, :] = s_ref[s, g, :, k * LANES:(k + 1) * LANES].T


def _ssd(proj, row0, nseq, L, st, cs, prm, *, sb):
    q = min(L, SSD_CHUNK)
    nchunk = L // q
    rows = sb * q
    base = row0 // rows
    assert row0 % rows == 0 and nseq % sb == 0
    vec = lambda w: pl.BlockSpec((1, w), lambda s, c: (0, 0))
    st_spec = pl.BlockSpec((sb, SSD_WIDTH, SSD_STATE), lambda s, c: (s, 0, 0))
    cs_spec = pl.BlockSpec((sb, SSD_CONV - 1, SSD_CONV_DIM), lambda s, c: (s, 0, 0))
    return pl.pallas_call(
        functools.partial(_ssd_kernel, q=q, sb=sb),
        out_shape=(jax.ShapeDtypeStruct((nseq * L, SSD_WIDTH), F32),
                   jax.ShapeDtypeStruct((nseq, SSD_WIDTH, SSD_STATE), F32),
                   jax.ShapeDtypeStruct((nseq, SSD_CONV - 1, SSD_CONV_DIM), F32)),
        grid=(nseq // sb, nchunk),
        in_specs=[pl.BlockSpec((rows, PROJ_WIDTH), lambda s, c: (base + s * nchunk + c, 0)), st_spec, cs_spec,
                  pl.BlockSpec((SSD_CONV, SSD_CONV_DIM), lambda s, c: (0, 0)), vec(SSD_CONV_DIM),
                  vec(LANES), vec(LANES), vec(SSD_WIDTH), vec(SSD_WIDTH)],
        out_specs=(pl.BlockSpec((rows, SSD_WIDTH), lambda s, c: (s * nchunk + c, 0)), st_spec, cs_spec),
        scratch_shapes=[pltpu.VMEM((q + 8, SSD_CONV_DIM), F32), pltpu.VMEM((sb, 8, SSD_CONV_DIM), F32),
                        pltpu.VMEM((sb, SSD_GROUPS, SSD_STATE, SSD_WIDTH // SSD_GROUPS), F32)],
        compiler_params=_params("parallel", "arbitrary"),
        name="ssd_mixer",
    )(proj, st, cs, *prm)


def _ssd_params(conv_w, conv_b, dt_bias, a_log, d_skip, norm_w):
    pad = jnp.zeros((LANES - SSD_HEADS,), F32)
    return (conv_w, conv_b.reshape(1, SSD_CONV_DIM), jnp.concatenate([dt_bias, pad]).reshape(1, LANES),
            jnp.concatenate([-jnp.exp(a_log), pad]).reshape(1, LANES),
            jnp.repeat(d_skip, SSD_HEAD_DIM).reshape(1, SSD_WIDTH), norm_w.reshape(1, SSD_WIDTH))


def _attn_kernel(q_ref, k_ref, v_ref, o_ref, *, lq, sb):
    scale = MEM_HEAD_DIM ** -0.5
    for s in range(sb):
        rq = slice(s * lq, (s + 1) * lq)
        rk = slice(s * MEM_TOKENS, (s + 1) * MEM_TOKENS)
        for h in range(MEM_HEADS):
            cl = slice(h * MEM_HEAD_DIM, (h + 1) * MEM_HEAD_DIM)
            sc = _dot_nt(q_ref[rq, cl].astype(BF16), k_ref[rk, cl].astype(BF16)) * scale
            sc = sc - jnp.max(sc, axis=-1, keepdims=True)
            p = jnp.exp(sc)
            p = p / jnp.sum(p, axis=-1, keepdims=True)
            o_ref[rq, cl] = _dot(p.astype(BF16), v_ref[rk, cl].astype(BF16))


def _attend(q, row0, nseq, L, k2d, v2d, kcol, vcol, *, lq, sb):
    nl = L // lq
    rows = sb * lq
    base = row0 // rows
    assert row0 % rows == 0 and (sb == 1 or nl == 1)
    return pl.pallas_call(
        functools.partial(_attn_kernel, lq=lq, sb=sb),
        out_shape=jax.ShapeDtypeStruct((nseq * L, MEM_WIDTH), F32),
        grid=(nseq // sb, nl),
        in_specs=[pl.BlockSpec((rows, MEM_WIDTH), lambda s, l: (base + s * nl + l, 0)),
                  pl.BlockSpec((sb * MEM_TOKENS, MEM_WIDTH), lambda s, l: (s, kcol)),
                  pl.BlockSpec((sb * MEM_TOKENS, MEM_WIDTH), lambda s, l: (s, vcol))],
        out_specs=pl.BlockSpec((rows, MEM_WIDTH), lambda s, l: (s * nl + l, 0)),
        compiler_params=_params("parallel", "arbitrary"),
        name="mem_attention",
    )(q, k2d, v2d)


def _head_sum(x, ones_bd):
    return _dot_exact(x, ones_bd)


def _rwkv_prep_kernel(h_ref, p_ref, mu_ref, wl_ref, w0_ref, a0_ref, kk_ref, ka_ref, rk_ref, ones_ref,
                      r_ref, w_ref, k_ref, v_ref, n_ref, b_ref, g_ref, bo_ref):
    W = RWKV_WIDTH
    h = h_ref[...]
    hs = h + (p_ref[...] - h) * mu_ref[...]
    r, k, v = hs[:, :W], hs[:, W:2 * W], hs[:, 2 * W:3 * W]
    lo = hs[:, 3 * W:]
    lane = lax.broadcasted_iota(jnp.int32, lo.shape, 1)
    act = jnp.where(lane < RWKV_W_LORA, jnp.tanh(lo),
                    jnp.where(lane < RWKV_W_LORA + RWKV_A_LORA, lo, _sigmoid(lo)))
    lora = _dot(act.astype(BF16), wl_ref[...])
    w_log = -_softplus(-(w0_ref[...] + lora[:, :W])) - 0.5
    a = _sigmoid(a0_ref[...] + lora[:, W:2 * W])
    ones_bd = ones_ref[...]
    kk = k * kk_ref[...]
    kk = kk / jnp.maximum(jnp.sqrt(_head_sum(kk * kk, ones_bd)), 1e-12)
    k2 = k * (1.0 + (a - 1.0) * ka_ref[...])
    r_ref[...] = r
    w_ref[...] = -jnp.exp(w_log)
    k_ref[...] = k2
    v_ref[...] = v
    n_ref[...] = kk
    b_ref[...] = kk * a
    g_ref[...] = lora[:, 2 * W:]
    bo_ref[...] = _head_sum(r * k2 * rk_ref[...], ones_bd) * v


def _rwkv_prep(hr, prev, prm):
    m = hr.shape[0]
    tm, W = ROW_TILE, RWKV_WIDTH
    mu, wl, w0, a0, k_k, k_a, r_k, ones_bd = prm
    vec = lambda w: pl.BlockSpec((1, w), lambda i: (0, 0))
    row = lambda w: pl.BlockSpec((tm, w), lambda i: (i, 0))
    return pl.pallas_call(
        _rwkv_prep_kernel,
        out_shape=tuple(jax.ShapeDtypeStruct((m, W), F32) for _ in range(8)),
        grid=(m // tm,),
        in_specs=[row(RWKV_SHIFT_DIM), row(RWKV_SHIFT_DIM), vec(RWKV_SHIFT_DIM),
                  pl.BlockSpec(wl.shape, lambda i: (0, 0)), vec(W), vec(W), vec(W), vec(W), vec(W),
                  pl.BlockSpec((W, W), lambda i: (0, 0))],
        out_specs=tuple(row(W) for _ in range(8)),
        compiler_params=_params("parallel"),
        name="rwkv_prep",
    )(hr, prev, mu, wl, w0, a0, k_k, k_a, r_k, ones_bd)


def _rwkv_post_kernel(y_ref, bo_ref, g_ref, gw_ref, gb_ref, ones_ref, o_ref):
    ones_bd = ones_ref[...]
    y = y_ref[...]
    inv = 1.0 / RWKV_HEAD_DIM
    d = y - _head_sum(y, ones_bd) * inv
    var = _head_sum(d * d, ones_bd) * inv
    yn = d * lax.rsqrt(var + RWKV_GN_EPS) * gw_ref[...] + gb_ref[...]
    o_ref[...] = (yn + bo_ref[...]) * g_ref[...]


def _rwkv_post(y, bonus, g, gn_w, gn_b, ones_bd):
    m, W = y.shape
    tm = ROW_TILE
    vec = pl.BlockSpec((1, W), lambda i: (0, 0))
    row = pl.BlockSpec((tm, W), lambda i: (i, 0))
    return pl.pallas_call(
        _rwkv_post_kernel,
        out_shape=jax.ShapeDtypeStruct((m, W), F32),
        grid=(m // tm,),
        in_specs=[row, row, row, vec, vec, pl.BlockSpec((W, W), lambda i: (0, 0))],
        out_specs=row,
        compiler_params=_params("parallel"),
        name="rwkv_post",
    )(y, bonus, g, gn_w.reshape(1, W), gn_b.reshape(1, W), ones_bd)


def _rwkv_params(mu, w0, w2, a0, a2, g2, k_k, k_a, r_k):
    W = RWKV_WIDTH
    nl = RWKV_W_LORA + RWKV_A_LORA + RWKV_G_LORA
    wl = jnp.zeros((nl, 3 * W), F32)
    wl = wl.at[:RWKV_W_LORA, :W].set(w2)
    wl = wl.at[RWKV_W_LORA:RWKV_W_LORA + RWKV_A_LORA, W:2 * W].set(a2)
    wl = wl.at[RWKV_W_LORA + RWKV_A_LORA:, 2 * W:].set(g2)
    head = jnp.arange(W) // RWKV_HEAD_DIM
    ones_bd = (head[:, None] == head[None, :]).astype(F32)
    v = lambda t: t.reshape(1, -1)
    return (v(mu), wl.astype(BF16), v(w0), v(a0), v(k_k), v(k_a), v(r_k), ones_bd)


def _bdot(a, b):
    return jnp.dot(a.astype(BF16), b.astype(BF16), preferred_element_type=F32)


def _bdot_nt(a, b):
    return lax.dot_general(a.astype(BF16), b.astype(BF16), (((1,), (1,)), ((), ())), preferred_element_type=F32)


def _split(x):
    hi = x.astype(BF16)
    return hi, (x - hi.astype(F32)).astype(BF16)


def _dot3(a, b):
    ah, al = _split(a)
    bh, bl = _split(b)
    return (jnp.dot(ah, bh, preferred_element_type=F32) + jnp.dot(al, bh, preferred_element_type=F32)
            + jnp.dot(ah, bl, preferred_element_type=F32))


def _dot3_nt(a, b):
    ah, al = _split(a)
    bh, bl = _split(b)
    dn = (((1,), (1,)), ((), ()))
    return (lax.dot_general(ah, bh, dn, preferred_element_type=F32)
            + lax.dot_general(al, bh, dn, preferred_element_type=F32)
            + lax.dot_general(ah, bl, dn, preferred_element_type=F32))


RWKV_GROUP = 4
RWKV_GW = RWKV_GROUP * RWKV_HEAD_DIM
RWKV_NG = RWKV_HEADS // RWKV_GROUP


def _rwkv_chunk_kernel(r_ref, ls_ref, k_ref, v_ref, n_ref, b_ref, s0_ref, y_ref, sf_ref, s_ref, *, C, ns):
    c = pl.program_id(1)
    G, GW, HD = RWKV_GROUP, RWKV_GW, RWKV_HEAD_DIM
    RI = ns * C
    R = G * RI
    SB = G * C
    groups = range(RWKV_NG)

    @pl.when(c == 0)
    def _():
        for q in groups:
            for s in range(ns):
                s_ref[q, s] = jnp.concatenate([s0_ref[s, G * q + h] for h in range(G)], axis=1)

    ri = lax.broadcasted_iota(jnp.int32, (2 * RI, RI), 0)
    ci = lax.broadcasted_iota(jnp.int32, (2 * RI, RI), 1)
    same = ((ri % RI) // C) == (ci // C)
    cum = (same & ((ri >= RI) | (ri >= ci))).astype(BF16)
    ls_all = ls_ref[...]
    l1 = ls_all.astype(BF16)
    l2f = ls_all - l1.astype(F32)
    l2 = l2f.astype(BF16)
    l3 = (l2f - l2.astype(F32)).astype(BF16)
    lw2 = (jnp.dot(cum, l1, preferred_element_type=F32) + jnp.dot(cum, l2, preferred_element_type=F32)
           + jnp.dot(cum, l3, preferred_element_type=F32))
    lw_all, lwl_all = lw2[:RI], lw2[RI:]

    lane_in = lax.broadcasted_iota(jnp.int32, (C, GW), 1) // HD
    row = lax.broadcasted_iota(jnp.int32, (R, R), 0)
    col = lax.broadcasted_iota(jnp.int32, (R, R), 1)
    ent = (row // C) == (col // C)
    strict = ent & (row > col)
    incl = ent & (row >= col)
    eye = row == col
    eye_f = eye.astype(F32)
    own = (lax.broadcasted_iota(jnp.int32, (SB, GW), 0) // C) == (lax.broadcasted_iota(jnp.int32, (SB, GW), 1) // HD)
    rows_r = lax.broadcasted_iota(jnp.int32, (R, GW), 0)
    rows_2r = lax.broadcasted_iota(jnp.int32, (2 * R, GW), 0)

    def stack(x):
        parts = []
        for s in range(ns):
            xs = x[s * C:(s + 1) * C]
            parts += [jnp.where(lane_in == h, xs, 0.0) for h in range(G)]
        return jnp.concatenate(parts, axis=0)

    def dup(x):
        parts = []
        for s in range(ns):
            parts += [x[s * C:(s + 1) * C]] * G
        return jnp.concatenate(parts, axis=0)

    st = []
    for q in groups:
        gl = slice(q * GW, (q + 1) * GW)
        lw, lwl, ls = lw_all[:, gl], lwl_all[:, gl], ls_all[:, gl]
        w_inv = jnp.exp(-lw)
        w_rest = jnp.exp(lwl - lw)
        kk, bb = k_ref[:, gl], b_ref[:, gl]
        st.append(dict(
            n_st=stack(n_ref[:, gl] * jnp.exp(lw - ls)), r_st=stack(r_ref[:, gl] * jnp.exp(lw)),
            v_st=stack(v_ref[:, gl]), bh_st=stack(bb * w_rest), kh_st=stack(kk * w_rest),
            b_dup=dup(bb * w_inv), k_dup=dup(kk * w_inv), w_c=jnp.exp(lwl)))
    for d in st:
        nr = jnp.concatenate([d['n_st'], d['r_st']], axis=0)
        gb = _bdot_nt(nr, d['b_dup'])
        gk = _bdot_nt(nr, d['k_dup'])
        d['a_nb'] = jnp.where(strict, gb[:R], 0.0)
        d['a_rb'] = jnp.where(incl, gb[R:], 0.0)
        d['a_nk'] = jnp.where(strict, gk[:R], 0.0)
        d['a_rk'] = jnp.where(incl, gk[R:], 0.0)
        d['t'] = eye_f - d['a_nb']
        d['p'] = d['a_nb']
    for _ in range(C.bit_length() - 2):
        for d in st:
            d['p'] = _bdot(d['p'], d['p'])
        for d in st:
            d['t'] = _bdot(d['t'], eye_f + d['p'])
    for d in st:
        d['p1'] = _bdot(d['t'], d['n_st'])
        d['z'] = _bdot(d['a_nk'], d['v_st'])
    for d in st:
        d['p2'] = _bdot(d['t'], d['z'])
    for q, d in enumerate(st):
        p1, p2 = d['p1'], d['p2']
        p1_t = p1.T
        lt = jnp.concatenate([d['v_st'].T, -p2.T], axis=1)
        kb = jnp.concatenate([d['kh_st'], d['bh_st']], axis=0)
        sa_parts, rs_parts = [], []
        for s in range(ns):
            rsl = slice(s * SB, (s + 1) * SB)
            S = s_ref[q, s]
            ss = jnp.concatenate([S] * G, axis=0)
            xr = _dot3_nt(jnp.concatenate([p1[rsl], d['r_st'][rsl]], axis=0), ss)
            sa_parts.append(-jnp.where(own, xr[:SB], 0.0) - p2[rsl])
            rs_parts.append(jnp.where(own, xr[SB:], 0.0))
            if ns == 1:
                bh_s, kb_s = d['bh_st'], kb
            else:
                bh_s = jnp.where((rows_r // SB) == s, d['bh_st'], 0.0)
                kb_s = jnp.where(((rows_2r % R) // SB) == s, kb, 0.0)
            m_bd = jnp.where(eye, d['w_c'][s * C:s * C + 1, :], 0.0) - _bdot(p1_t, bh_s)
            nf = _bdot(lt, kb_s)
            fold = nf[:HD]
            for h in range(1, G):
                fold = fold + nf[h * HD:(h + 1) * HD]
            s_ref[q, s] = _dot3(S, m_bd) + fold
        sa_st = jnp.concatenate(sa_parts, axis=0) if ns > 1 else sa_parts[0]
        rs_st = jnp.concatenate(rs_parts, axis=0) if ns > 1 else rs_parts[0]
        y_st = rs_st + _bdot(jnp.concatenate([d['a_rb'], d['a_rk']], axis=1),
                             jnp.concatenate([sa_st, d['v_st']], axis=0))
        for s in range(ns):
            y = y_st[s * SB:s * SB + C]
            for h in range(1, G):
                y = y + y_st[s * SB + h * C:s * SB + (h + 1) * C]
            y_ref[s * C:(s + 1) * C, q * GW:(q + 1) * GW] = y

    @pl.when(c == pl.num_programs(1) - 1)
    def _():
        for q in groups:
            for s in range(ns):
                S = s_ref[q, s]
                for h in range(G):
                    sf_ref[s, G * q + h] = S[:, h * HD:(h + 1) * HD]


def _rwkv_chunked(r, ls, k, v, kk, b, state, row0, nseq, L):
    HD, W = RWKV_HEAD_DIM, RWKV_WIDTH
    C = min(L, HD)
    ns = HD // C
    nt = L // C
    rows = ns * C
    base = row0 // rows
    assert row0 % rows == 0 and (ns == 1 or nt == 1) and nseq % ns == 0
    row_spec = pl.BlockSpec((rows, W), lambda s, c: (base + s * nt + c, 0))
    st_spec = pl.BlockSpec((ns, RWKV_HEADS, HD, HD), lambda s, c: (s, 0, 0, 0))
    return pl.pallas_call(
        functools.partial(_rwkv_chunk_kernel, C=C, ns=ns),
        out_shape=(jax.ShapeDtypeStruct((nseq * L, W), F32), jax.ShapeDtypeStruct((nseq, RWKV_HEADS, HD, HD), F32)),
        grid=(nseq // ns, nt),
        in_specs=[row_spec] * 6 + [st_spec],
        out_specs=(pl.BlockSpec((rows, W), lambda s, c: (s * nt + c, 0)), st_spec),
        scratch_shapes=[pltpu.VMEM((RWKV_NG, ns, HD, RWKV_GW), F32)],
        compiler_params=_params("parallel", "arbitrary"),
        name="rwkv_chunked",
    )(r, ls, k, v, kk, b, state)


def kernel(x_prompt, x_sample, mem_prompt, cache_mem_k, cache_mem_v, state_ssd, state_ssd_conv, state_rwkv, state_rwkv_shift, state_s5_re, state_s5_im, norm_mix, w_in, ssd_conv_w, ssd_conv_b, ssd_dt_bias, ssd_a_log, ssd_d, ssd_norm_w, rwkv_mu, rwkv_w0, rwkv_w2, rwkv_a0, rwkv_a2, rwkv_g2, rwkv_k_k, rwkv_k_a, rwkv_r_k, rwkv_gn_w, rwkv_gn_b, s5_lam_re, s5_lam_im, s5_b_re, s5_b_im, s5_c_re, s5_c_im, s5_d, s5_log_dt, s5_glu_w, s5_glu_b, s5_norm_w, w_out, norm_mem, mem_norm_w, wq_mem, wk_mem, wv_mem, wo_mem, norm_ffn, ffn_w1, ffn_w3, ffn_w2, moe_router_w, moe_router_b, moe_w1, moe_w3, moe_w2, final_norm_w):
    bp, lp, d = x_prompt.shape
    bs, ls, _ = x_sample.shape
    tp, ts = bp * lp, bs * ls
    x = jnp.concatenate([x_prompt.reshape(tp, d), x_sample.reshape(ts, d)], axis=0)
    mem_rows = mem_prompt.reshape(bp * MEM_TOKENS, d)
    s5_pad = SUBLANES - bp

    p_mk, p_mv, p_st, s_st = [], [], [], []
    for i in range(DEPTH):
        c0 = SSD_WIDTH
        c1 = c0 + SSD_CONV_DIM
        c2 = c1 + SSD_HEADS
        wi = w_in[i]
        w_in_packed = jnp.concatenate(
            [wi[:, :c1], wi[:, c2:], wi[:, c1:c2], jnp.zeros((d, PROJ_WIDTH - PROJ_DT - SSD_HEADS), F32)],
            axis=1).astype(BF16)
        proj = _mm(x, w_in_packed, norm_w=norm_mix[i])

        wkv = jnp.concatenate([wk_mem[i], wv_mem[i]], axis=1).astype(BF16)
        kv = _mm(mem_rows, wkv, norm_w=mem_norm_w[i])
        p_mk.append(kv[:, :MEM_WIDTH].reshape(bp, MEM_TOKENS, MEM_HEADS, MEM_HEAD_DIM))
        p_mv.append(kv[:, MEM_WIDTH:].reshape(bp, MEM_TOKENS, MEM_HEADS, MEM_HEAD_DIM))

        ssd_prm = _ssd_params(ssd_conv_w[i], ssd_conv_b[i], ssd_dt_bias[i], ssd_a_log[i], ssd_d[i], ssd_norm_w[i])
        y_ssd_p, ssd_p, conv_p = _ssd(
            proj, 0, bp, lp, jnp.zeros((bp, SSD_WIDTH, SSD_STATE), F32),
            jnp.zeros((bp, SSD_CONV - 1, SSD_CONV_DIM), F32), ssd_prm, sb=1)
        y_ssd_s, ssd_s, conv_s = _ssd(
            proj, tp, bs, ls, state_ssd[i].reshape(bs, SSD_WIDTH, SSD_STATE), state_ssd_conv[i], ssd_prm, sb=8)

        s5_prm = _s5_params(s5_lam_re[i], s5_lam_im[i], s5_b_re[i], s5_b_im[i], s5_c_re[i], s5_c_im[i], s5_d[i],
                            s5_log_dt[i], s5_glu_w[i], s5_glu_b[i], s5_norm_w[i])
        u = proj[:, PROJ_U:PROJ_DT]
        u_p = jnp.pad(jnp.swapaxes(u[:tp].reshape(bp, lp, S5_WIDTH), 0, 1), ((0, 0), (0, s5_pad), (0, 0)))
        zst = jnp.zeros((SUBLANES, S5_CHANNELS), F32)
        y5_p, s5r_p, s5i_p = _s5(u_p, zst, zst, s5_prm, tc=64, nbb=SUBLANES)
        u_s = jnp.swapaxes(u[tp:].reshape(bs, ls, S5_WIDTH), 0, 1)
        y5_s, s5r_s, s5i_s = _s5(u_s, state_s5_re[i].reshape(bs, S5_CHANNELS),
                                 state_s5_im[i].reshape(bs, S5_CHANNELS), s5_prm, tc=ls, nbb=64)
        y_s5 = jnp.concatenate([jnp.swapaxes(y5_p[:, :bp], 0, 1).reshape(tp, S5_WIDTH),
                                jnp.swapaxes(y5_s, 0, 1).reshape(ts, S5_WIDTH)], axis=0)

        rw_prm = _rwkv_params(rwkv_mu[i], rwkv_w0[i], rwkv_w2[i], rwkv_a0[i], rwkv_a2[i], rwkv_g2[i],
                              rwkv_k_k[i], rwkv_k_a[i], rwkv_r_k[i])
        hr = proj[:, PROJ_HR:PROJ_U]
        hr_p = hr[:tp].reshape(bp, lp, RWKV_SHIFT_DIM)
        hr_s = hr[tp:].reshape(bs, ls, RWKV_SHIFT_DIM)
        prev = jnp.concatenate([
            jnp.concatenate([jnp.zeros((bp, 1, RWKV_SHIFT_DIM), F32), hr_p[:, :-1]], axis=1).reshape(tp, -1),
            jnp.concatenate([state_rwkv_shift[i][:, None], hr_s[:, :-1]], axis=1).reshape(ts, -1)], axis=0)
        r_, w_, k_, v_, kk_, b_, g_, bonus = _rwkv_prep(hr, prev, rw_prm)
        yp_, wkv_p = _rwkv_chunked(r_, w_, k_, v_, kk_, b_,
                                   jnp.zeros((bp, RWKV_HEADS, RWKV_HEAD_DIM, RWKV_HEAD_DIM), F32), 0, bp, lp)
        ys_, wkv_s = _rwkv_chunked(r_, w_, k_, v_, kk_, b_, state_rwkv[i], tp, bs, ls)
        y_rw = _rwkv_post(jnp.concatenate([yp_, ys_], axis=0), bonus, g_, rwkv_gn_w[i], rwkv_gn_b[i], rw_prm[-1])

        p_st.append((ssd_p.reshape(bp, SSD_HEADS, SSD_HEAD_DIM, SSD_STATE), conv_p, wkv_p, hr_p[:, -1],
                     s5r_p[:bp].reshape(bp, S5_GROUPS, S5_STATE), s5i_p[:bp].reshape(bp, S5_GROUPS, S5_STATE)))
        s_st.append((ssd_s.reshape(bs, SSD_HEADS, SSD_HEAD_DIM, SSD_STATE), conv_s, wkv_s, hr_s[:, -1],
                     s5r_s.reshape(bs, S5_GROUPS, S5_STATE), s5i_s.reshape(bs, S5_GROUPS, S5_STATE)))

        ymix = jnp.concatenate([jnp.concatenate([y_ssd_p, y_ssd_s], axis=0), y_rw, y_s5], axis=1)
        x = _mm(ymix, w_out[i].astype(BF16), residual=x)

        q = _mm(x, wq_mem[i].astype(BF16), norm_w=norm_mem[i])
        o = jnp.concatenate([
            _attend(q, 0, bp, lp, kv, kv, 0, 1, lq=512, sb=1),
            _attend(q, tp, bs, ls, cache_mem_k[i].reshape(bs * MEM_TOKENS, MEM_WIDTH),
                    cache_mem_v[i].reshape(bs * MEM_TOKENS, MEM_WIDTH), 0, 0, lq=ls, sb=8)], axis=0)
        x = _mm(o, wo_mem[i].astype(BF16), residual=x)

        j = i // 2
        if i % 2 == 0:
            x = _ffn(x, norm_ffn[i], ffn_w1[j].astype(BF16), ffn_w3[j].astype(BF16), ffn_w2[j].astype(BF16))
        else:
            x = _moe(x, norm_ffn[i], moe_router_w[j], moe_router_b[j], moe_w1[j], moe_w3[j], moe_w2[j])

    y = _rms_rows(x, final_norm_w)
    y_prompt = y[:tp].reshape(bp, lp, d)
    y_sample = y[tp:].reshape(bs, ls, d)

    def stk(lst, j):
        return jnp.stack([s[j] for s in lst])

    return (y_prompt, y_sample, jnp.stack(p_mk), jnp.stack(p_mv),
            stk(p_st, 0), stk(p_st, 1), stk(p_st, 2), stk(p_st, 3), stk(p_st, 4), stk(p_st, 5),
            stk(s_st, 0), stk(s_st, 1), stk(s_st, 2), stk(s_st, 3), stk(s_st, 4), stk(s_st, 5))
```

```python
import functools

import jax
import jax.numpy as jnp
from jax import lax
from jax.experimental import pallas as pl
from jax.experimental.pallas import tpu as pltpu

D_MODEL = 2048
DEPTH = 2
SSD_WIDTH = 1024
SSD_HEAD_DIM = 64
SSD_HEADS = 16
SSD_GROUPS = 2
SSD_STATE = 128
SSD_CONV = 4
SSD_CONV_DIM = 1536
SSD_CHUNK = 64
RWKV_WIDTH = 512
RWKV_HEAD_DIM = 64
RWKV_HEADS = 8
RWKV_W_LORA = 64
RWKV_A_LORA = 64
RWKV_G_LORA = 128
RWKV_SHIFT_DIM = 1792
RWKV_GN_EPS = 64e-5
S5_WIDTH = 512
S5_GROUP = 16
S5_GROUPS = 32
S5_STATE = 64
S5_CHANNELS = S5_GROUPS * S5_STATE
S5_SLABS = 4
S5_SLAB_CH = S5_CHANNELS // S5_SLABS
MEM_TOKENS = 256
MEM_HEADS = 4
MEM_HEAD_DIM = 128
MEM_WIDTH = 512
N_EXPERTS = 8
TOP_K = 2
RMS_EPS = 1e-6

F32 = jnp.float32
BF16 = jnp.bfloat16
HIGHEST = lax.Precision.HIGHEST

PROJ_Z = 0
PROJ_XBC = PROJ_Z + SSD_WIDTH
PROJ_HR = PROJ_XBC + SSD_CONV_DIM
PROJ_U = PROJ_HR + RWKV_SHIFT_DIM
PROJ_DT = PROJ_U + S5_WIDTH
LANES = 128
SUBLANES = 8
PROJ_WIDTH = PROJ_DT + 2 * LANES

VMEM_LIMIT = 56 * 1024 * 1024
ROW_TILE = 512
MM_ROW_TILE = 1024
MOE_ROW_TILE = 1024
MOE_SUB_TILE = 512


def _rms_rows(x, w):
    return x * lax.rsqrt(jnp.mean(x * x, axis=-1, keepdims=True) + RMS_EPS) * w


def _sigmoid(x):
    return 1.0 / (1.0 + jnp.exp(-x))


def _silu(x):
    return x * _sigmoid(x)


def _softplus(x):
    return jnp.maximum(x, 0.0) + jnp.log1p(jnp.exp(-jnp.abs(x)))


def _dot(a, b):
    return jnp.dot(a, b, preferred_element_type=F32)


def _dot_nt(a, b):
    return lax.dot_general(a, b, (((1,), (1,)), ((), ())), preferred_element_type=F32)


def _dot_exact(a, b):
    return jnp.dot(a, b, precision=HIGHEST, preferred_element_type=F32)


def _params(*sem):
    return pltpu.CompilerParams(dimension_semantics=sem, vmem_limit_bytes=VMEM_LIMIT)


def _mm_kernel(*refs, has_norm, has_res):
    refs = list(refs)
    a_ref = refs.pop(0)
    nw_ref = refs.pop(0) if has_norm else None
    w_ref = refs.pop(0)
    res_ref = refs.pop(0) if has_res else None
    o_ref = refs.pop(0)
    abf_ref = refs.pop(0)

    @pl.when(pl.program_id(1) == 0)
    def _():
        a = a_ref[...]
        if has_norm:
            a = _rms_rows(a, nw_ref[...])
        abf_ref[...] = a.astype(BF16)

    acc = _dot(abf_ref[...], w_ref[...])
    if has_res:
        acc = acc + res_ref[...]
    o_ref[...] = acc


def _col_tile(n):
    for t in (1024, 512, 256, 128):
        if n % t == 0:
            return t
    raise ValueError(f"unsupported matmul width {n}")


def _mm(a, w, norm_w=None, residual=None):
    m, k = a.shape
    n = w.shape[1]
    tm, tn = (MM_ROW_TILE if m % MM_ROW_TILE == 0 else ROW_TILE), _col_tile(n)
    assert m % tm == 0
    has_norm, has_res = norm_w is not None, residual is not None
    in_specs = [pl.BlockSpec((tm, k), lambda i, j: (i, 0))]
    args = [a]
    if has_norm:
        in_specs.append(pl.BlockSpec((1, k), lambda i, j: (0, 0)))
        args.append(norm_w.reshape(1, k))
    in_specs.append(pl.BlockSpec((k, tn), lambda i, j: (0, j)))
    args.append(w)
    if has_res:
        in_specs.append(pl.BlockSpec((tm, tn), lambda i, j: (i, j)))
        args.append(residual)
    return pl.pallas_call(
        functools.partial(_mm_kernel, has_norm=has_norm, has_res=has_res),
        out_shape=jax.ShapeDtypeStruct((m, n), F32),
        grid=(m // tm, n // tn),
        in_specs=in_specs,
        out_specs=pl.BlockSpec((tm, tn), lambda i, j: (i, j)),
        scratch_shapes=[pltpu.VMEM((tm, k), BF16)],
        compiler_params=_params("parallel", "arbitrary"),
        name="matmul",
    )(*args)


def _ffn_kernel(x_ref, nw_ref, w1_ref, w3_ref, w2_ref, o_ref, h_ref, acc_ref):
    f = pl.program_id(1)

    @pl.when(f == 0)
    def _():
        h_ref[...] = _rms_rows(x_ref[...], nw_ref[...]).astype(BF16)
        acc_ref[...] = jnp.zeros_like(acc_ref)

    h = h_ref[...]
    g = _dot(h, w1_ref[...])
    u = _dot(h, w3_ref[...])
    a = (g * jax.nn.sigmoid(g) * u).astype(BF16)
    acc_ref[...] += _dot(a, w2_ref[...])

    @pl.when(f == pl.num_programs(1) - 1)
    def _():
        o_ref[...] = x_ref[...] + acc_ref[...]


def _ffn(x, norm_w, w1, w3, w2):
    m, d = x.shape
    dff = w1.shape[1]
    tm, tf = ROW_TILE, 512
    return pl.pallas_call(
        _ffn_kernel,
        out_shape=jax.ShapeDtypeStruct((m, d), F32),
        grid=(m // tm, dff // tf),
        in_specs=[
            pl.BlockSpec((tm, d), lambda i, f: (i, 0)),
            pl.BlockSpec((1, d), lambda i, f: (0, 0)),
            pl.BlockSpec((d, tf), lambda i, f: (0, f)),
            pl.BlockSpec((d, tf), lambda i, f: (0, f)),
            pl.BlockSpec((tf, d), lambda i, f: (f, 0)),
        ],
        out_specs=pl.BlockSpec((tm, d), lambda i, f: (i, 0)),
        scratch_shapes=[pltpu.VMEM((tm, d), BF16), pltpu.VMEM((tm, d), F32)],
        compiler_params=_params("parallel", "arbitrary"),
        name="ffn_swiglu",
    )(x, norm_w.reshape(1, d), w1, w3, w2)


def _moe_kernel(be_ref, nr_ref, x_ref, w1_ref, w3_ref, w2_ref, o_ref):
    b, f = pl.program_id(0), pl.program_id(1)

    @pl.when(f == 0)
    def _():
        o_ref[...] = jnp.zeros_like(o_ref)

    for sub in range(MOE_ROW_TILE // MOE_SUB_TILE):
        rows = slice(sub * MOE_SUB_TILE, (sub + 1) * MOE_SUB_TILE)

        @pl.when(nr_ref[b] > sub * MOE_SUB_TILE)
        def _():
            h = x_ref[rows, :]
            g = _dot(h, w1_ref[0].astype(BF16))
            u = _dot(h, w3_ref[0].astype(BF16))
            a = (g * jax.nn.sigmoid(g) * u).astype(BF16)
            o_ref[rows, :] += _dot(a, w2_ref[0].astype(BF16))


def _moe_experts(xb, block_e, block_rows, w1, w3, w2):
    cap, d = xb.shape
    dff = w1.shape[2]
    tm, tf = MOE_ROW_TILE, 256
    nb, nf = cap // tm, dff // tf

    def w13_map(b, f, be, nr):
        return (be[b], 0, jnp.where(nr[b] > 0, f, nf - 1))

    def w2_map(b, f, be, nr):
        return (be[b], jnp.where(nr[b] > 0, f, nf - 1), 0)

    grid_spec = pltpu.PrefetchScalarGridSpec(
        num_scalar_prefetch=2,
        grid=(nb, nf),
        in_specs=[
            pl.BlockSpec((tm, d), lambda b, f, be, nr: (b, 0)),
            pl.BlockSpec((1, d, tf), w13_map),
            pl.BlockSpec((1, d, tf), w13_map),
            pl.BlockSpec((1, tf, d), w2_map),
        ],
        out_specs=pl.BlockSpec((tm, d), lambda b, f, be, nr: (b, 0)),
    )
    return pl.pallas_call(
        _moe_kernel,
        out_shape=jax.ShapeDtypeStruct((cap, d), F32),
        grid_spec=grid_spec,
        compiler_params=_params("arbitrary", "arbitrary"),
        name="moe_swiglu",
    )(block_e, block_rows, xb, w1, w3, w2)


def _route_kernel(x_ref, nw_ref, wr_ref, br_ref, h_ref, lg_ref):
    h = _rms_rows(x_ref[...], nw_ref[...])
    h_ref[...] = h.astype(BF16)
    hh = h.astype(BF16)
    hl = (h - hh.astype(F32)).astype(BF16)
    wr = wr_ref[...]
    wh = wr.astype(BF16)
    wl = (wr - wh.astype(F32)).astype(BF16)
    lg_ref[...] = _dot(hh, wh) + _dot(hl, wh) + _dot(hh, wl) + br_ref[...]


def _norm_route(x, norm_w, w_router, b_router):
    t, d = x.shape
    tm = ROW_TILE
    wr = jnp.pad(w_router, ((0, 0), (0, LANES - N_EXPERTS)))
    br = jnp.pad(b_router, (0, LANES - N_EXPERTS)).reshape(1, LANES)
    h, lg = pl.pallas_call(
        _route_kernel,
        out_shape=(jax.ShapeDtypeStruct((t, d), BF16), jax.ShapeDtypeStruct((t, LANES), F32)),
        grid=(t // tm,),
        in_specs=[pl.BlockSpec((tm, d), lambda i: (i, 0)), pl.BlockSpec((1, d), lambda i: (0, 0)),
                  pl.BlockSpec((d, LANES), lambda i: (0, 0)), pl.BlockSpec((1, LANES), lambda i: (0, 0))],
        out_specs=(pl.BlockSpec((tm, d), lambda i: (i, 0)), pl.BlockSpec((tm, LANES), lambda i: (i, 0))),
        compiler_params=_params("parallel"),
        name="moe_route",
    )(x, norm_w.reshape(1, d), wr, br)
    return h, lg[:, :N_EXPERTS]


def _final_norm_kernel(x_ref, w_ref, o_ref):
    o_ref[...] = _rms_rows(x_ref[...], w_ref[...])


def _final_norm(x, w):
    t, d = x.shape
    tm = ROW_TILE
    return pl.pallas_call(
        _final_norm_kernel,
        out_shape=jax.ShapeDtypeStruct((t, d), F32),
        grid=(t // tm,),
        in_specs=[pl.BlockSpec((tm, d), lambda i: (i, 0)), pl.BlockSpec((1, d), lambda i: (0, 0))],
        out_specs=pl.BlockSpec((tm, d), lambda i: (i, 0)),
        compiler_params=_params("parallel"),
        name="final_norm",
    )(x, w.reshape(1, d))


def _moe(x, norm_w, w_router, b_router, w1, w3, w2):
    t, d = x.shape
    tm = MOE_ROW_TILE
    h, logits = _norm_route(x, norm_w, w_router, b_router)
    lj, le = logits[:, None, :], logits[:, :, None]
    eidx = jnp.arange(N_EXPERTS, dtype=jnp.int32)
    beats = (lj > le) | ((lj == le) & (eidx[None, None, :] < eidx[None, :, None]))
    rank = jnp.sum(beats.astype(jnp.int32), axis=-1)
    sel = jnp.stack([rank == k for k in range(TOP_K)], axis=1)
    top_v = jnp.sum(jnp.where(sel, logits[:, None, :], 0.0), axis=-1)
    gates = jax.nn.softmax(top_v, axis=-1)
    n_slots = t * TOP_K
    oh = sel.reshape(n_slots, N_EXPERTS).astype(jnp.int32)
    counts = jnp.sum(oh, axis=0)
    padded = (counts + tm - 1) // tm * tm
    pad_ends = jnp.cumsum(padded)
    pad_starts = pad_ends - padded
    within = jnp.cumsum(oh, axis=0) - oh
    slot_pos = jnp.sum(oh * (within + pad_starts[None, :]), axis=-1)
    nb = n_slots // tm + N_EXPERTS
    cap = nb * tm
    buf_tok = jnp.zeros((cap,), jnp.int32).at[slot_pos].set(jnp.arange(n_slots, dtype=jnp.int32) // TOP_K)
    block_start = jnp.arange(nb, dtype=jnp.int32) * tm
    block_e = jnp.minimum(jnp.sum((pad_ends[None, :] <= block_start[:, None]).astype(jnp.int32), axis=-1),
                          N_EXPERTS - 1)
    block_rows = jnp.clip(counts[block_e] - (block_start - pad_starts[block_e]), 0, tm).astype(jnp.int32)
    xb = h[buf_tok]
    yb = _moe_experts(xb, block_e.astype(jnp.int32), block_rows, w1, w3, w2)
    pos = slot_pos.reshape(t, TOP_K)
    y = yb[pos[:, 0]] * gates[:, 0:1] + yb[pos[:, 1]] * gates[:, 1:2]
    return x + y


def _s5_kernel(u_ref, sre_ref, sim_ref, are_ref, aim_ref, bb_ref, cre_ref, cim_ref, d_ref,
               gw_ref, gb_ref, nw_ref, y_ref, ore_ref, oim_ref, xr_ref, xi_ref, st_ref, *, tc, nbb):
    c = pl.program_id(1)
    rows = tc * nbb
    u = u_ref[...].reshape(rows, S5_WIDTH)
    ub = u.astype(BF16)
    for sl in range(S5_SLABS):
        bu = _dot(ub[:, sl * LANES:(sl + 1) * LANES], bb_ref[sl])
        xr_ref[:, sl * S5_SLAB_CH:(sl + 1) * S5_SLAB_CH] = bu[:, :S5_SLAB_CH]
        xi_ref[:, sl * S5_SLAB_CH:(sl + 1) * S5_SLAB_CH] = bu[:, S5_SLAB_CH:]

    @pl.when(c == 0)
    def _():
        st_ref[0] = sre_ref[...]
        st_ref[1] = sim_ref[...]

    ar = are_ref[...]
    ai = aim_ref[...]

    def step(t, carry):
        for g in range(nbb // SUBLANES):
            r0 = pl.multiple_of(t * nbb + g * SUBLANES, SUBLANES)
            sl = slice(g * SUBLANES, (g + 1) * SUBLANES)
            pr = st_ref[0, sl, :]
            pi = st_ref[1, sl, :]
            nr = ar * pr - ai * pi + xr_ref[pl.ds(r0, SUBLANES), :]
            ni = ar * pi + ai * pr + xi_ref[pl.ds(r0, SUBLANES), :]
            xr_ref[pl.ds(r0, SUBLANES), :] = nr
            xi_ref[pl.ds(r0, SUBLANES), :] = ni
            st_ref[0, sl, :] = nr
            st_ref[1, sl, :] = ni
        return carry

    lax.fori_loop(0, tc, step, 0)

    ys = []
    for sl in range(S5_SLABS):
        ch = slice(sl * S5_SLAB_CH, (sl + 1) * S5_SLAB_CH)
        ys.append(_dot(xr_ref[:, ch].astype(BF16), cre_ref[sl]) - _dot(xi_ref[:, ch].astype(BF16), cim_ref[sl]))
    y = jnp.concatenate(ys, axis=1) + d_ref[...] * u
    gy = 0.5 * y * (1.0 + jnp.tanh(0.7978845608028654 * (y + 0.044715 * (y * y * y))))
    y = gy * _sigmoid(_dot(gy.astype(BF16), gw_ref[...]) + gb_ref[...])
    y_ref[...] = _rms_rows(y, nw_ref[...]).reshape(tc, nbb, S5_WIDTH)

    @pl.when(c == pl.num_programs(1) - 1)
    def _():
        ore_ref[...] = st_ref[0]
        oim_ref[...] = st_ref[1]


def _s5(u_tm, st_re, st_im, prm, *, tc, nbb):
    L, n, _ = u_tm.shape
    ch = S5_CHANNELS
    vec = lambda w: pl.BlockSpec((1, w), lambda s, c: (0, 0))
    mat = lambda a, b: pl.BlockSpec((a, b), lambda s, c: (0, 0))
    slab = lambda a, b: pl.BlockSpec((S5_SLABS, a, b), lambda s, c: (0, 0, 0))
    st_spec = pl.BlockSpec((nbb, ch), lambda s, c: (s, 0))
    return pl.pallas_call(
        functools.partial(_s5_kernel, tc=tc, nbb=nbb),
        out_shape=(jax.ShapeDtypeStruct((L, n, S5_WIDTH), F32),
                   jax.ShapeDtypeStruct((n, ch), F32), jax.ShapeDtypeStruct((n, ch), F32)),
        grid=(n // nbb, L // tc),
        in_specs=[pl.BlockSpec((tc, nbb, S5_WIDTH), lambda s, c: (c, s, 0)), st_spec, st_spec,
                  vec(ch), vec(ch), slab(LANES, 2 * S5_SLAB_CH), slab(S5_SLAB_CH, LANES), slab(S5_SLAB_CH, LANES),
                  vec(S5_WIDTH), mat(S5_WIDTH, S5_WIDTH), vec(S5_WIDTH), vec(S5_WIDTH)],
        out_specs=(pl.BlockSpec((tc, nbb, S5_WIDTH), lambda s, c: (c, s, 0)), st_spec, st_spec),
        scratch_shapes=[pltpu.VMEM((tc * nbb, ch), F32), pltpu.VMEM((tc * nbb, ch), F32),
                        pltpu.VMEM((2, nbb, ch), F32)],
        compiler_params=_params("parallel", "arbitrary"),
        name="s5_mixer",
    )(u_tm, st_re, st_im, *prm)


def _s5_params(lam_re, lam_im, b_re, b_im, c_re, c_im, d_skip, log_dt, glu_w, glu_b, norm_w):
    delta = jnp.exp(log_dt)[:, None]
    mag = jnp.exp(lam_re * delta)
    ab_re, ab_im = mag * jnp.cos(lam_im * delta), mag * jnp.sin(lam_im * delta)
    den = lam_re * lam_re + lam_im * lam_im
    q_re = ((ab_re - 1.0) * lam_re + ab_im * lam_im) / den
    q_im = (ab_im * lam_re - (ab_re - 1.0) * lam_im) / den
    bb_re = q_re[..., None] * b_re - q_im[..., None] * b_im
    bb_im = q_re[..., None] * b_im + q_im[..., None] * b_re
    gs = S5_GROUPS // S5_SLABS
    eye = jnp.eye(gs, dtype=F32)

    def in_blockdiag(bb):
        t = jnp.swapaxes(bb, 1, 2).reshape(S5_SLABS, gs, S5_GROUP, S5_STATE)
        return (eye[None, :, None, :, None] * t[:, :, :, None, :]).reshape(S5_SLABS, LANES, S5_SLAB_CH)

    def out_blockdiag(cc):
        t = jnp.swapaxes(cc, 1, 2).reshape(S5_SLABS, gs, S5_STATE, S5_GROUP)
        return (eye[None, :, None, :, None] * t[:, :, :, None, :]).reshape(S5_SLABS, S5_SLAB_CH, LANES).astype(BF16)

    bb = jnp.concatenate([in_blockdiag(bb_re), in_blockdiag(bb_im)], axis=2).astype(BF16)
    return (ab_re.reshape(1, S5_CHANNELS), ab_im.reshape(1, S5_CHANNELS), bb,
            out_blockdiag(c_re), out_blockdiag(c_im), d_skip.reshape(1, S5_WIDTH), glu_w.astype(BF16),
            glu_b.reshape(1, S5_WIDTH), norm_w.reshape(1, S5_WIDTH))


def _ssd_kernel(p_ref, st_ref, cs_ref, cw_ref, cb_ref, dtb_ref, a_ref, dsk_ref, nw_ref,
                y_ref, ost_ref, ocs_ref, ext_ref, win_ref, s_ref, *, q, sb):
    c = pl.program_id(1)
    last = c == pl.num_programs(1) - 1
    hd, nh, gw = SSD_HEAD_DIM, SSD_HEADS, SSD_WIDTH // SSD_GROUPS
    pad_rows = hd - q

    lane = lax.broadcasted_iota(jnp.int32, (q, LANES), 1)
    row = lax.broadcasted_iota(jnp.int32, (q, LANES), 0)
    causal2 = row >= (lane % hd)
    lane64 = lax.broadcasted_iota(jnp.int32, (hd, LANES), 1)
    tri = (lax.broadcasted_iota(jnp.int32, (q, q), 0) >= lax.broadcasted_iota(jnp.int32, (q, q), 1)).astype(F32)
    e_h = lax.broadcasted_iota(jnp.int32, (LANES, SSD_WIDTH), 0)
    e_c = lax.broadcasted_iota(jnp.int32, (LANES, SSD_WIDTH), 1)
    expand = (e_h == e_c // hd).astype(F32)
    i_s = lax.broadcasted_iota(jnp.int32, (q, SSD_WIDTH), 0)
    i_c = lax.broadcasted_iota(jnp.int32, (q, SSD_WIDTH), 1)
    eye_x = (i_s == i_c % hd).astype(F32)

    for s in range(sb):
        rs = slice(s * q, (s + 1) * q)

        @pl.when(c == 0)
        def _():
            win_ref[s, 0:5, :] = jnp.zeros((5, SSD_CONV_DIM), F32)
            win_ref[s, 5:8, :] = cs_ref[s]
            for g in range(SSD_GROUPS):
                for k in range(gw // LANES):
                    r0 = g * gw + k * LANES
                    s_ref[s, g, :, k * LANES:(k + 1) * LANES] = st_ref[s, r0:r0 + LANES, :].T

        z = p_ref[rs, PROJ_Z:PROJ_XBC]
        xbc = p_ref[rs, PROJ_XBC:PROJ_HR]
        dt = p_ref[rs, PROJ_DT:PROJ_DT + LANES]
        ext_ref[0:8, :] = win_ref[s]
        ext_ref[8:8 + q, :] = xbc
        conv = cb_ref[...]
        for j in range(SSD_CONV):
            conv = conv + cw_ref[j:j + 1, :] * ext_ref[pl.ds(5 + j, q), :]
        win_ref[s] = ext_ref[q:q + 8, :]
        xc = _silu(conv)
        xs = xc[:, :SSD_WIDTH]
        bm = xc[:, SSD_WIDTH:SSD_WIDTH + SSD_GROUPS * SSD_STATE]
        cm = xc[:, SSD_WIDTH + SSD_GROUPS * SSD_STATE:]

        step = _softplus(dt + dtb_ref[...])
        adt = step * a_ref[...]
        step_x = _dot_exact(step, expand)
        acs_x = _dot_exact(tri, _dot_exact(adt, expand))
        diag = jnp.sum(acs_x * eye_x, axis=0, keepdims=True)
        acs_last = acs_x[q - 1:q, :]
        xdt = xs * step_x
        exp_acs = jnp.exp(acs_x)
        xw = xdt * jnp.exp(acs_last - acs_x)
        dec = jnp.exp(acs_last)

        for g in range(SSD_GROUPS):
            bg = bm[:, g * SSD_STATE:(g + 1) * SSD_STATE]
            cg = cm[:, g * SSD_STATE:(g + 1) * SSD_STATE].astype(BF16)
            gl = slice(g * gw, (g + 1) * gw)
            b64 = bg if pad_rows == 0 else jnp.concatenate([bg, jnp.zeros((pad_rows, SSD_STATE), F32)], axis=0)
            cb2 = _dot_nt(cg, jnp.concatenate([b64, b64], axis=0).astype(BF16))
            sg = s_ref[s, g]
            yoff = _dot(cg, sg.astype(BF16)) * exp_acs[:, gl]
            for pr in range(gw // LANES):
                l0 = g * gw + pr * LANES
                seg = acs_x[:, l0:l0 + LANES] - diag[:, l0:l0 + LANES]
                m = cb2 * jnp.exp(jnp.where(causal2, seg, -jnp.inf))
                xd = xdt[:, l0:l0 + LANES]
                xd64 = xd if pad_rows == 0 else jnp.concatenate([xd, jnp.zeros((pad_rows, LANES), F32)], axis=0)
                rhs = jnp.concatenate([jnp.where(lane64 < hd, xd64, 0.0), jnp.where(lane64 >= hd, xd64, 0.0)], axis=0)
                ydiag = _dot(m.astype(BF16), rhs.astype(BF16))
                y_ref[rs, l0:l0 + LANES] = ydiag + yoff[:, pr * LANES:(pr + 1) * LANES]
            bpad = jnp.concatenate([bg, jnp.zeros((LANES - q, SSD_STATE), F32)], axis=0)
            xwpad = jnp.concatenate([xw[:, gl], jnp.zeros((LANES - q, gw), F32)], axis=0)
            s_ref[s, g] = dec[:, gl] * sg + _dot(bpad.T.astype(BF16), xwpad.astype(BF16))

        y = y_ref[rs, :] + dsk_ref[...] * xs
        y = y * _silu(z)
        halves = []
        for g in range(SSD_GROUPS):
            yg = y[:, g * gw:(g + 1) * gw]
            halves.append(yg * lax.rsqrt(jnp.mean(yg * yg, axis=-1, keepdims=True) + RMS_EPS))
        y_ref[rs, :] = jnp.concatenate(halves, axis=1) * nw_ref[...]

        @pl.when(last)
        def _():
            ocs_ref[s] = ext_ref[q + 5:q + 8, :]
            for g in range(SSD_GROUPS):
                for k in range(gw // LANES):
                    r0 = g * gw + k * LANES
                    ost_ref[s, r0:r0 + LANES, :] = s_ref[s, g, :, k * LANES:(k + 1) * LANES].T


def _ssd(proj, row0, nseq, L, st, cs, prm, *, sb):
    q = min(L, SSD_CHUNK)
    nchunk = L // q
    rows = sb * q
    base = row0 // rows
    assert row0 % rows == 0 and nseq % sb == 0
    vec = lambda w: pl.BlockSpec((1, w), lambda s, c: (0, 0))
    st_spec = pl.BlockSpec((sb, SSD_WIDTH, SSD_STATE), lambda s, c: (s, 0, 0))
    cs_spec = pl.BlockSpec((sb, SSD_CONV - 1, SSD_CONV_DIM), lambda s, c: (s, 0, 0))
    return pl.pallas_call(
        functools.partial(_ssd_kernel, q=q, sb=sb),
        out_shape=(jax.ShapeDtypeStruct((nseq * L, SSD_WIDTH), F32),
                   jax.ShapeDtypeStruct((nseq, SSD_WIDTH, SSD_STATE), F32),
                   jax.ShapeDtypeStruct((nseq, SSD_CONV - 1, SSD_CONV_DIM), F32)),
        grid=(nseq // sb, nchunk),
        in_specs=[pl.BlockSpec((rows, PROJ_WIDTH), lambda s, c: (base + s * nchunk + c, 0)), st_spec, cs_spec,
                  pl.BlockSpec((SSD_CONV, SSD_CONV_DIM), lambda s, c: (0, 0)), vec(SSD_CONV_DIM),
                  vec(LANES), vec(LANES), vec(SSD_WIDTH), vec(SSD_WIDTH)],
        out_specs=(pl.BlockSpec((rows, SSD_WIDTH), lambda s, c: (s * nchunk + c, 0)), st_spec, cs_spec),
        scratch_shapes=[pltpu.VMEM((q + 8, SSD_CONV_DIM), F32), pltpu.VMEM((sb, 8, SSD_CONV_DIM), F32),
                        pltpu.VMEM((sb, SSD_GROUPS, SSD_STATE, SSD_WIDTH // SSD_GROUPS), F32)],
        compiler_params=_params("parallel", "arbitrary"),
        name="ssd_mixer",
    )(proj, st, cs, *prm)


def _ssd_params(conv_w, conv_b, dt_bias, a_log, d_skip, norm_w):
    pad = jnp.zeros((LANES - SSD_HEADS,), F32)
    return (conv_w, conv_b.reshape(1, SSD_CONV_DIM), jnp.concatenate([dt_bias, pad]).reshape(1, LANES),
            jnp.concatenate([-jnp.exp(a_log), pad]).reshape(1, LANES),
            jnp.repeat(d_skip, SSD_HEAD_DIM).reshape(1, SSD_WIDTH), norm_w.reshape(1, SSD_WIDTH))


def _attn_kernel(q_ref, k_ref, v_ref, o_ref, *, lq, sb):
    scale = MEM_HEAD_DIM ** -0.5
    for s in range(sb):
        rq = slice(s * lq, (s + 1) * lq)
        rk = slice(s * MEM_TOKENS, (s + 1) * MEM_TOKENS)
        for h in range(MEM_HEADS):
            cl = slice(h * MEM_HEAD_DIM, (h + 1) * MEM_HEAD_DIM)
            sc = _dot_nt(q_ref[rq, cl].astype(BF16), k_ref[rk, cl].astype(BF16)) * scale
            sc = sc - jnp.max(sc, axis=-1, keepdims=True)
            p = jnp.exp(sc)
            p = p / jnp.sum(p, axis=-1, keepdims=True)
            o_ref[rq, cl] = _dot(p.astype(BF16), v_ref[rk, cl].astype(BF16))


def _attend(q, row0, nseq, L, k2d, v2d, kcol, vcol, *, lq, sb):
    nl = L // lq
    rows = sb * lq
    base = row0 // rows
    assert row0 % rows == 0 and (sb == 1 or nl == 1)
    return pl.pallas_call(
        functools.partial(_attn_kernel, lq=lq, sb=sb),
        out_shape=jax.ShapeDtypeStruct((nseq * L, MEM_WIDTH), F32),
        grid=(nseq // sb, nl),
        in_specs=[pl.BlockSpec((rows, MEM_WIDTH), lambda s, l: (base + s * nl + l, 0)),
                  pl.BlockSpec((sb * MEM_TOKENS, MEM_WIDTH), lambda s, l: (s, kcol)),
                  pl.BlockSpec((sb * MEM_TOKENS, MEM_WIDTH), lambda s, l: (s, vcol))],
        out_specs=pl.BlockSpec((rows, MEM_WIDTH), lambda s, l: (s * nl + l, 0)),
        compiler_params=_params("parallel", "arbitrary"),
        name="mem_attention",
    )(q, k2d, v2d)


def _head_sum(x, ones_bd):
    return _dot_exact(x, ones_bd)


def _rwkv_prep_kernel(h_ref, p_ref, mu_ref, wl_ref, w0_ref, a0_ref, kk_ref, ka_ref, rk_ref, ones_ref,
                      r_ref, w_ref, k_ref, v_ref, n_ref, b_ref, g_ref, bo_ref):
    W = RWKV_WIDTH
    h = h_ref[...]
    hs = h + (p_ref[...] - h) * mu_ref[...]
    r, k, v = hs[:, :W], hs[:, W:2 * W], hs[:, 2 * W:3 * W]
    lo = hs[:, 3 * W:]
    lane = lax.broadcasted_iota(jnp.int32, lo.shape, 1)
    act = jnp.where(lane < RWKV_W_LORA, jnp.tanh(lo),
                    jnp.where(lane < RWKV_W_LORA + RWKV_A_LORA, lo, _sigmoid(lo)))
    lora = _dot(act.astype(BF16), wl_ref[...])
    w_log = -_softplus(-(w0_ref[...] + lora[:, :W])) - 0.5
    a = _sigmoid(a0_ref[...] + lora[:, W:2 * W])
    ones_bd = ones_ref[...]
    kk = k * kk_ref[...]
    kk = kk / jnp.maximum(jnp.sqrt(_head_sum(kk * kk, ones_bd)), 1e-12)
    k2 = k * (1.0 + (a - 1.0) * ka_ref[...])
    r_ref[...] = r
    w_ref[...] = -jnp.exp(w_log)
    k_ref[...] = k2
    v_ref[...] = v
    n_ref[...] = kk
    b_ref[...] = kk * a
    g_ref[...] = lora[:, 2 * W:]
    bo_ref[...] = _head_sum(r * k2 * rk_ref[...], ones_bd) * v


def _rwkv_prep(hr, prev, prm):
    m = hr.shape[0]
    tm, W = ROW_TILE, RWKV_WIDTH
    mu, wl, w0, a0, k_k, k_a, r_k, ones_bd = prm
    vec = lambda w: pl.BlockSpec((1, w), lambda i: (0, 0))
    row = lambda w: pl.BlockSpec((tm, w), lambda i: (i, 0))
    return pl.pallas_call(
        _rwkv_prep_kernel,
        out_shape=tuple(jax.ShapeDtypeStruct((m, W), F32) for _ in range(8)),
        grid=(m // tm,),
        in_specs=[row(RWKV_SHIFT_DIM), row(RWKV_SHIFT_DIM), vec(RWKV_SHIFT_DIM),
                  pl.BlockSpec(wl.shape, lambda i: (0, 0)), vec(W), vec(W), vec(W), vec(W), vec(W),
                  pl.BlockSpec((W, W), lambda i: (0, 0))],
        out_specs=tuple(row(W) for _ in range(8)),
        compiler_params=_params("parallel"),
        name="rwkv_prep",
    )(hr, prev, mu, wl, w0, a0, k_k, k_a, r_k, ones_bd)


def _rwkv_post_kernel(y_ref, bo_ref, g_ref, gw_ref, gb_ref, ones_ref, o_ref):
    ones_bd = ones_ref[...]
    y = y_ref[...]
    inv = 1.0 / RWKV_HEAD_DIM
    d = y - _head_sum(y, ones_bd) * inv
    var = _head_sum(d * d, ones_bd) * inv
    yn = d * lax.rsqrt(var + RWKV_GN_EPS) * gw_ref[...] + gb_ref[...]
    o_ref[...] = (yn + bo_ref[...]) * g_ref[...]


def _rwkv_post(y, bonus, g, gn_w, gn_b, ones_bd):
    m, W = y.shape
    tm = ROW_TILE
    vec = pl.BlockSpec((1, W), lambda i: (0, 0))
    row = pl.BlockSpec((tm, W), lambda i: (i, 0))
    return pl.pallas_call(
        _rwkv_post_kernel,
        out_shape=jax.ShapeDtypeStruct((m, W), F32),
        grid=(m // tm,),
        in_specs=[row, row, row, vec, vec, pl.BlockSpec((W, W), lambda i: (0, 0))],
        out_specs=row,
        compiler_params=_params("parallel"),
        name="rwkv_post",
    )(y, bonus, g, gn_w.reshape(1, W), gn_b.reshape(1, W), ones_bd)


def _rwkv_params(mu, w0, w2, a0, a2, g2, k_k, k_a, r_k):
    W = RWKV_WIDTH
    nl = RWKV_W_LORA + RWKV_A_LORA + RWKV_G_LORA
    wl = jnp.zeros((nl, 3 * W), F32)
    wl = wl.at[:RWKV_W_LORA, :W].set(w2)
    wl = wl.at[RWKV_W_LORA:RWKV_W_LORA + RWKV_A_LORA, W:2 * W].set(a2)
    wl = wl.at[RWKV_W_LORA + RWKV_A_LORA:, 2 * W:].set(g2)
    head = jnp.arange(W) // RWKV_HEAD_DIM
    ones_bd = (head[:, None] == head[None, :]).astype(F32)
    v = lambda t: t.reshape(1, -1)
    return (v(mu), wl.astype(BF16), v(w0), v(a0), v(k_k), v(k_a), v(r_k), ones_bd)


def _bdot(a, b):
    return jnp.dot(a.astype(BF16), b.astype(BF16), preferred_element_type=F32)


def _bdot_nt(a, b):
    return lax.dot_general(a.astype(BF16), b.astype(BF16), (((1,), (1,)), ((), ())), preferred_element_type=F32)


def _split(x):
    hi = x.astype(BF16)
    return hi, (x - hi.astype(F32)).astype(BF16)


def _dot3(a, b):
    ah, al = _split(a)
    bh, bl = _split(b)
    return (jnp.dot(ah, bh, preferred_element_type=F32) + jnp.dot(al, bh, preferred_element_type=F32)
            + jnp.dot(ah, bl, preferred_element_type=F32))


def _dot3_nt(a, b):
    ah, al = _split(a)
    bh, bl = _split(b)
    dn = (((1,), (1,)), ((), ()))
    return (lax.dot_general(ah, bh, dn, preferred_element_type=F32)
            + lax.dot_general(al, bh, dn, preferred_element_type=F32)
            + lax.dot_general(ah, bl, dn, preferred_element_type=F32))


RWKV_GROUP = 4
RWKV_GW = RWKV_GROUP * RWKV_HEAD_DIM
RWKV_NG = RWKV_HEADS // RWKV_GROUP


def _rwkv_chunk_kernel(r_ref, ls_ref, k_ref, v_ref, n_ref, b_ref, s0_ref, y_ref, sf_ref, s_ref, *, C, ns):
    c = pl.program_id(1)
    G, GW, HD = RWKV_GROUP, RWKV_GW, RWKV_HEAD_DIM
    RI = ns * C
    R = G * RI
    SB = G * C
    groups = range(RWKV_NG)

    @pl.when(c == 0)
    def _():
        for q in groups:
            for s in range(ns):
                s_ref[q, s] = jnp.concatenate([s0_ref[s, G * q + h] for h in range(G)], axis=1)

    ri = lax.broadcasted_iota(jnp.int32, (2 * RI, RI), 0)
    ci = lax.broadcasted_iota(jnp.int32, (2 * RI, RI), 1)
    same = ((ri % RI) // C) == (ci // C)
    cum = (same & ((ri >= RI) | (ri >= ci))).astype(BF16)
    ls_all = ls_ref[...]
    l1 = ls_all.astype(BF16)
    l2f = ls_all - l1.astype(F32)
    l2 = l2f.astype(BF16)
    l3 = (l2f - l2.astype(F32)).astype(BF16)
    lw2 = (jnp.dot(cum, l1, preferred_element_type=F32) + jnp.dot(cum, l2, preferred_element_type=F32)
           + jnp.dot(cum, l3, preferred_element_type=F32))
    lw_all, lwl_all = lw2[:RI], lw2[RI:]

    lane_in = lax.broadcasted_iota(jnp.int32, (C, GW), 1) // HD
    row = lax.broadcasted_iota(jnp.int32, (R, R), 0)
    col = lax.broadcasted_iota(jnp.int32, (R, R), 1)
    ent = (row // C) == (col // C)
    strict = ent & (row > col)
    incl = ent & (row >= col)
    eye = row == col
    eye_f = eye.astype(F32)
    own = (lax.broadcasted_iota(jnp.int32, (SB, GW), 0) // C) == (lax.broadcasted_iota(jnp.int32, (SB, GW), 1) // HD)
    rows_r = lax.broadcasted_iota(jnp.int32, (R, GW), 0)
    rows_2r = lax.broadcasted_iota(jnp.int32, (2 * R, GW), 0)

    def stack(x):
        parts = []
        for s in range(ns):
            xs = x[s * C:(s + 1) * C]
            parts += [jnp.where(lane_in == h, xs, 0.0) for h in range(G)]
        return jnp.concatenate(parts, axis=0)

    def dup(x):
        parts = []
        for s in range(ns):
            parts += [x[s * C:(s + 1) * C]] * G
        return jnp.concatenate(parts, axis=0)

    st = []
    for q in groups:
        gl = slice(q * GW, (q + 1) * GW)
        lw, lwl, ls = lw_all[:, gl], lwl_all[:, gl], ls_all[:, gl]
        w_inv = jnp.exp(-lw)
        w_rest = jnp.exp(lwl - lw)
        kk, bb = k_ref[:, gl], b_ref[:, gl]
        st.append(dict(
            n_st=stack(n_ref[:, gl] * jnp.exp(lw - ls)), r_st=stack(r_ref[:, gl] * jnp.exp(lw)),
            v_st=stack(v_ref[:, gl]), bh_st=stack(bb * w_rest), kh_st=stack(kk * w_rest),
            b_dup=dup(bb * w_inv), k_dup=dup(kk * w_inv), w_c=jnp.exp(lwl)))
    for d in st:
        nr = jnp.concatenate([d['n_st'], d['r_st']], axis=0)
        gb = _bdot_nt(nr, d['b_dup'])
        gk = _bdot_nt(nr, d['k_dup'])
        d['a_nb'] = jnp.where(strict, gb[:R], 0.0)
        d['a_rb'] = jnp.where(incl, gb[R:], 0.0)
        d['a_nk'] = jnp.where(strict, gk[:R], 0.0)
        d['a_rk'] = jnp.where(incl, gk[R:], 0.0)
        d['t'] = eye_f - d['a_nb']
        d['p'] = d['a_nb']
    for _ in range(C.bit_length() - 2):
        for d in st:
            d['p'] = _bdot(d['p'], d['p'])
        for d in st:
            d['t'] = _bdot(d['t'], eye_f + d['p'])
    for d in st:
        d['p1'] = _bdot(d['t'], d['n_st'])
        d['z'] = _bdot(d['a_nk'], d['v_st'])
    for d in st:
        d['p2'] = _bdot(d['t'], d['z'])
    for q, d in enumerate(st):
        p1, p2 = d['p1'], d['p2']
        p1_t = p1.T
        lt = jnp.concatenate([d['v_st'].T, -p2.T], axis=1)
        kb = jnp.concatenate([d['kh_st'], d['bh_st']], axis=0)
        sa_parts, rs_parts = [], []
        for s in range(ns):
            rsl = slice(s * SB, (s + 1) * SB)
            S = s_ref[q, s]
            ss = jnp.concatenate([S] * G, axis=0)
            xr = _dot3_nt(jnp.concatenate([p1[rsl], d['r_st'][rsl]], axis=0), ss)
            sa_parts.append(-jnp.where(own, xr[:SB], 0.0) - p2[rsl])
            rs_parts.append(jnp.where(own, xr[SB:], 0.0))
            if ns == 1:
                bh_s, kb_s = d['bh_st'], kb
            else:
                bh_s = jnp.where((rows_r // SB) == s, d['bh_st'], 0.0)
                kb_s = jnp.where(((rows_2r % R) // SB) == s, kb, 0.0)
            m_bd = jnp.where(eye, d['w_c'][s * C:s * C + 1, :], 0.0) - _bdot(p1_t, bh_s)
            nf = _bdot(lt, kb_s)
            fold = nf[:HD]
            for h in range(1, G):
                fold = fold + nf[h * HD:(h + 1) * HD]
            s_ref[q, s] = _dot3(S, m_bd) + fold
        sa_st = jnp.concatenate(sa_parts, axis=0) if ns > 1 else sa_parts[0]
        rs_st = jnp.concatenate(rs_parts, axis=0) if ns > 1 else rs_parts[0]
        y_st = rs_st + _bdot(jnp.concatenate([d['a_rb'], d['a_rk']], axis=1),
                             jnp.concatenate([sa_st, d['v_st']], axis=0))
        for s in range(ns):
            y = y_st[s * SB:s * SB + C]
            for h in range(1, G):
                y = y + y_st[s * SB + h * C:s * SB + (h + 1) * C]
            y_ref[s * C:(s + 1) * C, q * GW:(q + 1) * GW] = y

    @pl.when(c == pl.num_programs(1) - 1)
    def _():
        for q in groups:
            for s in range(ns):
                S = s_ref[q, s]
                for h in range(G):
                    sf_ref[s, G * q + h] = S[:, h * HD:(h + 1) * HD]


def _rwkv_chunked(r, ls, k, v, kk, b, state, row0, nseq, L):
    HD, W = RWKV_HEAD_DIM, RWKV_WIDTH
    C = min(L, HD)
    ns = HD // C
    nt = L // C
    rows = ns * C
    base = row0 // rows
    assert row0 % rows == 0 and (ns == 1 or nt == 1) and nseq % ns == 0
    row_spec = pl.BlockSpec((rows, W), lambda s, c: (base + s * nt + c, 0))
    st_spec = pl.BlockSpec((ns, RWKV_HEADS, HD, HD), lambda s, c: (s, 0, 0, 0))
    return pl.pallas_call(
        functools.partial(_rwkv_chunk_kernel, C=C, ns=ns),
        out_shape=(jax.ShapeDtypeStruct((nseq * L, W), F32), jax.ShapeDtypeStruct((nseq, RWKV_HEADS, HD, HD), F32)),
        grid=(nseq // ns, nt),
        in_specs=[row_spec] * 6 + [st_spec],
        out_specs=(pl.BlockSpec((rows, W), lambda s, c: (s * nt + c, 0)), st_spec),
        scratch_shapes=[pltpu.VMEM((RWKV_NG, ns, HD, RWKV_GW), F32)],
        compiler_params=_params("parallel", "arbitrary"),
        name="rwkv_chunked",
    )(r, ls, k, v, kk, b, state)


def kernel(x_prompt, x_sample, mem_prompt, cache_mem_k, cache_mem_v, state_ssd, state_ssd_conv, state_rwkv, state_rwkv_shift, state_s5_re, state_s5_im, norm_mix, w_in, ssd_conv_w, ssd_conv_b, ssd_dt_bias, ssd_a_log, ssd_d, ssd_norm_w, rwkv_mu, rwkv_w0, rwkv_w2, rwkv_a0, rwkv_a2, rwkv_g2, rwkv_k_k, rwkv_k_a, rwkv_r_k, rwkv_gn_w, rwkv_gn_b, s5_lam_re, s5_lam_im, s5_b_re, s5_b_im, s5_c_re, s5_c_im, s5_d, s5_log_dt, s5_glu_w, s5_glu_b, s5_norm_w, w_out, norm_mem, mem_norm_w, wq_mem, wk_mem, wv_mem, wo_mem, norm_ffn, ffn_w1, ffn_w3, ffn_w2, moe_router_w, moe_router_b, moe_w1, moe_w3, moe_w2, final_norm_w):
    bp, lp, d = x_prompt.shape
    bs, ls, _ = x_sample.shape
    tp, ts = bp * lp, bs * ls
    x = jnp.concatenate([x_prompt.reshape(tp, d), x_sample.reshape(ts, d)], axis=0)
    mem_rows = mem_prompt.reshape(bp * MEM_TOKENS, d)
    s5_pad = SUBLANES - bp

    p_mk, p_mv, p_st, s_st = [], [], [], []
    for i in range(DEPTH):
        c0 = SSD_WIDTH
        c1 = c0 + SSD_CONV_DIM
        c2 = c1 + SSD_HEADS
        wi = w_in[i]
        w_in_packed = jnp.concatenate(
            [wi[:, :c1], wi[:, c2:], wi[:, c1:c2], jnp.zeros((d, PROJ_WIDTH - PROJ_DT - SSD_HEADS), F32)],
            axis=1).astype(BF16)
        proj = _mm(x, w_in_packed, norm_w=norm_mix[i])

        wkv = jnp.concatenate([wk_mem[i], wv_mem[i]], axis=1).astype(BF16)
        kv = _mm(mem_rows, wkv, norm_w=mem_norm_w[i])
        p_mk.append(kv[:, :MEM_WIDTH].reshape(bp, MEM_TOKENS, MEM_HEADS, MEM_HEAD_DIM))
        p_mv.append(kv[:, MEM_WIDTH:].reshape(bp, MEM_TOKENS, MEM_HEADS, MEM_HEAD_DIM))

        ssd_prm = _ssd_params(ssd_conv_w[i], ssd_conv_b[i], ssd_dt_bias[i], ssd_a_log[i], ssd_d[i], ssd_norm_w[i])
        y_ssd_p, ssd_p, conv_p = _ssd(
            proj, 0, bp, lp, jnp.zeros((bp, SSD_WIDTH, SSD_STATE), F32),
            jnp.zeros((bp, SSD_CONV - 1, SSD_CONV_DIM), F32), ssd_prm, sb=1)
        y_ssd_s, ssd_s, conv_s = _ssd(
            proj, tp, bs, ls, state_ssd[i].reshape(bs, SSD_WIDTH, SSD_STATE), state_ssd_conv[i], ssd_prm, sb=8)

        s5_prm = _s5_params(s5_lam_re[i], s5_lam_im[i], s5_b_re[i], s5_b_im[i], s5_c_re[i], s5_c_im[i], s5_d[i],
                            s5_log_dt[i], s5_glu_w[i], s5_glu_b[i], s5_norm_w[i])
        u = proj[:, PROJ_U:PROJ_DT]
        u_p = jnp.pad(jnp.swapaxes(u[:tp].reshape(bp, lp, S5_WIDTH), 0, 1), ((0, 0), (0, s5_pad), (0, 0)))
        zst = jnp.zeros((SUBLANES, S5_CHANNELS), F32)
        y5_p, s5r_p, s5i_p = _s5(u_p, zst, zst, s5_prm, tc=64, nbb=SUBLANES)
        u_s = jnp.swapaxes(u[tp:].reshape(bs, ls, S5_WIDTH), 0, 1)
        y5_s, s5r_s, s5i_s = _s5(u_s, state_s5_re[i].reshape(bs, S5_CHANNELS),
                                 state_s5_im[i].reshape(bs, S5_CHANNELS), s5_prm, tc=ls, nbb=64)
        y_s5 = jnp.concatenate([jnp.swapaxes(y5_p[:, :bp], 0, 1).reshape(tp, S5_WIDTH),
                                jnp.swapaxes(y5_s, 0, 1).reshape(ts, S5_WIDTH)], axis=0)

        rw_prm = _rwkv_params(rwkv_mu[i], rwkv_w0[i], rwkv_w2[i], rwkv_a0[i], rwkv_a2[i], rwkv_g2[i],
                              rwkv_k_k[i], rwkv_k_a[i], rwkv_r_k[i])
        hr = proj[:, PROJ_HR:PROJ_U]
        hr_p = hr[:tp].reshape(bp, lp, RWKV_SHIFT_DIM)
        hr_s = hr[tp:].reshape(bs, ls, RWKV_SHIFT_DIM)
        prev = jnp.concatenate([
            jnp.concatenate([jnp.zeros((bp, 1, RWKV_SHIFT_DIM), F32), hr_p[:, :-1]], axis=1).reshape(tp, -1),
            jnp.concatenate([state_rwkv_shift[i][:, None], hr_s[:, :-1]], axis=1).reshape(ts, -1)], axis=0)
        r_, w_, k_, v_, kk_, b_, g_, bonus = _rwkv_prep(hr, prev, rw_prm)
        yp_, wkv_p = _rwkv_chunked(r_, w_, k_, v_, kk_, b_,
                                   jnp.zeros((bp, RWKV_HEADS, RWKV_HEAD_DIM, RWKV_HEAD_DIM), F32), 0, bp, lp)
        ys_, wkv_s = _rwkv_chunked(r_, w_, k_, v_, kk_, b_, state_rwkv[i], tp, bs, ls)
        y_rw = _rwkv_post(jnp.concatenate([yp_, ys_], axis=0), bonus, g_, rwkv_gn_w[i], rwkv_gn_b[i], rw_prm[-1])

        p_st.append((ssd_p.reshape(bp, SSD_HEADS, SSD_HEAD_DIM, SSD_STATE), conv_p, wkv_p, hr_p[:, -1],
                     s5r_p[:bp].reshape(bp, S5_GROUPS, S5_STATE), s5i_p[:bp].reshape(bp, S5_GROUPS, S5_STATE)))
        s_st.append((ssd_s.reshape(bs, SSD_HEADS, SSD_HEAD_DIM, SSD_STATE), conv_s, wkv_s, hr_s[:, -1],
                     s5r_s.reshape(bs, S5_GROUPS, S5_STATE), s5i_s.reshape(bs, S5_GROUPS, S5_STATE)))

        ymix = jnp.concatenate([jnp.concatenate([y_ssd_p, y_ssd_s], axis=0), y_rw, y_s5], axis=1)
        x = _mm(ymix, w_out[i].astype(BF16), residual=x)

        q = _mm(x, wq_mem[i].astype(BF16), norm_w=norm_mem[i])
        o = jnp.concatenate([
            _attend(q, 0, bp, lp, kv, kv, 0, 1, lq=512, sb=1),
            _attend(q, tp, bs, ls, cache_mem_k[i].reshape(bs * MEM_TOKENS, MEM_WIDTH),
                    cache_mem_v[i].reshape(bs * MEM_TOKENS, MEM_WIDTH), 0, 0, lq=ls, sb=8)], axis=0)
        x = _mm(o, wo_mem[i].astype(BF16), residual=x)

        j = i // 2
        if i % 2 == 0:
            x = _ffn(x, norm_ffn[i], ffn_w1[j].astype(BF16), ffn_w3[j].astype(BF16), ffn_w2[j].astype(BF16))
        else:
            x = _moe(x, norm_ffn[i], moe_router_w[j], moe_router_b[j], moe_w1[j], moe_w3[j], moe_w2[j])

    y = _final_norm(x, final_norm_w)
    y_prompt = y[:tp].reshape(bp, lp, d)
    y_sample = y[tp:].reshape(bs, ls, d)

    def stk(lst, j):
        return jnp.stack([s[j] for s in lst])

    return (y_prompt, y_sample, jnp.stack(p_mk), jnp.stack(p_mv),
            stk(p_st, 0), stk(p_st, 1), stk(p_st, 2), stk(p_st, 3), stk(p_st, 4), stk(p_st, 5),
            stk(s_st, 0), stk(s_st, 1), stk(s_st, 2), stk(s_st, 3), stk(s_st, 4), stk(s_st, 5))
```

```python
import functools

import jax
import jax.numpy as jnp
from jax import lax
from jax.experimental import pallas as pl
from jax.experimental.pallas import tpu as pltpu

D_MODEL = 2048
DEPTH = 2
SSD_WIDTH = 1024
SSD_HEAD_DIM = 64
SSD_HEADS = 16
SSD_GROUPS = 2
SSD_STATE = 128
SSD_CONV = 4
SSD_CONV_DIM = 1536
SSD_CHUNK = 64
RWKV_WIDTH = 512
RWKV_HEAD_DIM = 64
RWKV_HEADS = 8
RWKV_W_LORA = 64
RWKV_A_LORA = 64
RWKV_G_LORA = 128
RWKV_SHIFT_DIM = 1792
RWKV_GN_EPS = 64e-5
S5_WIDTH = 512
S5_GROUP = 16
S5_GROUPS = 32
S5_STATE = 64
S5_CHANNELS = S5_GROUPS * S5_STATE
S5_SLABS = 4
S5_SLAB_CH = S5_CHANNELS // S5_SLABS
MEM_TOKENS = 256
MEM_HEADS = 4
MEM_HEAD_DIM = 128
MEM_WIDTH = 512
N_EXPERTS = 8
TOP_K = 2
RMS_EPS = 1e-6

F32 = jnp.float32
BF16 = jnp.bfloat16
HIGHEST = lax.Precision.HIGHEST

PROJ_Z = 0
PROJ_XBC = PROJ_Z + SSD_WIDTH
PROJ_HR = PROJ_XBC + SSD_CONV_DIM
PROJ_U = PROJ_HR + RWKV_SHIFT_DIM
PROJ_DT = PROJ_U + S5_WIDTH
LANES = 128
SUBLANES = 8
PROJ_WIDTH = PROJ_DT + 2 * LANES

VMEM_LIMIT = 56 * 1024 * 1024
ROW_TILE = 512
MM_ROW_TILE = 1024
MOE_ROW_TILE = 1024
MOE_SUB_TILE = 512


def _rms_rows(x, w):
    return x * lax.rsqrt(jnp.mean(x * x, axis=-1, keepdims=True) + RMS_EPS) * w


def _sigmoid(x):
    return 1.0 / (1.0 + jnp.exp(-x))


def _silu(x):
    return x * _sigmoid(x)


def _softplus(x):
    return jnp.maximum(x, 0.0) + jnp.log1p(jnp.exp(-jnp.abs(x)))


def _dot(a, b):
    return jnp.dot(a, b, preferred_element_type=F32)


def _dot_nt(a, b):
    return lax.dot_general(a, b, (((1,), (1,)), ((), ())), preferred_element_type=F32)


def _dot_exact(a, b):
    return jnp.dot(a, b, precision=HIGHEST, preferred_element_type=F32)


def _params(*sem):
    return pltpu.CompilerParams(dimension_semantics=sem, vmem_limit_bytes=VMEM_LIMIT)


def _mm_kernel(*refs, has_norm, has_res):
    refs = list(refs)
    a_ref = refs.pop(0)
    nw_ref = refs.pop(0) if has_norm else None
    w_ref = refs.pop(0)
    res_ref = refs.pop(0) if has_res else None
    o_ref = refs.pop(0)
    abf_ref = refs.pop(0)

    @pl.when(pl.program_id(1) == 0)
    def _():
        a = a_ref[...]
        if has_norm:
            a = _rms_rows(a, nw_ref[...])
        abf_ref[...] = a.astype(BF16)

    acc = _dot(abf_ref[...], w_ref[...])
    if has_res:
        acc = acc + res_ref[...]
    o_ref[...] = acc


def _col_tile(n):
    for t in (1024, 512, 256, 128):
        if n % t == 0:
            return t
    raise ValueError(f"unsupported matmul width {n}")


def _mm(a, w, norm_w=None, residual=None):
    m, k = a.shape
    n = w.shape[1]
    tm, tn = (MM_ROW_TILE if m % MM_ROW_TILE == 0 else ROW_TILE), _col_tile(n)
    assert m % tm == 0
    has_norm, has_res = norm_w is not None, residual is not None
    in_specs = [pl.BlockSpec((tm, k), lambda i, j: (i, 0))]
    args = [a]
    if has_norm:
        in_specs.append(pl.BlockSpec((1, k), lambda i, j: (0, 0)))
        args.append(norm_w.reshape(1, k))
    in_specs.append(pl.BlockSpec((k, tn), lambda i, j: (0, j)))
    args.append(w)
    if has_res:
        in_specs.append(pl.BlockSpec((tm, tn), lambda i, j: (i, j)))
        args.append(residual)
    return pl.pallas_call(
        functools.partial(_mm_kernel, has_norm=has_norm, has_res=has_res),
        out_shape=jax.ShapeDtypeStruct((m, n), F32),
        grid=(m // tm, n // tn),
        in_specs=in_specs,
        out_specs=pl.BlockSpec((tm, tn), lambda i, j: (i, j)),
        scratch_shapes=[pltpu.VMEM((tm, k), BF16)],
        compiler_params=_params("parallel", "arbitrary"),
        name="matmul",
    )(*args)


def _ffn_kernel(x_ref, nw_ref, w1_ref, w3_ref, w2_ref, o_ref, h_ref, acc_ref):
    f = pl.program_id(1)

    @pl.when(f == 0)
    def _():
        h_ref[...] = _rms_rows(x_ref[...], nw_ref[...]).astype(BF16)
        acc_ref[...] = jnp.zeros_like(acc_ref)

    h = h_ref[...]
    g = _dot(h, w1_ref[...])
    u = _dot(h, w3_ref[...])
    a = (g * jax.nn.sigmoid(g) * u).astype(BF16)
    acc_ref[...] += _dot(a, w2_ref[...])

    @pl.when(f == pl.num_programs(1) - 1)
    def _():
        o_ref[...] = x_ref[...] + acc_ref[...]


def _ffn(x, norm_w, w1, w3, w2):
    m, d = x.shape
    dff = w1.shape[1]
    tm, tf = ROW_TILE, 512
    return pl.pallas_call(
        _ffn_kernel,
        out_shape=jax.ShapeDtypeStruct((m, d), F32),
        grid=(m // tm, dff // tf),
        in_specs=[
            pl.BlockSpec((tm, d), lambda i, f: (i, 0)),
            pl.BlockSpec((1, d), lambda i, f: (0, 0)),
            pl.BlockSpec((d, tf), lambda i, f: (0, f)),
            pl.BlockSpec((d, tf), lambda i, f: (0, f)),
            pl.BlockSpec((tf, d), lambda i, f: (f, 0)),
        ],
        out_specs=pl.BlockSpec((tm, d), lambda i, f: (i, 0)),
        scratch_shapes=[pltpu.VMEM((tm, d), BF16), pltpu.VMEM((tm, d), F32)],
        compiler_params=_params("parallel", "arbitrary"),
        name="ffn_swiglu",
    )(x, norm_w.reshape(1, d), w1, w3, w2)


def _row_copy(src_hbm, src_row, dst_vmem, dst_row, sem):
    return pltpu.make_async_copy(src_hbm.at[pl.ds(src_row, 1), :], dst_vmem.at[pl.ds(dst_row, 1), :], sem)


def _moe_kernel(be_ref, nr_ref, tok_ref, h_hbm, w1_ref, w3_ref, w2_ref, o_ref, xf_ref, xb_ref, sem):
    b, f = pl.program_id(0), pl.program_id(1)
    n_sub = MOE_ROW_TILE // MOE_SUB_TILE

    @pl.when(f == 0)
    def _():
        o_ref[...] = jnp.zeros_like(o_ref)
        for sub in range(n_sub):
            @pl.when(nr_ref[b] > sub * MOE_SUB_TILE)
            def _():
                r0 = sub * MOE_SUB_TILE

                def start(r, carry):
                    _row_copy(h_hbm, tok_ref[b * MOE_ROW_TILE + r0 + r], xf_ref, r0 + r, sem).start()
                    return carry

                def wait(r, carry):
                    _row_copy(h_hbm, 0, xf_ref, r0 + r, sem).wait()
                    return carry

                lax.fori_loop(0, MOE_SUB_TILE, start, 0)
                lax.fori_loop(0, MOE_SUB_TILE, wait, 0)
                xb_ref[r0:r0 + MOE_SUB_TILE, :] = xf_ref[r0:r0 + MOE_SUB_TILE, :].astype(BF16)

    for sub in range(n_sub):
        rows = slice(sub * MOE_SUB_TILE, (sub + 1) * MOE_SUB_TILE)

        @pl.when(nr_ref[b] > sub * MOE_SUB_TILE)
        def _():
            h = xb_ref[rows, :]
            g = _dot(h, w1_ref[0].astype(BF16))
            u = _dot(h, w3_ref[0].astype(BF16))
            a = (g * jax.nn.sigmoid(g) * u).astype(BF16)
            o_ref[rows, :] += _dot(a, w2_ref[0].astype(BF16))


def _moe_experts(h, buf_tok, block_e, block_rows, w1, w3, w2):
    d = h.shape[1]
    cap = buf_tok.shape[0]
    dff = w1.shape[2]
    tm, tf = MOE_ROW_TILE, 256
    nb, nf = cap // tm, dff // tf

    def w13_map(b, f, be, nr, tok):
        return (be[b], 0, jnp.where(nr[b] > 0, f, nf - 1))

    def w2_map(b, f, be, nr, tok):
        return (be[b], jnp.where(nr[b] > 0, f, nf - 1), 0)

    grid_spec = pltpu.PrefetchScalarGridSpec(
        num_scalar_prefetch=3,
        grid=(nb, nf),
        in_specs=[
            pl.BlockSpec(memory_space=pl.ANY),
            pl.BlockSpec((1, d, tf), w13_map),
            pl.BlockSpec((1, d, tf), w13_map),
            pl.BlockSpec((1, tf, d), w2_map),
        ],
        out_specs=pl.BlockSpec((tm, d), lambda b, f, be, nr, tok: (b, 0)),
        scratch_shapes=[pltpu.VMEM((tm, d), F32), pltpu.VMEM((tm, d), BF16), pltpu.SemaphoreType.DMA(())],
    )
    return pl.pallas_call(
        _moe_kernel,
        out_shape=jax.ShapeDtypeStruct((cap, d), F32),
        grid_spec=grid_spec,
        compiler_params=_params("arbitrary", "arbitrary"),
        name="moe_swiglu",
    )(block_e, block_rows, buf_tok, h, w1, w3, w2)


def _combine_kernel(pos_ref, x_ref, g_ref, yb_hbm, *rest, tq, final_norm):
    if final_norm:
        fw_ref, o_ref, ybuf, sem = rest
    else:
        o_ref, ybuf, sem = rest
    i = pl.program_id(0)

    def start(r, carry):
        for k in range(TOP_K):
            _row_copy(yb_hbm, pos_ref[(i * tq + r) * TOP_K + k], ybuf.at[k], r, sem).start()
        return carry

    def wait(r, carry):
        for k in range(TOP_K):
            _row_copy(yb_hbm, 0, ybuf.at[k], r, sem).wait()
        return carry

    lax.fori_loop(0, tq, start, 0)
    lax.fori_loop(0, tq, wait, 0)
    g = g_ref[...]
    y = x_ref[...] + g[:, 0:1] * ybuf[0] + g[:, 1:2] * ybuf[1]
    if final_norm:
        y = _rms_rows(y, fw_ref[...])
    o_ref[...] = y


def _moe_combine(x, yb, pos, gates, final_w=None):
    t, d = x.shape
    tq = 256
    g = jnp.pad(gates, ((0, 0), (0, LANES - TOP_K)))
    in_specs = [pl.BlockSpec((tq, d), lambda i, pos: (i, 0)), pl.BlockSpec((tq, LANES), lambda i, pos: (i, 0)),
                pl.BlockSpec(memory_space=pl.ANY)]
    args = [pos.reshape(-1), x, g, yb]
    if final_w is not None:
        in_specs.append(pl.BlockSpec((1, d), lambda i, pos: (0, 0)))
        args.append(final_w.reshape(1, d))
    grid_spec = pltpu.PrefetchScalarGridSpec(
        num_scalar_prefetch=1,
        grid=(t // tq,),
        in_specs=in_specs,
        out_specs=pl.BlockSpec((tq, d), lambda i, pos: (i, 0)),
        scratch_shapes=[pltpu.VMEM((TOP_K, tq, d), F32), pltpu.SemaphoreType.DMA(())],
    )
    return pl.pallas_call(
        functools.partial(_combine_kernel, tq=tq, final_norm=final_w is not None),
        out_shape=jax.ShapeDtypeStruct((t, d), F32),
        grid_spec=grid_spec,
        compiler_params=_params("arbitrary"),
        name="moe_combine",
    )(*args)


def _route_kernel(x_ref, nw_ref, wr_ref, br_ref, h_ref, lg_ref):
    h = _rms_rows(x_ref[...], nw_ref[...])
    h_ref[...] = h
    hh = h.astype(BF16)
    hl = (h - hh.astype(F32)).astype(BF16)
    wr = wr_ref[...]
    wh = wr.astype(BF16)
    wl = (wr - wh.astype(F32)).astype(BF16)
    lg_ref[...] = _dot(hh, wh) + _dot(hl, wh) + _dot(hh, wl) + br_ref[...]


def _norm_route(x, norm_w, w_router, b_router):
    t, d = x.shape
    tm = ROW_TILE
    wr = jnp.pad(w_router, ((0, 0), (0, LANES - N_EXPERTS)))
    br = jnp.pad(b_router, (0, LANES - N_EXPERTS)).reshape(1, LANES)
    h, lg = pl.pallas_call(
        _route_kernel,
        out_shape=(jax.ShapeDtypeStruct((t, d), F32), jax.ShapeDtypeStruct((t, LANES), F32)),
        grid=(t // tm,),
        in_specs=[pl.BlockSpec((tm, d), lambda i: (i, 0)), pl.BlockSpec((1, d), lambda i: (0, 0)),
                  pl.BlockSpec((d, LANES), lambda i: (0, 0)), pl.BlockSpec((1, LANES), lambda i: (0, 0))],
        out_specs=(pl.BlockSpec((tm, d), lambda i: (i, 0)), pl.BlockSpec((tm, LANES), lambda i: (i, 0))),
        compiler_params=_params("parallel"),
        name="moe_route",
    )(x, norm_w.reshape(1, d), wr, br)
    return h, lg[:, :N_EXPERTS]


def _final_norm_kernel(x_ref, w_ref, o_ref):
    o_ref[...] = _rms_rows(x_ref[...], w_ref[...])


def _final_norm(x, w):
    t, d = x.shape
    tm = ROW_TILE
    return pl.pallas_call(
        _final_norm_kernel,
        out_shape=jax.ShapeDtypeStruct((t, d), F32),
        grid=(t // tm,),
        in_specs=[pl.BlockSpec((tm, d), lambda i: (i, 0)), pl.BlockSpec((1, d), lambda i: (0, 0))],
        out_specs=pl.BlockSpec((tm, d), lambda i: (i, 0)),
        compiler_params=_params("parallel"),
        name="final_norm",
    )(x, w.reshape(1, d))


def _moe(x, norm_w, w_router, b_router, w1, w3, w2, final_w=None):
    t, d = x.shape
    tm = MOE_ROW_TILE
    h, logits = _norm_route(x, norm_w, w_router, b_router)
    lj, le = logits[:, None, :], logits[:, :, None]
    eidx = jnp.arange(N_EXPERTS, dtype=jnp.int32)
    beats = (lj > le) | ((lj == le) & (eidx[None, None, :] < eidx[None, :, None]))
    rank = jnp.sum(beats.astype(jnp.int32), axis=-1)
    sel = jnp.stack([rank == k for k in range(TOP_K)], axis=1)
    top_v = jnp.sum(jnp.where(sel, logits[:, None, :], 0.0), axis=-1)
    gates = jax.nn.softmax(top_v, axis=-1)
    n_slots = t * TOP_K
    oh = sel.reshape(n_slots, N_EXPERTS).astype(jnp.int32)
    counts = jnp.sum(oh, axis=0)
    padded = (counts + tm - 1) // tm * tm
    pad_ends = jnp.cumsum(padded)
    pad_starts = pad_ends - padded
    within = jnp.cumsum(oh, axis=0) - oh
    slot_pos = jnp.sum(oh * (within + pad_starts[None, :]), axis=-1)
    nb = n_slots // tm + N_EXPERTS
    cap = nb * tm
    buf_tok = jnp.zeros((cap,), jnp.int32).at[slot_pos].set(jnp.arange(n_slots, dtype=jnp.int32) // TOP_K)
    block_start = jnp.arange(nb, dtype=jnp.int32) * tm
    block_e = jnp.minimum(jnp.sum((pad_ends[None, :] <= block_start[:, None]).astype(jnp.int32), axis=-1),
                          N_EXPERTS - 1)
    block_rows = jnp.clip(counts[block_e] - (block_start - pad_starts[block_e]), 0, tm).astype(jnp.int32)
    yb = _moe_experts(h, buf_tok, block_e.astype(jnp.int32), block_rows, w1, w3, w2)
    return _moe_combine(x, yb, slot_pos.astype(jnp.int32), gates, final_w)


def _s5_kernel(u_ref, sre_ref, sim_ref, are_ref, aim_ref, bb_ref, cre_ref, cim_ref, d_ref,
               gw_ref, gb_ref, nw_ref, y_ref, ore_ref, oim_ref, xr_ref, xi_ref, st_ref, *, tc, nbb):
    c = pl.program_id(1)
    rows = tc * nbb
    u = u_ref[...].reshape(rows, S5_WIDTH)
    ub = u.astype(BF16)
    for sl in range(S5_SLABS):
        bu = _dot(ub[:, sl * LANES:(sl + 1) * LANES], bb_ref[sl])
        xr_ref[:, sl * S5_SLAB_CH:(sl + 1) * S5_SLAB_CH] = bu[:, :S5_SLAB_CH]
        xi_ref[:, sl * S5_SLAB_CH:(sl + 1) * S5_SLAB_CH] = bu[:, S5_SLAB_CH:]

    @pl.when(c == 0)
    def _():
        st_ref[0] = sre_ref[...]
        st_ref[1] = sim_ref[...]

    ar = are_ref[...]
    ai = aim_ref[...]

    def step(t, carry):
        for g in range(nbb // SUBLANES):
            r0 = pl.multiple_of(t * nbb + g * SUBLANES, SUBLANES)
            sl = slice(g * SUBLANES, (g + 1) * SUBLANES)
            pr = st_ref[0, sl, :]
            pi = st_ref[1, sl, :]
            nr = ar * pr - ai * pi + xr_ref[pl.ds(r0, SUBLANES), :]
            ni = ar * pi + ai * pr + xi_ref[pl.ds(r0, SUBLANES), :]
            xr_ref[pl.ds(r0, SUBLANES), :] = nr
            xi_ref[pl.ds(r0, SUBLANES), :] = ni
            st_ref[0, sl, :] = nr
            st_ref[1, sl, :] = ni
        return carry

    lax.fori_loop(0, tc, step, 0)

    ys = []
    for sl in range(S5_SLABS):
        ch = slice(sl * S5_SLAB_CH, (sl + 1) * S5_SLAB_CH)
        ys.append(_dot(xr_ref[:, ch].astype(BF16), cre_ref[sl]) - _dot(xi_ref[:, ch].astype(BF16), cim_ref[sl]))
    y = jnp.concatenate(ys, axis=1) + d_ref[...] * u
    gy = 0.5 * y * (1.0 + jnp.tanh(0.7978845608028654 * (y + 0.044715 * (y * y * y))))
    y = gy * _sigmoid(_dot(gy.astype(BF16), gw_ref[...]) + gb_ref[...])
    y_ref[...] = _rms_rows(y, nw_ref[...]).reshape(tc, nbb, S5_WIDTH)

    @pl.when(c == pl.num_programs(1) - 1)
    def _():
        ore_ref[...] = st_ref[0]
        oim_ref[...] = st_ref[1]


def _s5(u_tm, st_re, st_im, prm, *, tc, nbb):
    L, n, _ = u_tm.shape
    ch = S5_CHANNELS
    vec = lambda w: pl.BlockSpec((1, w), lambda s, c: (0, 0))
    mat = lambda a, b: pl.BlockSpec((a, b), lambda s, c: (0, 0))
    slab = lambda a, b: pl.BlockSpec((S5_SLABS, a, b), lambda s, c: (0, 0, 0))
    st_spec = pl.BlockSpec((nbb, ch), lambda s, c: (s, 0))
    return pl.pallas_call(
        functools.partial(_s5_kernel, tc=tc, nbb=nbb),
        out_shape=(jax.ShapeDtypeStruct((L, n, S5_WIDTH), F32),
                   jax.ShapeDtypeStruct((n, ch), F32), jax.ShapeDtypeStruct((n, ch), F32)),
        grid=(n // nbb, L // tc),
        in_specs=[pl.BlockSpec((tc, nbb, S5_WIDTH), lambda s, c: (c, s, 0)), st_spec, st_spec,
                  vec(ch), vec(ch), slab(LANES, 2 * S5_SLAB_CH), slab(S5_SLAB_CH, LANES), slab(S5_SLAB_CH, LANES),
                  vec(S5_WIDTH), mat(S5_WIDTH, S5_WIDTH), vec(S5_WIDTH), vec(S5_WIDTH)],
        out_specs=(pl.BlockSpec((tc, nbb, S5_WIDTH), lambda s, c: (c, s, 0)), st_spec, st_spec),
        scratch_shapes=[pltpu.VMEM((tc * nbb, ch), F32), pltpu.VMEM((tc * nbb, ch), F32),
                        pltpu.VMEM((2, nbb, ch), F32)],
        compiler_params=_params("parallel", "arbitrary"),
        name="s5_mixer",
    )(u_tm, st_re, st_im, *prm)


def _s5_params(lam_re, lam_im, b_re, b_im, c_re, c_im, d_skip, log_dt, glu_w, glu_b, norm_w):
    delta = jnp.exp(log_dt)[:, None]
    mag = jnp.exp(lam_re * delta)
    ab_re, ab_im = mag * jnp.cos(lam_im * delta), mag * jnp.sin(lam_im * delta)
    den = lam_re * lam_re + lam_im * lam_im
    q_re = ((ab_re - 1.0) * lam_re + ab_im * lam_im) / den
    q_im = (ab_im * lam_re - (ab_re - 1.0) * lam_im) / den
    bb_re = q_re[..., None] * b_re - q_im[..., None] * b_im
    bb_im = q_re[..., None] * b_im + q_im[..., None] * b_re
    gs = S5_GROUPS // S5_SLABS
    eye = jnp.eye(gs, dtype=F32)

    def in_blockdiag(bb):
        t = jnp.swapaxes(bb, 1, 2).reshape(S5_SLABS, gs, S5_GROUP, S5_STATE)
        return (eye[None, :, None, :, None] * t[:, :, :, None, :]).reshape(S5_SLABS, LANES, S5_SLAB_CH)

    def out_blockdiag(cc):
        t = jnp.swapaxes(cc, 1, 2).reshape(S5_SLABS, gs, S5_STATE, S5_GROUP)
        return (eye[None, :, None, :, None] * t[:, :, :, None, :]).reshape(S5_SLABS, S5_SLAB_CH, LANES).astype(BF16)

    bb = jnp.concatenate([in_blockdiag(bb_re), in_blockdiag(bb_im)], axis=2).astype(BF16)
    return (ab_re.reshape(1, S5_CHANNELS), ab_im.reshape(1, S5_CHANNELS), bb,
            out_blockdiag(c_re), out_blockdiag(c_im), d_skip.reshape(1, S5_WIDTH), glu_w.astype(BF16),
            glu_b.reshape(1, S5_WIDTH), norm_w.reshape(1, S5_WIDTH))


def _ssd_kernel(p_ref, st_ref, cs_ref, cw_ref, cb_ref, dtb_ref, a_ref, dsk_ref, nw_ref,
                y_ref, ost_ref, ocs_ref, ext_ref, win_ref, s_ref, *, q, sb):
    c = pl.program_id(1)
    last = c == pl.num_programs(1) - 1
    hd, nh, gw = SSD_HEAD_DIM, SSD_HEADS, SSD_WIDTH // SSD_GROUPS
    pad_rows = hd - q

    lane = lax.broadcasted_iota(jnp.int32, (q, LANES), 1)
    row = lax.broadcasted_iota(jnp.int32, (q, LANES), 0)
    causal2 = row >= (lane % hd)
    lane64 = lax.broadcasted_iota(jnp.int32, (hd, LANES), 1)
    tri = (lax.broadcasted_iota(jnp.int32, (q, q), 0) >= lax.broadcasted_iota(jnp.int32, (q, q), 1)).astype(F32)
    e_h = lax.broadcasted_iota(jnp.int32, (LANES, SSD_WIDTH), 0)
    e_c = lax.broadcasted_iota(jnp.int32, (LANES, SSD_WIDTH), 1)
    expand = (e_h == e_c // hd).astype(F32)
    i_s = lax.broadcasted_iota(jnp.int32, (q, SSD_WIDTH), 0)
    i_c = lax.broadcasted_iota(jnp.int32, (q, SSD_WIDTH), 1)
    eye_x = (i_s == i_c % hd).astype(F32)

    for s in range(sb):
        rs = slice(s * q, (s + 1) * q)

        @pl.when(c == 0)
        def _():
            win_ref[s, 0:5, :] = jnp.zeros((5, SSD_CONV_DIM), F32)
            win_ref[s, 5:8, :] = cs_ref[s]
            for g in range(SSD_GROUPS):
                for k in range(gw // LANES):
                    r0 = g * gw + k * LANES
                    s_ref[s, g, :, k * LANES:(k + 1) * LANES] = st_ref[s, r0:r0 + LANES, :].T

        z = p_ref[rs, PROJ_Z:PROJ_XBC]
        xbc = p_ref[rs, PROJ_XBC:PROJ_HR]
        dt = p_ref[rs, PROJ_DT:PROJ_DT + LANES]
        ext_ref[0:8, :] = win_ref[s]
        ext_ref[8:8 + q, :] = xbc
        conv = cb_ref[...]
        for j in range(SSD_CONV):
            conv = conv + cw_ref[j:j + 1, :] * ext_ref[pl.ds(5 + j, q), :]
        win_ref[s] = ext_ref[q:q + 8, :]
        xc = _silu(conv)
        xs = xc[:, :SSD_WIDTH]
        bm = xc[:, SSD_WIDTH:SSD_WIDTH + SSD_GROUPS * SSD_STATE]
        cm = xc[:, SSD_WIDTH + SSD_GROUPS * SSD_STATE:]

        step = _softplus(dt + dtb_ref[...])
        adt = step * a_ref[...]
        step_x = _dot_exact(step, expand)
        acs_x = _dot_exact(tri, _dot_exact(adt, expand))
        diag = jnp.sum(acs_x * eye_x, axis=0, keepdims=True)
        acs_last = acs_x[q - 1:q, :]
        xdt = xs * step_x
        exp_acs = jnp.exp(acs_x)
        xw = xdt * jnp.exp(acs_last - acs_x)
        dec = jnp.exp(acs_last)

        for g in range(SSD_GROUPS):
            bg = bm[:, g * SSD_STATE:(g + 1) * SSD_STATE]
            cg = cm[:, g * SSD_STATE:(g + 1) * SSD_STATE].astype(BF16)
            gl = slice(g * gw, (g + 1) * gw)
            b64 = bg if pad_rows == 0 else jnp.concatenate([bg, jnp.zeros((pad_rows, SSD_STATE), F32)], axis=0)
            cb2 = _dot_nt(cg, jnp.concatenate([b64, b64], axis=0).astype(BF16))
            sg = s_ref[s, g]
            yoff = _dot(cg, sg.astype(BF16)) * exp_acs[:, gl]
            for pr in range(gw // LANES):
                l0 = g * gw + pr * LANES
                seg = acs_x[:, l0:l0 + LANES] - diag[:, l0:l0 + LANES]
                m = cb2 * jnp.exp(jnp.where(causal2, seg, -jnp.inf))
                xd = xdt[:, l0:l0 + LANES]
                xd64 = xd if pad_rows == 0 else jnp.concatenate([xd, jnp.zeros((pad_rows, LANES), F32)], axis=0)
                rhs = jnp.concatenate([jnp.where(lane64 < hd, xd64, 0.0), jnp.where(lane64 >= hd, xd64, 0.0)], axis=0)
                ydiag = _dot(m.astype(BF16), rhs.astype(BF16))
                y_ref[rs, l0:l0 + LANES] = ydiag + yoff[:, pr * LANES:(pr + 1) * LANES]
            bpad = jnp.concatenate([bg, jnp.zeros((LANES - q, SSD_STATE), F32)], axis=0)
            xwpad = jnp.concatenate([xw[:, gl], jnp.zeros((LANES - q, gw), F32)], axis=0)
            s_ref[s, g] = dec[:, gl] * sg + _dot(bpad.T.astype(BF16), xwpad.astype(BF16))

        y = y_ref[rs, :] + dsk_ref[...] * xs
        y = y * _silu(z)
        halves = []
        for g in range(SSD_GROUPS):
            yg = y[:, g * gw:(g + 1) * gw]
            halves.append(yg * lax.rsqrt(jnp.mean(yg * yg, axis=-1, keepdims=True) + RMS_EPS))
        y_ref[rs, :] = jnp.concatenate(halves, axis=1) * nw_ref[...]

        @pl.when(last)
        def _():
            ocs_ref[s] = ext_ref[q + 5:q + 8, :]
            for g in range(SSD_GROUPS):
                for k in range(gw // LANES):
                    r0 = g * gw + k * LANES
                    ost_ref[s, r0:r0 + LANES, :] = s_ref[s, g, :, k * LANES:(k + 1) * LANES].T


def _ssd(proj, row0, nseq, L, st, cs, prm, *, sb):
    q = min(L, SSD_CHUNK)
    nchunk = L // q
    rows = sb * q
    base = row0 // rows
    assert row0 % rows == 0 and nseq % sb == 0
    vec = lambda w: pl.BlockSpec((1, w), lambda s, c: (0, 0))
    st_spec = pl.BlockSpec((sb, SSD_WIDTH, SSD_STATE), lambda s, c: (s, 0, 0))
    cs_spec = pl.BlockSpec((sb, SSD_CONV - 1, SSD_CONV_DIM), lambda s, c: (s, 0, 0))
    return pl.pallas_call(
        functools.partial(_ssd_kernel, q=q, sb=sb),
        out_shape=(jax.ShapeDtypeStruct((nseq * L, SSD_WIDTH), F32),
                   jax.ShapeDtypeStruct((nseq, SSD_WIDTH, SSD_STATE), F32),
                   jax.ShapeDtypeStruct((nseq, SSD_CONV - 1, SSD_CONV_DIM), F32)),
        grid=(nseq // sb, nchunk),
        in_specs=[pl.BlockSpec((rows, PROJ_WIDTH), lambda s, c: (base + s * nchunk + c, 0)), st_spec, cs_spec,
                  pl.BlockSpec((SSD_CONV, SSD_CONV_DIM), lambda s, c: (0, 0)), vec(SSD_CONV_DIM),
                  vec(LANES), vec(LANES), vec(SSD_WIDTH), vec(SSD_WIDTH)],
        out_specs=(pl.BlockSpec((rows, SSD_WIDTH), lambda s, c: (s * nchunk + c, 0)), st_spec, cs_spec),
        scratch_shapes=[pltpu.VMEM((q + 8, SSD_CONV_DIM), F32), pltpu.VMEM((sb, 8, SSD_CONV_DIM), F32),
                        pltpu.VMEM((sb, SSD_GROUPS, SSD_STATE, SSD_WIDTH // SSD_GROUPS), F32)],
        compiler_params=_params("parallel", "arbitrary"),
        name="ssd_mixer",
    )(proj, st, cs, *prm)


def _ssd_params(conv_w, conv_b, dt_bias, a_log, d_skip, norm_w):
    pad = jnp.zeros((LANES - SSD_HEADS,), F32)
    return (conv_w, conv_b.reshape(1, SSD_CONV_DIM), jnp.concatenate([dt_bias, pad]).reshape(1, LANES),
            jnp.concatenate([-jnp.exp(a_log), pad]).reshape(1, LANES),
            jnp.repeat(d_skip, SSD_HEAD_DIM).reshape(1, SSD_WIDTH), norm_w.reshape(1, SSD_WIDTH))


def _attn_kernel(q_ref, k_ref, v_ref, o_ref, *, lq, sb):
    scale = MEM_HEAD_DIM ** -0.5
    heads = [slice(h * MEM_HEAD_DIM, (h + 1) * MEM_HEAD_DIM) for h in range(MEM_HEADS)]
    for s in range(sb):
        rq = slice(s * lq, (s + 1) * lq)
        rk = slice(s * MEM_TOKENS, (s + 1) * MEM_TOKENS)
        sc = [_dot_nt(q_ref[rq, cl].astype(BF16), k_ref[rk, cl].astype(BF16)) * scale for cl in heads]
        sc = [x - jnp.max(x, axis=-1, keepdims=True) for x in sc]
        p = [jnp.exp(x) for x in sc]
        p = [x / jnp.sum(x, axis=-1, keepdims=True) for x in p]
        for cl, x in zip(heads, p):
            o_ref[rq, cl] = _dot(x.astype(BF16), v_ref[rk, cl].astype(BF16))


def _attend(q, row0, nseq, L, k2d, v2d, kcol, vcol, *, lq, sb):
    nl = L // lq
    rows = sb * lq
    base = row0 // rows
    assert row0 % rows == 0 and (sb == 1 or nl == 1)
    return pl.pallas_call(
        functools.partial(_attn_kernel, lq=lq, sb=sb),
        out_shape=jax.ShapeDtypeStruct((nseq * L, MEM_WIDTH), F32),
        grid=(nseq // sb, nl),
        in_specs=[pl.BlockSpec((rows, MEM_WIDTH), lambda s, l: (base + s * nl + l, 0)),
                  pl.BlockSpec((sb * MEM_TOKENS, MEM_WIDTH), lambda s, l: (s, kcol)),
                  pl.BlockSpec((sb * MEM_TOKENS, MEM_WIDTH), lambda s, l: (s, vcol))],
        out_specs=pl.BlockSpec((rows, MEM_WIDTH), lambda s, l: (s * nl + l, 0)),
        compiler_params=_params("parallel", "arbitrary"),
        name="mem_attention",
    )(q, k2d, v2d)


def _head_sum(x, ones_bd):
    return _dot_exact(x, ones_bd)


def _rwkv_prep_kernel(h_ref, p_ref, mu_ref, wl_ref, w0_ref, a0_ref, kk_ref, ka_ref, rk_ref, ones_ref,
                      r_ref, w_ref, k_ref, v_ref, n_ref, b_ref, g_ref, bo_ref):
    W = RWKV_WIDTH
    h = h_ref[...]
    hs = h + (p_ref[...] - h) * mu_ref[...]
    r, k, v = hs[:, :W], hs[:, W:2 * W], hs[:, 2 * W:3 * W]
    lo = hs[:, 3 * W:]
    lane = lax.broadcasted_iota(jnp.int32, lo.shape, 1)
    act = jnp.where(lane < RWKV_W_LORA, jnp.tanh(lo),
                    jnp.where(lane < RWKV_W_LORA + RWKV_A_LORA, lo, _sigmoid(lo)))
    lora = _dot(act.astype(BF16), wl_ref[...])
    w_log = -_softplus(-(w0_ref[...] + lora[:, :W])) - 0.5
    a = _sigmoid(a0_ref[...] + lora[:, W:2 * W])
    ones_bd = ones_ref[...]
    kk = k * kk_ref[...]
    kk = kk / jnp.maximum(jnp.sqrt(_head_sum(kk * kk, ones_bd)), 1e-12)
    k2 = k * (1.0 + (a - 1.0) * ka_ref[...])
    r_ref[...] = r
    w_ref[...] = -jnp.exp(w_log)
    k_ref[...] = k2
    v_ref[...] = v
    n_ref[...] = kk
    b_ref[...] = kk * a
    g_ref[...] = lora[:, 2 * W:]
    bo_ref[...] = _head_sum(r * k2 * rk_ref[...], ones_bd) * v


def _rwkv_prep(hr, prev, prm):
    m = hr.shape[0]
    tm, W = ROW_TILE, RWKV_WIDTH
    mu, wl, w0, a0, k_k, k_a, r_k, ones_bd = prm
    vec = lambda w: pl.BlockSpec((1, w), lambda i: (0, 0))
    row = lambda w: pl.BlockSpec((tm, w), lambda i: (i, 0))
    return pl.pallas_call(
        _rwkv_prep_kernel,
        out_shape=tuple(jax.ShapeDtypeStruct((m, W), F32) for _ in range(8)),
        grid=(m // tm,),
        in_specs=[row(RWKV_SHIFT_DIM), row(RWKV_SHIFT_DIM), vec(RWKV_SHIFT_DIM),
                  pl.BlockSpec(wl.shape, lambda i: (0, 0)), vec(W), vec(W), vec(W), vec(W), vec(W),
                  pl.BlockSpec((W, W), lambda i: (0, 0))],
        out_specs=tuple(row(W) for _ in range(8)),
        compiler_params=_params("parallel"),
        name="rwkv_prep",
    )(hr, prev, mu, wl, w0, a0, k_k, k_a, r_k, ones_bd)


def _rwkv_post_kernel(y_ref, bo_ref, g_ref, gw_ref, gb_ref, ones_ref, o_ref):
    ones_bd = ones_ref[...]
    y = y_ref[...]
    inv = 1.0 / RWKV_HEAD_DIM
    d = y - _head_sum(y, ones_bd) * inv
    var = _head_sum(d * d, ones_bd) * inv
    yn = d * lax.rsqrt(var + RWKV_GN_EPS) * gw_ref[...] + gb_ref[...]
    o_ref[...] = (yn + bo_ref[...]) * g_ref[...]


def _rwkv_post(y, bonus, g, gn_w, gn_b, ones_bd):
    m, W = y.shape
    tm = ROW_TILE
    vec = pl.BlockSpec((1, W), lambda i: (0, 0))
    row = pl.BlockSpec((tm, W), lambda i: (i, 0))
    return pl.pallas_call(
        _rwkv_post_kernel,
        out_shape=jax.ShapeDtypeStruct((m, W), F32),
        grid=(m // tm,),
        in_specs=[row, row, row, vec, vec, pl.BlockSpec((W, W), lambda i: (0, 0))],
        out_specs=row,
        compiler_params=_params("parallel"),
        name="rwkv_post",
    )(y, bonus, g, gn_w.reshape(1, W), gn_b.reshape(1, W), ones_bd)


def _rwkv_params(mu, w0, w2, a0, a2, g2, k_k, k_a, r_k):
    W = RWKV_WIDTH
    nl = RWKV_W_LORA + RWKV_A_LORA + RWKV_G_LORA
    wl = jnp.zeros((nl, 3 * W), F32)
    wl = wl.at[:RWKV_W_LORA, :W].set(w2)
    wl = wl.at[RWKV_W_LORA:RWKV_W_LORA + RWKV_A_LORA, W:2 * W].set(a2)
    wl = wl.at[RWKV_W_LORA + RWKV_A_LORA:, 2 * W:].set(g2)
    head = jnp.arange(W) // RWKV_HEAD_DIM
    ones_bd = (head[:, None] == head[None, :]).astype(F32)
    v = lambda t: t.reshape(1, -1)
    return (v(mu), wl.astype(BF16), v(w0), v(a0), v(k_k), v(k_a), v(r_k), ones_bd)


def _bdot(a, b):
    return jnp.dot(a.astype(BF16), b.astype(BF16), preferred_element_type=F32)


def _bdot_nt(a, b):
    return lax.dot_general(a.astype(BF16), b.astype(BF16), (((1,), (1,)), ((), ())), preferred_element_type=F32)


def _split(x):
    hi = x.astype(BF16)
    return hi, (x - hi.astype(F32)).astype(BF16)


def _dot3(a, b):
    ah, al = _split(a)
    bh, bl = _split(b)
    return (jnp.dot(ah, bh, preferred_element_type=F32) + jnp.dot(al, bh, preferred_element_type=F32)
            + jnp.dot(ah, bl, preferred_element_type=F32))


def _dot3_nt(a, b):
    ah, al = _split(a)
    bh, bl = _split(b)
    dn = (((1,), (1,)), ((), ()))
    return (lax.dot_general(ah, bh, dn, preferred_element_type=F32)
            + lax.dot_general(al, bh, dn, preferred_element_type=F32)
            + lax.dot_general(ah, bl, dn, preferred_element_type=F32))


RWKV_GROUP = 4
RWKV_GW = RWKV_GROUP * RWKV_HEAD_DIM
RWKV_NG = RWKV_HEADS // RWKV_GROUP


def _rwkv_chunk_kernel(r_ref, ls_ref, k_ref, v_ref, n_ref, b_ref, s0_ref, y_ref, sf_ref, s_ref, *, C, ns):
    c = pl.program_id(1)
    G, GW, HD = RWKV_GROUP, RWKV_GW, RWKV_HEAD_DIM
    RI = ns * C
    R = G * RI
    SB = G * C
    groups = range(RWKV_NG)

    @pl.when(c == 0)
    def _():
        for q in groups:
            for s in range(ns):
                s_ref[q, s] = jnp.concatenate([s0_ref[s, G * q + h] for h in range(G)], axis=1)

    ri = lax.broadcasted_iota(jnp.int32, (2 * RI, RI), 0)
    ci = lax.broadcasted_iota(jnp.int32, (2 * RI, RI), 1)
    same = ((ri % RI) // C) == (ci // C)
    cum = (same & ((ri >= RI) | (ri >= ci))).astype(BF16)
    ls_all = ls_ref[...]
    l1 = ls_all.astype(BF16)
    l2f = ls_all - l1.astype(F32)
    l2 = l2f.astype(BF16)
    l3 = (l2f - l2.astype(F32)).astype(BF16)
    lw2 = (jnp.dot(cum, l1, preferred_element_type=F32) + jnp.dot(cum, l2, preferred_element_type=F32)
           + jnp.dot(cum, l3, preferred_element_type=F32))
    lw_all, lwl_all = lw2[:RI], lw2[RI:]

    lane_in = lax.broadcasted_iota(jnp.int32, (C, GW), 1) // HD
    row = lax.broadcasted_iota(jnp.int32, (R, R), 0)
    col = lax.broadcasted_iota(jnp.int32, (R, R), 1)
    ent = (row // C) == (col // C)
    strict = ent & (row > col)
    incl = ent & (row >= col)
    eye = row == col
    eye_f = eye.astype(F32)
    own = (lax.broadcasted_iota(jnp.int32, (SB, GW), 0) // C) == (lax.broadcasted_iota(jnp.int32, (SB, GW), 1) // HD)
    rows_r = lax.broadcasted_iota(jnp.int32, (R, GW), 0)
    rows_2r = lax.broadcasted_iota(jnp.int32, (2 * R, GW), 0)

    def stack(x):
        parts = []
        for s in range(ns):
            xs = x[s * C:(s + 1) * C]
            parts += [jnp.where(lane_in == h, xs, 0.0) for h in range(G)]
        return jnp.concatenate(parts, axis=0)

    def dup(x):
        parts = []
        for s in range(ns):
            parts += [x[s * C:(s + 1) * C]] * G
        return jnp.concatenate(parts, axis=0)

    st = []
    for q in groups:
        gl = slice(q * GW, (q + 1) * GW)
        lw, lwl, ls = lw_all[:, gl], lwl_all[:, gl], ls_all[:, gl]
        w_inv = jnp.exp(-lw)
        w_rest = jnp.exp(lwl - lw)
        kk, bb = k_ref[:, gl], b_ref[:, gl]
        st.append(dict(
            n_st=stack(n_ref[:, gl] * jnp.exp(lw - ls)), r_st=stack(r_ref[:, gl] * jnp.exp(lw)),
            v_st=stack(v_ref[:, gl]), bh_st=stack(bb * w_rest), kh_st=stack(kk * w_rest),
            b_dup=dup(bb * w_inv), k_dup=dup(kk * w_inv), w_c=jnp.exp(lwl)))
    for d in st:
        nr = jnp.concatenate([d['n_st'], d['r_st']], axis=0)
        gb = _bdot_nt(nr, d['b_dup'])
        gk = _bdot_nt(nr, d['k_dup'])
        d['a_nb'] = jnp.where(strict, gb[:R], 0.0)
        d['a_rb'] = jnp.where(incl, gb[R:], 0.0)
        d['a_nk'] = jnp.where(strict, gk[:R], 0.0)
        d['a_rk'] = jnp.where(incl, gk[R:], 0.0)
        d['t'] = eye_f - d['a_nb']
        d['p'] = d['a_nb']
    for _ in range(C.bit_length() - 2):
        for d in st:
            d['p'] = _bdot(d['p'], d['p'])
        for d in st:
            d['t'] = _bdot(d['t'], eye_f + d['p'])
    for d in st:
        d['p1'] = _bdot(d['t'], d['n_st'])
        d['z'] = _bdot(d['a_nk'], d['v_st'])
    for d in st:
        d['p2'] = _bdot(d['t'], d['z'])
    for q, d in enumerate(st):
        p1, p2 = d['p1'], d['p2']
        p1_t = p1.T
        lt = jnp.concatenate([d['v_st'].T, -p2.T], axis=1)
        kb = jnp.concatenate([d['kh_st'], d['bh_st']], axis=0)
        sa_parts, rs_parts = [], []
        for s in range(ns):
            rsl = slice(s * SB, (s + 1) * SB)
            S = s_ref[q, s]
            ss = jnp.concatenate([S] * G, axis=0)
            xr = _dot3_nt(jnp.concatenate([p1[rsl], d['r_st'][rsl]], axis=0), ss)
            sa_parts.append(-jnp.where(own, xr[:SB], 0.0) - p2[rsl])
            rs_parts.append(jnp.where(own, xr[SB:], 0.0))
            if ns == 1:
                bh_s, kb_s = d['bh_st'], kb
            else:
                bh_s = jnp.where((rows_r // SB) == s, d['bh_st'], 0.0)
                kb_s = jnp.where(((rows_2r % R) // SB) == s, kb, 0.0)
            m_bd = jnp.where(eye, d['w_c'][s * C:s * C + 1, :], 0.0) - _bdot(p1_t, bh_s)
            nf = _bdot(lt, kb_s)
            fold = nf[:HD]
            for h in range(1, G):
                fold = fold + nf[h * HD:(h + 1) * HD]
            s_ref[q, s] = _dot3(S, m_bd) + fold
        sa_st = jnp.concatenate(sa_parts, axis=0) if ns > 1 else sa_parts[0]
        rs_st = jnp.concatenate(rs_parts, axis=0) if ns > 1 else rs_parts[0]
        y_st = rs_st + _bdot(jnp.concatenate([d['a_rb'], d['a_rk']], axis=1),
                             jnp.concatenate([sa_st, d['v_st']], axis=0))
        for s in range(ns):
            y = y_st[s * SB:s * SB + C]
            for h in range(1, G):
                y = y + y_st[s * SB + h * C:s * SB + (h + 1) * C]
            y_ref[s * C:(s + 1) * C, q * GW:(q + 1) * GW] = y

    @pl.when(c == pl.num_programs(1) - 1)
    def _():
        for q in groups:
            for s in range(ns):
                S = s_ref[q, s]
                for h in range(G):
                    sf_ref[s, G * q + h] = S[:, h * HD:(h + 1) * HD]


def _rwkv_chunked(r, ls, k, v, kk, b, state, row0, nseq, L):
    HD, W = RWKV_HEAD_DIM, RWKV_WIDTH
    C = min(L, HD)
    ns = HD // C
    nt = L // C
    rows = ns * C
    base = row0 // rows
    assert row0 % rows == 0 and (ns == 1 or nt == 1) and nseq % ns == 0
    row_spec = pl.BlockSpec((rows, W), lambda s, c: (base + s * nt + c, 0))
    st_spec = pl.BlockSpec((ns, RWKV_HEADS, HD, HD), lambda s, c: (s, 0, 0, 0))
    return pl.pallas_call(
        functools.partial(_rwkv_chunk_kernel, C=C, ns=ns),
        out_shape=(jax.ShapeDtypeStruct((nseq * L, W), F32), jax.ShapeDtypeStruct((nseq, RWKV_HEADS, HD, HD), F32)),
        grid=(nseq // ns, nt),
        in_specs=[row_spec] * 6 + [st_spec],
        out_specs=(pl.BlockSpec((rows, W), lambda s, c: (s * nt + c, 0)), st_spec),
        scratch_shapes=[pltpu.VMEM((RWKV_NG, ns, HD, RWKV_GW), F32)],
        compiler_params=_params("parallel", "arbitrary"),
        name="rwkv_chunked",
    )(r, ls, k, v, kk, b, state)


def kernel(x_prompt, x_sample, mem_prompt, cache_mem_k, cache_mem_v, state_ssd, state_ssd_conv, state_rwkv, state_rwkv_shift, state_s5_re, state_s5_im, norm_mix, w_in, ssd_conv_w, ssd_conv_b, ssd_dt_bias, ssd_a_log, ssd_d, ssd_norm_w, rwkv_mu, rwkv_w0, rwkv_w2, rwkv_a0, rwkv_a2, rwkv_g2, rwkv_k_k, rwkv_k_a, rwkv_r_k, rwkv_gn_w, rwkv_gn_b, s5_lam_re, s5_lam_im, s5_b_re, s5_b_im, s5_c_re, s5_c_im, s5_d, s5_log_dt, s5_glu_w, s5_glu_b, s5_norm_w, w_out, norm_mem, mem_norm_w, wq_mem, wk_mem, wv_mem, wo_mem, norm_ffn, ffn_w1, ffn_w3, ffn_w2, moe_router_w, moe_router_b, moe_w1, moe_w3, moe_w2, final_norm_w):
    bp, lp, d = x_prompt.shape
    bs, ls, _ = x_sample.shape
    tp, ts = bp * lp, bs * ls
    x = jnp.concatenate([x_prompt.reshape(tp, d), x_sample.reshape(ts, d)], axis=0)
    mem_rows = mem_prompt.reshape(bp * MEM_TOKENS, d)
    s5_pad = SUBLANES - bp

    p_mk, p_mv, p_st, s_st = [], [], [], []
    for i in range(DEPTH):
        c0 = SSD_WIDTH
        c1 = c0 + SSD_CONV_DIM
        c2 = c1 + SSD_HEADS
        wi = w_in[i]
        w_in_packed = jnp.concatenate(
            [wi[:, :c1], wi[:, c2:], wi[:, c1:c2], jnp.zeros((d, PROJ_WIDTH - PROJ_DT - SSD_HEADS), F32)],
            axis=1).astype(BF16)
        proj = _mm(x, w_in_packed, norm_w=norm_mix[i])

        wkv = jnp.concatenate([wk_mem[i], wv_mem[i]], axis=1).astype(BF16)
        kv = _mm(mem_rows, wkv, norm_w=mem_norm_w[i])
        p_mk.append(kv[:, :MEM_WIDTH].reshape(bp, MEM_TOKENS, MEM_HEADS, MEM_HEAD_DIM))
        p_mv.append(kv[:, MEM_WIDTH:].reshape(bp, MEM_TOKENS, MEM_HEADS, MEM_HEAD_DIM))

        ssd_prm = _ssd_params(ssd_conv_w[i], ssd_conv_b[i], ssd_dt_bias[i], ssd_a_log[i], ssd_d[i], ssd_norm_w[i])
        y_ssd_p, ssd_p, conv_p = _ssd(
            proj, 0, bp, lp, jnp.zeros((bp, SSD_WIDTH, SSD_STATE), F32),
            jnp.zeros((bp, SSD_CONV - 1, SSD_CONV_DIM), F32), ssd_prm, sb=1)
        y_ssd_s, ssd_s, conv_s = _ssd(
            proj, tp, bs, ls, state_ssd[i].reshape(bs, SSD_WIDTH, SSD_STATE), state_ssd_conv[i], ssd_prm, sb=8)

        s5_prm = _s5_params(s5_lam_re[i], s5_lam_im[i], s5_b_re[i], s5_b_im[i], s5_c_re[i], s5_c_im[i], s5_d[i],
                            s5_log_dt[i], s5_glu_w[i], s5_glu_b[i], s5_norm_w[i])
        u = proj[:, PROJ_U:PROJ_DT]
        u_p = jnp.pad(jnp.swapaxes(u[:tp].reshape(bp, lp, S5_WIDTH), 0, 1), ((0, 0), (0, s5_pad), (0, 0)))
        zst = jnp.zeros((SUBLANES, S5_CHANNELS), F32)
        y5_p, s5r_p, s5i_p = _s5(u_p, zst, zst, s5_prm, tc=64, nbb=SUBLANES)
        u_s = jnp.swapaxes(u[tp:].reshape(bs, ls, S5_WIDTH), 0, 1)
        y5_s, s5r_s, s5i_s = _s5(u_s, state_s5_re[i].reshape(bs, S5_CHANNELS),
                                 state_s5_im[i].reshape(bs, S5_CHANNELS), s5_prm, tc=ls, nbb=64)
        y_s5 = jnp.concatenate([jnp.swapaxes(y5_p[:, :bp], 0, 1).reshape(tp, S5_WIDTH),
                                jnp.swapaxes(y5_s, 0, 1).reshape(ts, S5_WIDTH)], axis=0)

        rw_prm = _rwkv_params(rwkv_mu[i], rwkv_w0[i], rwkv_w2[i], rwkv_a0[i], rwkv_a2[i], rwkv_g2[i],
                              rwkv_k_k[i], rwkv_k_a[i], rwkv_r_k[i])
        hr = proj[:, PROJ_HR:PROJ_U]
        hr_p = hr[:tp].reshape(bp, lp, RWKV_SHIFT_DIM)
        hr_s = hr[tp:].reshape(bs, ls, RWKV_SHIFT_DIM)
        prev = jnp.concatenate([
            jnp.concatenate([jnp.zeros((bp, 1, RWKV_SHIFT_DIM), F32), hr_p[:, :-1]], axis=1).reshape(tp, -1),
            jnp.concatenate([state_rwkv_shift[i][:, None], hr_s[:, :-1]], axis=1).reshape(ts, -1)], axis=0)
        r_, w_, k_, v_, kk_, b_, g_, bonus = _rwkv_prep(hr, prev, rw_prm)
        yp_, wkv_p = _rwkv_chunked(r_, w_, k_, v_, kk_, b_,
                                   jnp.zeros((bp, RWKV_HEADS, RWKV_HEAD_DIM, RWKV_HEAD_DIM), F32), 0, bp, lp)
        ys_, wkv_s = _rwkv_chunked(r_, w_, k_, v_, kk_, b_, state_rwkv[i], tp, bs, ls)
        y_rw = _rwkv_post(jnp.concatenate([yp_, ys_], axis=0), bonus, g_, rwkv_gn_w[i], rwkv_gn_b[i], rw_prm[-1])

        p_st.append((ssd_p.reshape(bp, SSD_HEADS, SSD_HEAD_DIM, SSD_STATE), conv_p, wkv_p, hr_p[:, -1],
                     s5r_p[:bp].reshape(bp, S5_GROUPS, S5_STATE), s5i_p[:bp].reshape(bp, S5_GROUPS, S5_STATE)))
        s_st.append((ssd_s.reshape(bs, SSD_HEADS, SSD_HEAD_DIM, SSD_STATE), conv_s, wkv_s, hr_s[:, -1],
                     s5r_s.reshape(bs, S5_GROUPS, S5_STATE), s5i_s.reshape(bs, S5_GROUPS, S5_STATE)))

        ymix = jnp.concatenate([jnp.concatenate([y_ssd_p, y_ssd_s], axis=0), y_rw, y_s5], axis=1)
        x = _mm(ymix, w_out[i].astype(BF16), residual=x)

        q = _mm(x, wq_mem[i].astype(BF16), norm_w=norm_mem[i])
        o = jnp.concatenate([
            _attend(q, 0, bp, lp, kv, kv, 0, 1, lq=512, sb=1),
            _attend(q, tp, bs, ls, cache_mem_k[i].reshape(bs * MEM_TOKENS, MEM_WIDTH),
                    cache_mem_v[i].reshape(bs * MEM_TOKENS, MEM_WIDTH), 0, 0, lq=ls, sb=8)], axis=0)
        x = _mm(o, wo_mem[i].astype(BF16), residual=x)

        j = i // 2
        if i % 2 == 0:
            x = _ffn(x, norm_ffn[i], ffn_w1[j].astype(BF16), ffn_w3[j].astype(BF16), ffn_w2[j].astype(BF16))
        else:
            x = _moe(x, norm_ffn[i], moe_router_w[j], moe_router_b[j], moe_w1[j], moe_w3[j], moe_w2[j],
                     final_w=final_norm_w if i == DEPTH - 1 else None)

    y = x if DEPTH % 2 == 0 else _final_norm(x, final_norm_w)
    y_prompt = y[:tp].reshape(bp, lp, d)
    y_sample = y[tp:].reshape(bs, ls, d)

    def stk(lst, j):
        return jnp.stack([s[j] for s in lst])

    return (y_prompt, y_sample, jnp.stack(p_mk), jnp.stack(p_mv),
            stk(p_st, 0), stk(p_st, 1), stk(p_st, 2), stk(p_st, 3), stk(p_st, 4), stk(p_st, 5),
            stk(s_st, 0), stk(s_st, 1), stk(s_st, 2), stk(s_st, 3), stk(s_st, 4), stk(s_st, 5))
```

```python
import functools

import jax
import jax.numpy as jnp
from jax import lax
from jax.experimental import pallas as pl
from jax.experimental.pallas import tpu as pltpu

D_MODEL = 2048
DEPTH = 2
SSD_WIDTH = 1024
SSD_HEAD_DIM = 64
SSD_HEADS = 16
SSD_GROUPS = 2
SSD_STATE = 128
SSD_CONV = 4
SSD_CONV_DIM = 1536
SSD_CHUNK = 64
RWKV_WIDTH = 512
RWKV_HEAD_DIM = 64
RWKV_HEADS = 8
RWKV_W_LORA = 64
RWKV_A_LORA = 64
RWKV_G_LORA = 128
RWKV_SHIFT_DIM = 1792
RWKV_GN_EPS = 64e-5
S5_WIDTH = 512
S5_GROUP = 16
S5_GROUPS = 32
S5_STATE = 64
S5_CHANNELS = S5_GROUPS * S5_STATE
S5_SLABS = 4
S5_SLAB_CH = S5_CHANNELS // S5_SLABS
MEM_TOKENS = 256
MEM_HEADS = 4
MEM_HEAD_DIM = 128
MEM_WIDTH = 512
N_EXPERTS = 8
TOP_K = 2
RMS_EPS = 1e-6

F32 = jnp.float32
BF16 = jnp.bfloat16
HIGHEST = lax.Precision.HIGHEST

PROJ_Z = 0
PROJ_XBC = PROJ_Z + SSD_WIDTH
PROJ_HR = PROJ_XBC + SSD_CONV_DIM
PROJ_U = PROJ_HR + RWKV_SHIFT_DIM
PROJ_DT = PROJ_U + S5_WIDTH
LANES = 128
SUBLANES = 8
PROJ_WIDTH = PROJ_DT + 2 * LANES

VMEM_LIMIT = 56 * 1024 * 1024
ROW_TILE = 512
MM_ROW_TILE = 1024
MOE_ROW_TILE = 1024
MOE_SUB_TILE = 512


def _rms_rows(x, w):
    return x * lax.rsqrt(jnp.mean(x * x, axis=-1, keepdims=True) + RMS_EPS) * w


def _sigmoid(x):
    return 1.0 / (1.0 + jnp.exp(-x))


def _silu(x):
    return x * _sigmoid(x)


def _softplus(x):
    return jnp.maximum(x, 0.0) + jnp.log1p(jnp.exp(-jnp.abs(x)))


def _dot(a, b):
    return jnp.dot(a, b, preferred_element_type=F32)


def _dot_nt(a, b):
    return lax.dot_general(a, b, (((1,), (1,)), ((), ())), preferred_element_type=F32)


def _dot_exact(a, b):
    return jnp.dot(a, b, precision=HIGHEST, preferred_element_type=F32)


def _params(*sem):
    return pltpu.CompilerParams(dimension_semantics=sem, vmem_limit_bytes=VMEM_LIMIT)


def _split_rows_specs(pair, tm, width, grid_rank):
    n_first = pair[0].shape[0] // tm
    assert pair[0].shape[0] % tm == 0 and pair[1].shape[0] % tm == 0
    if grid_rank == 1:
        first = pl.BlockSpec((tm, width), lambda i: (jnp.minimum(i, n_first - 1), 0))
        rest = pl.BlockSpec((tm, width), lambda i: (jnp.maximum(i - n_first, 0), 0))
    else:
        first = pl.BlockSpec((tm, width), lambda i, j: (jnp.minimum(i, n_first - 1), 0))
        rest = pl.BlockSpec((tm, width), lambda i, j: (jnp.maximum(i - n_first, 0), 0))
    return n_first, [first, rest]


def _mm_kernel(*refs, parts, has_norm, has_res):
    refs = list(refs)
    a_refs = []
    for width, n_first in parts:
        a_refs.append(refs.pop(0) if n_first is None else (refs.pop(0), refs.pop(0)))
    nw_ref = refs.pop(0) if has_norm else None
    w_ref = refs.pop(0)
    res_ref = refs.pop(0) if has_res else None
    o_ref = refs.pop(0)
    abf_ref = refs.pop(0)
    i = pl.program_id(0)

    @pl.when(pl.program_id(1) == 0)
    def _():
        c0 = 0
        for (width, n_first), a_ref in zip(parts, a_refs):
            a = a_ref[...] if n_first is None else jnp.where(i < n_first, a_ref[0][...], a_ref[1][...])
            if has_norm:
                a = _rms_rows(a, nw_ref[...])
            abf_ref[:, c0:c0 + width] = a.astype(BF16)
            c0 += width

    acc = _dot(abf_ref[...], w_ref[...])
    if has_res:
        acc = acc + res_ref[...]
    o_ref[...] = acc


def _col_tile(n):
    for t in (1024, 512, 256, 128):
        if n % t == 0:
            return t
    raise ValueError(f"unsupported matmul width {n}")


def _mm(a, w, norm_w=None, residual=None):
    a_parts = a if isinstance(a, list) else [a]
    rows = lambda p: p.shape[0] if not isinstance(p, tuple) else p[0].shape[0] + p[1].shape[0]
    m = rows(a_parts[0])
    k = sum((p[0] if isinstance(p, tuple) else p).shape[1] for p in a_parts)
    n = w.shape[1]
    tm, tn = (MM_ROW_TILE if m % MM_ROW_TILE == 0 else ROW_TILE), _col_tile(n)
    assert m % tm == 0 and w.shape[0] == k
    has_norm, has_res = norm_w is not None, residual is not None
    assert not (has_norm and len(a_parts) > 1)
    in_specs, args, parts = [], [], []
    for p in a_parts:
        if isinstance(p, tuple):
            n_first, specs = _split_rows_specs(p, tm, p[0].shape[1], 2)
            in_specs += specs
            args += list(p)
            parts.append((p[0].shape[1], n_first))
        else:
            in_specs.append(pl.BlockSpec((tm, p.shape[1]), lambda i, j: (i, 0)))
            args.append(p)
            parts.append((p.shape[1], None))
    if has_norm:
        in_specs.append(pl.BlockSpec((1, k), lambda i, j: (0, 0)))
        args.append(norm_w.reshape(1, k))
    in_specs.append(pl.BlockSpec((k, tn), lambda i, j: (0, j)))
    args.append(w)
    if has_res:
        in_specs.append(pl.BlockSpec((tm, tn), lambda i, j: (i, j)))
        args.append(residual)
    return pl.pallas_call(
        functools.partial(_mm_kernel, parts=tuple(parts), has_norm=has_norm, has_res=has_res),
        out_shape=jax.ShapeDtypeStruct((m, n), F32),
        grid=(m // tm, n // tn),
        in_specs=in_specs,
        out_specs=pl.BlockSpec((tm, tn), lambda i, j: (i, j)),
        scratch_shapes=[pltpu.VMEM((tm, k), BF16)],
        compiler_params=_params("parallel", "arbitrary"),
        name="matmul",
    )(*args)


def _ffn_kernel(x_ref, nw_ref, w1_ref, w3_ref, w2_ref, o_ref, h_ref, acc_ref):
    f = pl.program_id(1)

    @pl.when(f == 0)
    def _():
        h_ref[...] = _rms_rows(x_ref[...], nw_ref[...]).astype(BF16)
        acc_ref[...] = jnp.zeros_like(acc_ref)

    h = h_ref[...]
    g = _dot(h, w1_ref[...])
    u = _dot(h, w3_ref[...])
    a = (g * jax.nn.sigmoid(g) * u).astype(BF16)
    acc_ref[...] += _dot(a, w2_ref[...])

    @pl.when(f == pl.num_programs(1) - 1)
    def _():
        o_ref[...] = x_ref[...] + acc_ref[...]


def _ffn(x, norm_w, w1, w3, w2):
    m, d = x.shape
    dff = w1.shape[1]
    tm, tf = ROW_TILE, 512
    return pl.pallas_call(
        _ffn_kernel,
        out_shape=jax.ShapeDtypeStruct((m, d), F32),
        grid=(m // tm, dff // tf),
        in_specs=[
            pl.BlockSpec((tm, d), lambda i, f: (i, 0)),
            pl.BlockSpec((1, d), lambda i, f: (0, 0)),
            pl.BlockSpec((d, tf), lambda i, f: (0, f)),
            pl.BlockSpec((d, tf), lambda i, f: (0, f)),
            pl.BlockSpec((tf, d), lambda i, f: (f, 0)),
        ],
        out_specs=pl.BlockSpec((tm, d), lambda i, f: (i, 0)),
        scratch_shapes=[pltpu.VMEM((tm, d), BF16), pltpu.VMEM((tm, d), F32)],
        compiler_params=_params("parallel", "arbitrary"),
        name="ffn_swiglu",
    )(x, norm_w.reshape(1, d), w1, w3, w2)


def _row_copy(src_hbm, src_row, dst_vmem, dst_row, sem):
    return pltpu.make_async_copy(src_hbm.at[pl.ds(src_row, 1), :], dst_vmem.at[pl.ds(dst_row, 1), :], sem)


def _moe_kernel(be_ref, nr_ref, tok_ref, h_hbm, w1_ref, w3_ref, w2_ref, o_ref, xf_ref, xb_ref, sem, *, nb, nf):
    b, f = pl.program_id(0), pl.program_id(1)
    n_sub = MOE_ROW_TILE // MOE_SUB_TILE

    def rows_needed(blk):
        return (nr_ref[blk] + MOE_SUB_TILE - 1) // MOE_SUB_TILE * MOE_SUB_TILE

    def start_rows(blk, lo, hi):
        def body(r, carry):
            _row_copy(h_hbm, tok_ref[blk * MOE_ROW_TILE + r], xf_ref, r, sem).start()
            return carry
        lax.fori_loop(lo, hi, body, 0)

    @pl.when(f == 0)
    def _():
        @pl.when(b == 0)
        def _():
            start_rows(0, 0, rows_needed(0))

        def wait(r, carry):
            _row_copy(h_hbm, 0, xf_ref, r, sem).wait()
            return carry

        lax.fori_loop(0, rows_needed(b), wait, 0)
        o_ref[...] = jnp.zeros_like(o_ref)
        for sub in range(n_sub):
            @pl.when(nr_ref[b] > sub * MOE_SUB_TILE)
            def _():
                r0 = sub * MOE_SUB_TILE
                xb_ref[r0:r0 + MOE_SUB_TILE, :] = xf_ref[r0:r0 + MOE_SUB_TILE, :].astype(BF16)

    @pl.when((f > 0) & (b + 1 < nb))
    def _():
        per_step = -(-MOE_ROW_TILE // (nf - 1))
        start_rows(b + 1, (f - 1) * per_step, jnp.minimum(f * per_step, rows_needed(b + 1)))

    for sub in range(n_sub):
        rows = slice(sub * MOE_SUB_TILE, (sub + 1) * MOE_SUB_TILE)

        @pl.when(nr_ref[b] > sub * MOE_SUB_TILE)
        def _():
            h = xb_ref[rows, :]
            g = _dot(h, w1_ref[0].astype(BF16))
            u = _dot(h, w3_ref[0].astype(BF16))
            a = (g * jax.nn.sigmoid(g) * u).astype(BF16)
            o_ref[rows, :] += _dot(a, w2_ref[0].astype(BF16))


def _moe_experts(h, buf_tok, block_e, block_rows, w1, w3, w2):
    d = h.shape[1]
    cap = buf_tok.shape[0]
    dff = w1.shape[2]
    tm, tf = MOE_ROW_TILE, 256
    nb, nf = cap // tm, dff // tf

    def w13_map(b, f, be, nr, tok):
        return (be[b], 0, jnp.where(nr[b] > 0, f, nf - 1))

    def w2_map(b, f, be, nr, tok):
        return (be[b], jnp.where(nr[b] > 0, f, nf - 1), 0)

    grid_spec = pltpu.PrefetchScalarGridSpec(
        num_scalar_prefetch=3,
        grid=(nb, nf),
        in_specs=[
            pl.BlockSpec(memory_space=pl.ANY),
            pl.BlockSpec((1, d, tf), w13_map),
            pl.BlockSpec((1, d, tf), w13_map),
            pl.BlockSpec((1, tf, d), w2_map),
        ],
        out_specs=pl.BlockSpec((tm, d), lambda b, f, be, nr, tok: (b, 0)),
        scratch_shapes=[pltpu.VMEM((tm, d), F32), pltpu.VMEM((tm, d), BF16), pltpu.SemaphoreType.DMA(())],
    )
    return pl.pallas_call(
        functools.partial(_moe_kernel, nb=nb, nf=nf),
        out_shape=jax.ShapeDtypeStruct((cap, d), F32),
        grid_spec=grid_spec,
        compiler_params=_params("arbitrary", "arbitrary"),
        name="moe_swiglu",
    )(block_e, block_rows, buf_tok, h, w1, w3, w2)


def _combine_kernel(pos_ref, x_ref, g_ref, yb_hbm, *rest, tq, final_norm):
    if final_norm:
        fw_ref, o_ref, ybuf, sem = rest
    else:
        o_ref, ybuf, sem = rest
    i = pl.program_id(0)

    def start(r, carry):
        for k in range(TOP_K):
            _row_copy(yb_hbm, pos_ref[(i * tq + r) * TOP_K + k], ybuf.at[k], r, sem).start()
        return carry

    def wait(r, carry):
        for k in range(TOP_K):
            _row_copy(yb_hbm, 0, ybuf.at[k], r, sem).wait()
        return carry

    lax.fori_loop(0, tq, start, 0)
    lax.fori_loop(0, tq, wait, 0)
    g = g_ref[...]
    y = x_ref[...] + g[:, 0:1] * ybuf[0] + g[:, 1:2] * ybuf[1]
    if final_norm:
        y = _rms_rows(y, fw_ref[...])
    o_ref[...] = y


def _moe_combine(x, yb, pos, gates, final_w=None):
    t, d = x.shape
    tq = 256
    g = jnp.pad(gates, ((0, 0), (0, LANES - TOP_K)))
    in_specs = [pl.BlockSpec((tq, d), lambda i, pos: (i, 0)), pl.BlockSpec((tq, LANES), lambda i, pos: (i, 0)),
                pl.BlockSpec(memory_space=pl.ANY)]
    args = [pos.reshape(-1), x, g, yb]
    if final_w is not None:
        in_specs.append(pl.BlockSpec((1, d), lambda i, pos: (0, 0)))
        args.append(final_w.reshape(1, d))
    grid_spec = pltpu.PrefetchScalarGridSpec(
        num_scalar_prefetch=1,
        grid=(t // tq,),
        in_specs=in_specs,
        out_specs=pl.BlockSpec((tq, d), lambda i, pos: (i, 0)),
        scratch_shapes=[pltpu.VMEM((TOP_K, tq, d), F32), pltpu.SemaphoreType.DMA(())],
    )
    return pl.pallas_call(
        functools.partial(_combine_kernel, tq=tq, final_norm=final_w is not None),
        out_shape=jax.ShapeDtypeStruct((t, d), F32),
        grid_spec=grid_spec,
        compiler_params=_params("arbitrary"),
        name="moe_combine",
    )(*args)


def _route_kernel(x_ref, nw_ref, wr_ref, br_ref, h_ref, lg_ref):
    h = _rms_rows(x_ref[...], nw_ref[...])
    h_ref[...] = h
    hh = h.astype(BF16)
    hl = (h - hh.astype(F32)).astype(BF16)
    wr = wr_ref[...]
    wh = wr.astype(BF16)
    wl = (wr - wh.astype(F32)).astype(BF16)
    lg_ref[...] = _dot(hh, wh) + _dot(hl, wh) + _dot(hh, wl) + br_ref[...]


def _norm_route(x, norm_w, w_router, b_router):
    t, d = x.shape
    tm = ROW_TILE
    wr = jnp.pad(w_router, ((0, 0), (0, LANES - N_EXPERTS)))
    br = jnp.pad(b_router, (0, LANES - N_EXPERTS)).reshape(1, LANES)
    h, lg = pl.pallas_call(
        _route_kernel,
        out_shape=(jax.ShapeDtypeStruct((t, d), F32), jax.ShapeDtypeStruct((t, LANES), F32)),
        grid=(t // tm,),
        in_specs=[pl.BlockSpec((tm, d), lambda i: (i, 0)), pl.BlockSpec((1, d), lambda i: (0, 0)),
                  pl.BlockSpec((d, LANES), lambda i: (0, 0)), pl.BlockSpec((1, LANES), lambda i: (0, 0))],
        out_specs=(pl.BlockSpec((tm, d), lambda i: (i, 0)), pl.BlockSpec((tm, LANES), lambda i: (i, 0))),
        compiler_params=_params("parallel"),
        name="moe_route",
    )(x, norm_w.reshape(1, d), wr, br)
    return h, lg[:, :N_EXPERTS]


def _final_norm_kernel(x_ref, w_ref, o_ref):
    o_ref[...] = _rms_rows(x_ref[...], w_ref[...])


def _final_norm(x, w):
    t, d = x.shape
    tm = ROW_TILE
    return pl.pallas_call(
        _final_norm_kernel,
        out_shape=jax.ShapeDtypeStruct((t, d), F32),
        grid=(t // tm,),
        in_specs=[pl.BlockSpec((tm, d), lambda i: (i, 0)), pl.BlockSpec((1, d), lambda i: (0, 0))],
        out_specs=pl.BlockSpec((tm, d), lambda i: (i, 0)),
        compiler_params=_params("parallel"),
        name="final_norm",
    )(x, w.reshape(1, d))


def _moe(x, norm_w, w_router, b_router, w1, w3, w2, final_w=None):
    t, d = x.shape
    tm = MOE_ROW_TILE
    h, logits = _norm_route(x, norm_w, w_router, b_router)
    lj, le = logits[:, None, :], logits[:, :, None]
    eidx = jnp.arange(N_EXPERTS, dtype=jnp.int32)
    beats = (lj > le) | ((lj == le) & (eidx[None, None, :] < eidx[None, :, None]))
    rank = jnp.sum(beats.astype(jnp.int32), axis=-1)
    sel = jnp.stack([rank == k for k in range(TOP_K)], axis=1)
    top_v = jnp.sum(jnp.where(sel, logits[:, None, :], 0.0), axis=-1)
    gates = jax.nn.softmax(top_v, axis=-1)
    n_slots = t * TOP_K
    oh = sel.reshape(n_slots, N_EXPERTS).astype(jnp.int32)
    counts = jnp.sum(oh, axis=0)
    padded = (counts + tm - 1) // tm * tm
    pad_ends = jnp.cumsum(padded)
    pad_starts = pad_ends - padded
    within = jnp.cumsum(oh, axis=0) - oh
    slot_pos = jnp.sum(oh * (within + pad_starts[None, :]), axis=-1)
    nb = n_slots // tm + N_EXPERTS
    cap = nb * tm
    buf_tok = jnp.zeros((cap,), jnp.int32).at[slot_pos].set(jnp.arange(n_slots, dtype=jnp.int32) // TOP_K)
    block_start = jnp.arange(nb, dtype=jnp.int32) * tm
    block_e = jnp.minimum(jnp.sum((pad_ends[None, :] <= block_start[:, None]).astype(jnp.int32), axis=-1),
                          N_EXPERTS - 1)
    block_rows = jnp.clip(counts[block_e] - (block_start - pad_starts[block_e]), 0, tm).astype(jnp.int32)
    yb = _moe_experts(h, buf_tok, block_e.astype(jnp.int32), block_rows, w1, w3, w2)
    return _moe_combine(x, yb, slot_pos.astype(jnp.int32), gates, final_w)


def _s5_kernel(u_ref, sre_ref, sim_ref, are_ref, aim_ref, bb_ref, cre_ref, cim_ref, d_ref,
               gw_ref, gb_ref, nw_ref, y_ref, ore_ref, oim_ref, xr_ref, xi_ref, st_ref, *, tc, nbb):
    c = pl.program_id(1)
    rows = tc * nbb
    u = u_ref[...].reshape(rows, S5_WIDTH)
    ub = u.astype(BF16)
    for sl in range(S5_SLABS):
        bu = _dot(ub[:, sl * LANES:(sl + 1) * LANES], bb_ref[sl])
        xr_ref[:, sl * S5_SLAB_CH:(sl + 1) * S5_SLAB_CH] = bu[:, :S5_SLAB_CH]
        xi_ref[:, sl * S5_SLAB_CH:(sl + 1) * S5_SLAB_CH] = bu[:, S5_SLAB_CH:]

    @pl.when(c == 0)
    def _():
        st_ref[0] = sre_ref[...]
        st_ref[1] = sim_ref[...]

    ar = are_ref[...]
    ai = aim_ref[...]

    def step(t, carry):
        for g in range(nbb // SUBLANES):
            r0 = pl.multiple_of(t * nbb + g * SUBLANES, SUBLANES)
            sl = slice(g * SUBLANES, (g + 1) * SUBLANES)
            pr = st_ref[0, sl, :]
            pi = st_ref[1, sl, :]
            nr = ar * pr - ai * pi + xr_ref[pl.ds(r0, SUBLANES), :]
            ni = ar * pi + ai * pr + xi_ref[pl.ds(r0, SUBLANES), :]
            xr_ref[pl.ds(r0, SUBLANES), :] = nr
            xi_ref[pl.ds(r0, SUBLANES), :] = ni
            st_ref[0, sl, :] = nr
            st_ref[1, sl, :] = ni
        return carry

    lax.fori_loop(0, tc, step, 0)

    ys = []
    for sl in range(S5_SLABS):
        ch = slice(sl * S5_SLAB_CH, (sl + 1) * S5_SLAB_CH)
        ys.append(_dot(xr_ref[:, ch].astype(BF16), cre_ref[sl]) - _dot(xi_ref[:, ch].astype(BF16), cim_ref[sl]))
    y = jnp.concatenate(ys, axis=1) + d_ref[...] * u
    gy = 0.5 * y * (1.0 + jnp.tanh(0.7978845608028654 * (y + 0.044715 * (y * y * y))))
    y = gy * _sigmoid(_dot(gy.astype(BF16), gw_ref[...]) + gb_ref[...])
    y_ref[...] = _rms_rows(y, nw_ref[...]).reshape(tc, nbb, S5_WIDTH)

    @pl.when(c == pl.num_programs(1) - 1)
    def _():
        ore_ref[...] = st_ref[0]
        oim_ref[...] = st_ref[1]


def _s5(u_tm, st_re, st_im, prm, *, tc, nbb):
    L, n, _ = u_tm.shape
    ch = S5_CHANNELS
    vec = lambda w: pl.BlockSpec((1, w), lambda s, c: (0, 0))
    mat = lambda a, b: pl.BlockSpec((a, b), lambda s, c: (0, 0))
    slab = lambda a, b: pl.BlockSpec((S5_SLABS, a, b), lambda s, c: (0, 0, 0))
    st_spec = pl.BlockSpec((nbb, ch), lambda s, c: (s, 0))
    return pl.pallas_call(
        functools.partial(_s5_kernel, tc=tc, nbb=nbb),
        out_shape=(jax.ShapeDtypeStruct((L, n, S5_WIDTH), F32),
                   jax.ShapeDtypeStruct((n, ch), F32), jax.ShapeDtypeStruct((n, ch), F32)),
        grid=(n // nbb, L // tc),
        in_specs=[pl.BlockSpec((tc, nbb, S5_WIDTH), lambda s, c: (c, s, 0)), st_spec, st_spec,
                  vec(ch), vec(ch), slab(LANES, 2 * S5_SLAB_CH), slab(S5_SLAB_CH, LANES), slab(S5_SLAB_CH, LANES),
                  vec(S5_WIDTH), mat(S5_WIDTH, S5_WIDTH), vec(S5_WIDTH), vec(S5_WIDTH)],
        out_specs=(pl.BlockSpec((tc, nbb, S5_WIDTH), lambda s, c: (c, s, 0)), st_spec, st_spec),
        scratch_shapes=[pltpu.VMEM((tc * nbb, ch), F32), pltpu.VMEM((tc * nbb, ch), F32),
                        pltpu.VMEM((2, nbb, ch), F32)],
        compiler_params=_params("parallel", "arbitrary"),
        name="s5_mixer",
    )(u_tm, st_re, st_im, *prm)


def _s5_params(lam_re, lam_im, b_re, b_im, c_re, c_im, d_skip, log_dt, glu_w, glu_b, norm_w):
    delta = jnp.exp(log_dt)[:, None]
    mag = jnp.exp(lam_re * delta)
    ab_re, ab_im = mag * jnp.cos(lam_im * delta), mag * jnp.sin(lam_im * delta)
    den = lam_re * lam_re + lam_im * lam_im
    q_re = ((ab_re - 1.0) * lam_re + ab_im * lam_im) / den
    q_im = (ab_im * lam_re - (ab_re - 1.0) * lam_im) / den
    bb_re = q_re[..., None] * b_re - q_im[..., None] * b_im
    bb_im = q_re[..., None] * b_im + q_im[..., None] * b_re
    gs = S5_GROUPS // S5_SLABS
    eye = jnp.eye(gs, dtype=F32)

    def in_blockdiag(bb):
        t = jnp.swapaxes(bb, 1, 2).reshape(S5_SLABS, gs, S5_GROUP, S5_STATE)
        return (eye[None, :, None, :, None] * t[:, :, :, None, :]).reshape(S5_SLABS, LANES, S5_SLAB_CH)

    def out_blockdiag(cc):
        t = jnp.swapaxes(cc, 1, 2).reshape(S5_SLABS, gs, S5_STATE, S5_GROUP)
        return (eye[None, :, None, :, None] * t[:, :, :, None, :]).reshape(S5_SLABS, S5_SLAB_CH, LANES).astype(BF16)

    bb = jnp.concatenate([in_blockdiag(bb_re), in_blockdiag(bb_im)], axis=2).astype(BF16)
    return (ab_re.reshape(1, S5_CHANNELS), ab_im.reshape(1, S5_CHANNELS), bb,
            out_blockdiag(c_re), out_blockdiag(c_im), d_skip.reshape(1, S5_WIDTH), glu_w.astype(BF16),
            glu_b.reshape(1, S5_WIDTH), norm_w.reshape(1, S5_WIDTH))


def _ssd_kernel(p_ref, st_ref, cs_ref, cw_ref, cb_ref, dtb_ref, a_ref, dsk_ref, nw_ref,
                y_ref, ost_ref, ocs_ref, ext_ref, win_ref, s_ref, *, q, sb):
    c = pl.program_id(1)
    last = c == pl.num_programs(1) - 1
    hd, nh, gw = SSD_HEAD_DIM, SSD_HEADS, SSD_WIDTH // SSD_GROUPS
    pad_rows = hd - q

    lane = lax.broadcasted_iota(jnp.int32, (q, LANES), 1)
    row = lax.broadcasted_iota(jnp.int32, (q, LANES), 0)
    causal2 = row >= (lane % hd)
    lane64 = lax.broadcasted_iota(jnp.int32, (hd, LANES), 1)
    tri = (lax.broadcasted_iota(jnp.int32, (q, q), 0) >= lax.broadcasted_iota(jnp.int32, (q, q), 1)).astype(F32)
    e_h = lax.broadcasted_iota(jnp.int32, (LANES, SSD_WIDTH), 0)
    e_c = lax.broadcasted_iota(jnp.int32, (LANES, SSD_WIDTH), 1)
    expand = (e_h == e_c // hd).astype(F32)
    i_s = lax.broadcasted_iota(jnp.int32, (q, SSD_WIDTH), 0)
    i_c = lax.broadcasted_iota(jnp.int32, (q, SSD_WIDTH), 1)
    eye_x = (i_s == i_c % hd).astype(F32)

    for s in range(sb):
        rs = slice(s * q, (s + 1) * q)

        @pl.when(c == 0)
        def _():
            win_ref[s, 0:5, :] = jnp.zeros((5, SSD_CONV_DIM), F32)
            win_ref[s, 5:8, :] = cs_ref[s]
            for g in range(SSD_GROUPS):
                for k in range(gw // LANES):
                    r0 = g * gw + k * LANES
                    s_ref[s, g, :, k * LANES:(k + 1) * LANES] = st_ref[s, r0:r0 + LANES, :].T

        z = p_ref[rs, PROJ_Z:PROJ_XBC]
        xbc = p_ref[rs, PROJ_XBC:PROJ_HR]
        dt = p_ref[rs, PROJ_DT:PROJ_DT + LANES]
        ext_ref[0:8, :] = win_ref[s]
        ext_ref[8:8 + q, :] = xbc
        conv = cb_ref[...]
        for j in range(SSD_CONV):
            conv = conv + cw_ref[j:j + 1, :] * ext_ref[pl.ds(5 + j, q), :]
        win_ref[s] = ext_ref[q:q + 8, :]
        xc = _silu(conv)
        xs = xc[:, :SSD_WIDTH]
        bm = xc[:, SSD_WIDTH:SSD_WIDTH + SSD_GROUPS * SSD_STATE]
        cm = xc[:, SSD_WIDTH + SSD_GROUPS * SSD_STATE:]

        step = _softplus(dt + dtb_ref[...])
        adt = step * a_ref[...]
        step_x = _dot_exact(step, expand)
        acs_x = _dot_exact(tri, _dot_exact(adt, expand))
        diag = jnp.sum(acs_x * eye_x, axis=0, keepdims=True)
        acs_last = acs_x[q - 1:q, :]
        xdt = xs * step_x
        exp_acs = jnp.exp(acs_x)
        xw = xdt * jnp.exp(acs_last - acs_x)
        dec = jnp.exp(acs_last)

        for g in range(SSD_GROUPS):
            bg = bm[:, g * SSD_STATE:(g + 1) * SSD_STATE]
            cg = cm[:, g * SSD_STATE:(g + 1) * SSD_STATE].astype(BF16)
            gl = slice(g * gw, (g + 1) * gw)
            b64 = bg if pad_rows == 0 else jnp.concatenate([bg, jnp.zeros((pad_rows, SSD_STATE), F32)], axis=0)
            cb2 = _dot_nt(cg, jnp.concatenate([b64, b64], axis=0).astype(BF16))
            sg = s_ref[s, g]
            yoff = _dot(cg, sg.astype(BF16)) * exp_acs[:, gl]
            for pr in range(gw // LANES):
                l0 = g * gw + pr * LANES
                seg = acs_x[:, l0:l0 + LANES] - diag[:, l0:l0 + LANES]
                m = cb2 * jnp.exp(jnp.where(causal2, seg, -jnp.inf))
                xd = xdt[:, l0:l0 + LANES]
                xd64 = xd if pad_rows == 0 else jnp.concatenate([xd, jnp.zeros((pad_rows, LANES), F32)], axis=0)
                rhs = jnp.concatenate([jnp.where(lane64 < hd, xd64, 0.0), jnp.where(lane64 >= hd, xd64, 0.0)], axis=0)
                ydiag = _dot(m.astype(BF16), rhs.astype(BF16))
                y_ref[rs, l0:l0 + LANES] = ydiag + yoff[:, pr * LANES:(pr + 1) * LANES]
            bpad = jnp.concatenate([bg, jnp.zeros((LANES - q, SSD_STATE), F32)], axis=0)
            xwpad = jnp.concatenate([xw[:, gl], jnp.zeros((LANES - q, gw), F32)], axis=0)
            s_ref[s, g] = dec[:, gl] * sg + _dot(bpad.T.astype(BF16), xwpad.astype(BF16))

        y = y_ref[rs, :] + dsk_ref[...] * xs
        y = y * _silu(z)
        halves = []
        for g in range(SSD_GROUPS):
            yg = y[:, g * gw:(g + 1) * gw]
            halves.append(yg * lax.rsqrt(jnp.mean(yg * yg, axis=-1, keepdims=True) + RMS_EPS))
        y_ref[rs, :] = jnp.concatenate(halves, axis=1) * nw_ref[...]

        @pl.when(last)
        def _():
            ocs_ref[s] = ext_ref[q + 5:q + 8, :]
            for g in range(SSD_GROUPS):
                for k in range(gw // LANES):
                    r0 = g * gw + k * LANES
                    ost_ref[s, r0:r0 + LANES, :] = s_ref[s, g, :, k * LANES:(k + 1) * LANES].T


def _ssd(proj, row0, nseq, L, st, cs, seq0, prm, *, sb):
    q = min(L, SSD_CHUNK)
    nchunk = L // q
    rows = sb * q
    base = row0 // rows
    sbase = seq0 // sb
    assert row0 % rows == 0 and nseq % sb == 0 and seq0 % sb == 0
    vec = lambda w: pl.BlockSpec((1, w), lambda s, c: (0, 0))
    st_spec = pl.BlockSpec((sb, SSD_WIDTH, SSD_STATE), lambda s, c: (s, 0, 0))
    cs_spec = pl.BlockSpec((sb, SSD_CONV - 1, SSD_CONV_DIM), lambda s, c: (s, 0, 0))
    st_in = pl.BlockSpec((sb, SSD_WIDTH, SSD_STATE), lambda s, c: (sbase + s, 0, 0))
    cs_in = pl.BlockSpec((sb, SSD_CONV - 1, SSD_CONV_DIM), lambda s, c: (sbase + s, 0, 0))
    return pl.pallas_call(
        functools.partial(_ssd_kernel, q=q, sb=sb),
        out_shape=(jax.ShapeDtypeStruct((nseq * L, SSD_WIDTH), F32),
                   jax.ShapeDtypeStruct((nseq, SSD_WIDTH, SSD_STATE), F32),
                   jax.ShapeDtypeStruct((nseq, SSD_CONV - 1, SSD_CONV_DIM), F32)),
        grid=(nseq // sb, nchunk),
        in_specs=[pl.BlockSpec((rows, PROJ_WIDTH), lambda s, c: (base + s * nchunk + c, 0)), st_in, cs_in,
                  pl.BlockSpec((SSD_CONV, SSD_CONV_DIM), lambda s, c: (0, 0)), vec(SSD_CONV_DIM),
                  vec(LANES), vec(LANES), vec(SSD_WIDTH), vec(SSD_WIDTH)],
        out_specs=(pl.BlockSpec((rows, SSD_WIDTH), lambda s, c: (s * nchunk + c, 0)), st_spec, cs_spec),
        scratch_shapes=[pltpu.VMEM((q + 8, SSD_CONV_DIM), F32), pltpu.VMEM((sb, 8, SSD_CONV_DIM), F32),
                        pltpu.VMEM((sb, SSD_GROUPS, SSD_STATE, SSD_WIDTH // SSD_GROUPS), F32)],
        compiler_params=_params("parallel", "arbitrary"),
        name="ssd_mixer",
    )(proj, st, cs, *prm)


def _ssd_params(conv_w, conv_b, dt_bias, a_log, d_skip, norm_w):
    pad = jnp.zeros((LANES - SSD_HEADS,), F32)
    return (conv_w, conv_b.reshape(1, SSD_CONV_DIM), jnp.concatenate([dt_bias, pad]).reshape(1, LANES),
            jnp.concatenate([-jnp.exp(a_log), pad]).reshape(1, LANES),
            jnp.repeat(d_skip, SSD_HEAD_DIM).reshape(1, SSD_WIDTH), norm_w.reshape(1, SSD_WIDTH))


def _attn_kernel(q_ref, k_ref, v_ref, o_ref, *, lq, sb):
    scale = MEM_HEAD_DIM ** -0.5
    heads = [slice(h * MEM_HEAD_DIM, (h + 1) * MEM_HEAD_DIM) for h in range(MEM_HEADS)]
    for s in range(sb):
        rq = slice(s * lq, (s + 1) * lq)
        rk = slice(s * MEM_TOKENS, (s + 1) * MEM_TOKENS)
        sc = [_dot_nt(q_ref[rq, cl].astype(BF16), k_ref[rk, cl].astype(BF16)) * scale for cl in heads]
        sc = [x - jnp.max(x, axis=-1, keepdims=True) for x in sc]
        p = [jnp.exp(x) for x in sc]
        p = [x / jnp.sum(x, axis=-1, keepdims=True) for x in p]
        for cl, x in zip(heads, p):
            o_ref[rq, cl] = _dot(x.astype(BF16), v_ref[rk, cl].astype(BF16))


def _attend(q, row0, nseq, L, k2d, v2d, kcol, vcol, seq0, *, lq, sb):
    nl = L // lq
    rows = sb * lq
    base = row0 // rows
    kbase = seq0 // sb
    assert row0 % rows == 0 and (sb == 1 or nl == 1) and seq0 % sb == 0
    return pl.pallas_call(
        functools.partial(_attn_kernel, lq=lq, sb=sb),
        out_shape=jax.ShapeDtypeStruct((nseq * L, MEM_WIDTH), F32),
        grid=(nseq // sb, nl),
        in_specs=[pl.BlockSpec((rows, MEM_WIDTH), lambda s, l: (base + s * nl + l, 0)),
                  pl.BlockSpec((sb * MEM_TOKENS, MEM_WIDTH), lambda s, l: (kbase + s, kcol)),
                  pl.BlockSpec((sb * MEM_TOKENS, MEM_WIDTH), lambda s, l: (kbase + s, vcol))],
        out_specs=pl.BlockSpec((rows, MEM_WIDTH), lambda s, l: (s * nl + l, 0)),
        compiler_params=_params("parallel", "arbitrary"),
        name="mem_attention",
    )(q, k2d, v2d)


def _head_sum(x, ones_bd):
    return _dot_exact(x, ones_bd)


def _rwkv_prep_kernel(h_ref, p_ref, mu_ref, wl_ref, w0_ref, a0_ref, kk_ref, ka_ref, rk_ref, ones_ref,
                      r_ref, w_ref, k_ref, v_ref, n_ref, b_ref, g_ref, bo_ref):
    W = RWKV_WIDTH
    h = h_ref[...]
    hs = h + (p_ref[...] - h) * mu_ref[...]
    r, k, v = hs[:, :W], hs[:, W:2 * W], hs[:, 2 * W:3 * W]
    lo = hs[:, 3 * W:]
    lane = lax.broadcasted_iota(jnp.int32, lo.shape, 1)
    act = jnp.where(lane < RWKV_W_LORA, jnp.tanh(lo),
                    jnp.where(lane < RWKV_W_LORA + RWKV_A_LORA, lo, _sigmoid(lo)))
    lora = _dot(act.astype(BF16), wl_ref[...])
    w_log = -_softplus(-(w0_ref[...] + lora[:, :W])) - 0.5
    a = _sigmoid(a0_ref[...] + lora[:, W:2 * W])
    ones_bd = ones_ref[...]
    kk = k * kk_ref[...]
    kk = kk / jnp.maximum(jnp.sqrt(_head_sum(kk * kk, ones_bd)), 1e-12)
    k2 = k * (1.0 + (a - 1.0) * ka_ref[...])
    r_ref[...] = r
    w_ref[...] = -jnp.exp(w_log)
    k_ref[...] = k2
    v_ref[...] = v
    n_ref[...] = kk
    b_ref[...] = kk * a
    g_ref[...] = lora[:, 2 * W:]
    bo_ref[...] = _head_sum(r * k2 * rk_ref[...], ones_bd) * v


def _rwkv_prep(hr, prev, prm):
    m = hr.shape[0]
    tm, W = ROW_TILE, RWKV_WIDTH
    mu, wl, w0, a0, k_k, k_a, r_k, ones_bd = prm
    vec = lambda w: pl.BlockSpec((1, w), lambda i: (0, 0))
    row = lambda w: pl.BlockSpec((tm, w), lambda i: (i, 0))
    return pl.pallas_call(
        _rwkv_prep_kernel,
        out_shape=tuple(jax.ShapeDtypeStruct((m, W), F32) for _ in range(8)),
        grid=(m // tm,),
        in_specs=[row(RWKV_SHIFT_DIM), row(RWKV_SHIFT_DIM), vec(RWKV_SHIFT_DIM),
                  pl.BlockSpec(wl.shape, lambda i: (0, 0)), vec(W), vec(W), vec(W), vec(W), vec(W),
                  pl.BlockSpec((W, W), lambda i: (0, 0))],
        out_specs=tuple(row(W) for _ in range(8)),
        compiler_params=_params("parallel"),
        name="rwkv_prep",
    )(hr, prev, mu, wl, w0, a0, k_k, k_a, r_k, ones_bd)


def _rwkv_post_kernel(yp_ref, ys_ref, bo_ref, g_ref, gw_ref, gb_ref, ones_ref, o_ref, *, n_first):
    ones_bd = ones_ref[...]
    y = jnp.where(pl.program_id(0) < n_first, yp_ref[...], ys_ref[...])
    inv = 1.0 / RWKV_HEAD_DIM
    d = y - _head_sum(y, ones_bd) * inv
    var = _head_sum(d * d, ones_bd) * inv
    yn = d * lax.rsqrt(var + RWKV_GN_EPS) * gw_ref[...] + gb_ref[...]
    o_ref[...] = (yn + bo_ref[...]) * g_ref[...]


def _rwkv_post(y_pair, bonus, g, gn_w, gn_b, ones_bd):
    m, W = bonus.shape
    tm = ROW_TILE
    vec = pl.BlockSpec((1, W), lambda i: (0, 0))
    row = pl.BlockSpec((tm, W), lambda i: (i, 0))
    n_first, y_specs = _split_rows_specs(y_pair, tm, W, 1)
    return pl.pallas_call(
        functools.partial(_rwkv_post_kernel, n_first=n_first),
        out_shape=jax.ShapeDtypeStruct((m, W), F32),
        grid=(m // tm,),
        in_specs=y_specs + [row, row, vec, vec, pl.BlockSpec((W, W), lambda i: (0, 0))],
        out_specs=row,
        compiler_params=_params("parallel"),
        name="rwkv_post",
    )(*y_pair, bonus, g, gn_w.reshape(1, W), gn_b.reshape(1, W), ones_bd)


def _rwkv_params(mu, w0, w2, a0, a2, g2, k_k, k_a, r_k):
    W = RWKV_WIDTH
    nl = RWKV_W_LORA + RWKV_A_LORA + RWKV_G_LORA
    wl = jnp.zeros((nl, 3 * W), F32)
    wl = wl.at[:RWKV_W_LORA, :W].set(w2)
    wl = wl.at[RWKV_W_LORA:RWKV_W_LORA + RWKV_A_LORA, W:2 * W].set(a2)
    wl = wl.at[RWKV_W_LORA + RWKV_A_LORA:, 2 * W:].set(g2)
    head = jnp.arange(W) // RWKV_HEAD_DIM
    ones_bd = (head[:, None] == head[None, :]).astype(F32)
    v = lambda t: t.reshape(1, -1)
    return (v(mu), wl.astype(BF16), v(w0), v(a0), v(k_k), v(k_a), v(r_k), ones_bd)


def _bdot(a, b):
    return jnp.dot(a.astype(BF16), b.astype(BF16), preferred_element_type=F32)


def _bdot_nt(a, b):
    return lax.dot_general(a.astype(BF16), b.astype(BF16), (((1,), (1,)), ((), ())), preferred_element_type=F32)


def _split(x):
    hi = x.astype(BF16)
    return hi, (x - hi.astype(F32)).astype(BF16)


def _dot3(a, b):
    ah, al = _split(a)
    bh, bl = _split(b)
    return (jnp.dot(ah, bh, preferred_element_type=F32) + jnp.dot(al, bh, preferred_element_type=F32)
            + jnp.dot(ah, bl, preferred_element_type=F32))


def _dot3_nt(a, b):
    ah, al = _split(a)
    bh, bl = _split(b)
    dn = (((1,), (1,)), ((), ()))
    return (lax.dot_general(ah, bh, dn, preferred_element_type=F32)
            + lax.dot_general(al, bh, dn, preferred_element_type=F32)
            + lax.dot_general(ah, bl, dn, preferred_element_type=F32))


RWKV_GROUP = 4
RWKV_GW = RWKV_GROUP * RWKV_HEAD_DIM
RWKV_NG = RWKV_HEADS // RWKV_GROUP


def _rwkv_chunk_kernel(r_ref, ls_ref, k_ref, v_ref, n_ref, b_ref, s0_ref, y_ref, sf_ref, s_ref, *, C, ns):
    c = pl.program_id(1)
    G, GW, HD = RWKV_GROUP, RWKV_GW, RWKV_HEAD_DIM
    RI = ns * C
    R = G * RI
    SB = G * C
    groups = range(RWKV_NG)

    @pl.when(c == 0)
    def _():
        for q in groups:
            for s in range(ns):
                s_ref[q, s] = jnp.concatenate([s0_ref[s, G * q + h] for h in range(G)], axis=1)

    ri = lax.broadcasted_iota(jnp.int32, (2 * RI, RI), 0)
    ci = lax.broadcasted_iota(jnp.int32, (2 * RI, RI), 1)
    same = ((ri % RI) // C) == (ci // C)
    cum = (same & ((ri >= RI) | (ri >= ci))).astype(BF16)
    ls_all = ls_ref[...]
    l1 = ls_all.astype(BF16)
    l2f = ls_all - l1.astype(F32)
    l2 = l2f.astype(BF16)
    l3 = (l2f - l2.astype(F32)).astype(BF16)
    lw2 = (jnp.dot(cum, l1, preferred_element_type=F32) + jnp.dot(cum, l2, preferred_element_type=F32)
           + jnp.dot(cum, l3, preferred_element_type=F32))
    lw_all, lwl_all = lw2[:RI], lw2[RI:]

    lane_in = lax.broadcasted_iota(jnp.int32, (C, GW), 1) // HD
    row = lax.broadcasted_iota(jnp.int32, (R, R), 0)
    col = lax.broadcasted_iota(jnp.int32, (R, R), 1)
    ent = (row // C) == (col // C)
    strict = ent & (row > col)
    incl = ent & (row >= col)
    eye = row == col
    eye_f = eye.astype(F32)
    own = (lax.broadcasted_iota(jnp.int32, (SB, GW), 0) // C) == (lax.broadcasted_iota(jnp.int32, (SB, GW), 1) // HD)
    rows_r = lax.broadcasted_iota(jnp.int32, (R, GW), 0)
    rows_2r = lax.broadcasted_iota(jnp.int32, (2 * R, GW), 0)

    def stack(x):
        parts = []
        for s in range(ns):
            xs = x[s * C:(s + 1) * C]
            parts += [jnp.where(lane_in == h, xs, 0.0) for h in range(G)]
        return jnp.concatenate(parts, axis=0)

    def dup(x):
        parts = []
        for s in range(ns):
            parts += [x[s * C:(s + 1) * C]] * G
        return jnp.concatenate(parts, axis=0)

    st = []
    for q in groups:
        gl = slice(q * GW, (q + 1) * GW)
        lw, lwl, ls = lw_all[:, gl], lwl_all[:, gl], ls_all[:, gl]
        w_inv = jnp.exp(-lw)
        w_rest = jnp.exp(lwl - lw)
        kk, bb = k_ref[:, gl], b_ref[:, gl]
        st.append(dict(
            n_st=stack(n_ref[:, gl] * jnp.exp(lw - ls)), r_st=stack(r_ref[:, gl] * jnp.exp(lw)),
            v_st=stack(v_ref[:, gl]), bh_st=stack(bb * w_rest), kh_st=stack(kk * w_rest),
            b_dup=dup(bb * w_inv), k_dup=dup(kk * w_inv), w_c=jnp.exp(lwl)))
    for d in st:
        nr = jnp.concatenate([d['n_st'], d['r_st']], axis=0)
        gb = _bdot_nt(nr, d['b_dup'])
        gk = _bdot_nt(nr, d['k_dup'])
        d['a_nb'] = jnp.where(strict, gb[:R], 0.0)
        d['a_rb'] = jnp.where(incl, gb[R:], 0.0)
        d['a_nk'] = jnp.where(strict, gk[:R], 0.0)
        d['a_rk'] = jnp.where(incl, gk[R:], 0.0)
        d['t'] = eye_f - d['a_nb']
        d['p'] = d['a_nb']
    for _ in range(C.bit_length() - 2):
        for d in st:
            d['p'] = _bdot(d['p'], d['p'])
        for d in st:
            d['t'] = _bdot(d['t'], eye_f + d['p'])
    for d in st:
        d['p1'] = _bdot(d['t'], d['n_st'])
        d['z'] = _bdot(d['a_nk'], d['v_st'])
    for d in st:
        d['p2'] = _bdot(d['t'], d['z'])
    for q, d in enumerate(st):
        p1, p2 = d['p1'], d['p2']
        p1_t = p1.T
        lt = jnp.concatenate([d['v_st'].T, -p2.T], axis=1)
        kb = jnp.concatenate([d['kh_st'], d['bh_st']], axis=0)
        sa_parts, rs_parts = [], []
        for s in range(ns):
            rsl = slice(s * SB, (s + 1) * SB)
            S = s_ref[q, s]
            ss = jnp.concatenate([S] * G, axis=0)
            xr = _dot3_nt(jnp.concatenate([p1[rsl], d['r_st'][rsl]], axis=0), ss)
            sa_parts.append(-jnp.where(own, xr[:SB], 0.0) - p2[rsl])
            rs_parts.append(jnp.where(own, xr[SB:], 0.0))
            if ns == 1:
                bh_s, kb_s = d['bh_st'], kb
            else:
                bh_s = jnp.where((rows_r // SB) == s, d['bh_st'], 0.0)
                kb_s = jnp.where(((rows_2r % R) // SB) == s, kb, 0.0)
            m_bd = jnp.where(eye, d['w_c'][s * C:s * C + 1, :], 0.0) - _bdot(p1_t, bh_s)
            nf = _bdot(lt, kb_s)
            fold = nf[:HD]
            for h in range(1, G):
                fold = fold + nf[h * HD:(h + 1) * HD]
            s_ref[q, s] = _dot3(S, m_bd) + fold
        sa_st = jnp.concatenate(sa_parts, axis=0) if ns > 1 else sa_parts[0]
        rs_st = jnp.concatenate(rs_parts, axis=0) if ns > 1 else rs_parts[0]
        y_st = rs_st + _bdot(jnp.concatenate([d['a_rb'], d['a_rk']], axis=1),
                             jnp.concatenate([sa_st, d['v_st']], axis=0))
        for s in range(ns):
            y = y_st[s * SB:s * SB + C]
            for h in range(1, G):
                y = y + y_st[s * SB + h * C:s * SB + (h + 1) * C]
            y_ref[s * C:(s + 1) * C, q * GW:(q + 1) * GW] = y

    @pl.when(c == pl.num_programs(1) - 1)
    def _():
        for q in groups:
            for s in range(ns):
                S = s_ref[q, s]
                for h in range(G):
                    sf_ref[s, G * q + h] = S[:, h * HD:(h + 1) * HD]


def _rwkv_chunked(r, ls, k, v, kk, b, state, seq0, row0, nseq, L):
    HD, W = RWKV_HEAD_DIM, RWKV_WIDTH
    C = min(L, HD)
    ns = HD // C
    nt = L // C
    rows = ns * C
    base = row0 // rows
    sbase = seq0 // ns
    assert row0 % rows == 0 and (ns == 1 or nt == 1) and nseq % ns == 0 and seq0 % ns == 0
    row_spec = pl.BlockSpec((rows, W), lambda s, c: (base + s * nt + c, 0))
    st_spec = pl.BlockSpec((ns, RWKV_HEADS, HD, HD), lambda s, c: (s, 0, 0, 0))
    st_in = pl.BlockSpec((ns, RWKV_HEADS, HD, HD), lambda s, c: (sbase + s, 0, 0, 0))
    return pl.pallas_call(
        functools.partial(_rwkv_chunk_kernel, C=C, ns=ns),
        out_shape=(jax.ShapeDtypeStruct((nseq * L, W), F32), jax.ShapeDtypeStruct((nseq, RWKV_HEADS, HD, HD), F32)),
        grid=(nseq // ns, nt),
        in_specs=[row_spec] * 6 + [st_in],
        out_specs=(pl.BlockSpec((rows, W), lambda s, c: (s * nt + c, 0)), st_spec),
        scratch_shapes=[pltpu.VMEM((RWKV_NG, ns, HD, RWKV_GW), F32)],
        compiler_params=_params("parallel", "arbitrary"),
        name="rwkv_chunked",
    )(r, ls, k, v, kk, b, state)


def kernel(x_prompt, x_sample, mem_prompt, cache_mem_k, cache_mem_v, state_ssd, state_ssd_conv, state_rwkv, state_rwkv_shift, state_s5_re, state_s5_im, norm_mix, w_in, ssd_conv_w, ssd_conv_b, ssd_dt_bias, ssd_a_log, ssd_d, ssd_norm_w, rwkv_mu, rwkv_w0, rwkv_w2, rwkv_a0, rwkv_a2, rwkv_g2, rwkv_k_k, rwkv_k_a, rwkv_r_k, rwkv_gn_w, rwkv_gn_b, s5_lam_re, s5_lam_im, s5_b_re, s5_b_im, s5_c_re, s5_c_im, s5_d, s5_log_dt, s5_glu_w, s5_glu_b, s5_norm_w, w_out, norm_mem, mem_norm_w, wq_mem, wk_mem, wv_mem, wo_mem, norm_ffn, ffn_w1, ffn_w3, ffn_w2, moe_router_w, moe_router_b, moe_w1, moe_w3, moe_w2, final_norm_w):
    bp, lp, d = x_prompt.shape
    bs, ls, _ = x_sample.shape
    tp, ts = bp * lp, bs * ls
    x = jnp.concatenate([x_prompt.reshape(tp, d), x_sample.reshape(ts, d)], axis=0)
    mem_rows = mem_prompt.reshape(bp * MEM_TOKENS, d)
    s5_pad = SUBLANES - bp
    ssd_all = state_ssd.reshape(DEPTH * bs, SSD_WIDTH, SSD_STATE)
    conv_all = state_ssd_conv.reshape(DEPTH * bs, SSD_CONV - 1, SSD_CONV_DIM)
    wkv_all = state_rwkv.reshape(DEPTH * bs, RWKV_HEADS, RWKV_HEAD_DIM, RWKV_HEAD_DIM)
    ck_all = cache_mem_k.reshape(DEPTH * bs * MEM_TOKENS, MEM_WIDTH)
    cv_all = cache_mem_v.reshape(DEPTH * bs * MEM_TOKENS, MEM_WIDTH)
    ssd_zero = jnp.zeros((bp, SSD_WIDTH, SSD_STATE), F32)
    conv_zero = jnp.zeros((bp, SSD_CONV - 1, SSD_CONV_DIM), F32)
    wkv_zero = jnp.zeros((bp, RWKV_HEADS, RWKV_HEAD_DIM, RWKV_HEAD_DIM), F32)

    p_mk, p_mv, p_st, s_st = [], [], [], []
    for i in range(DEPTH):
        c0 = SSD_WIDTH
        c1 = c0 + SSD_CONV_DIM
        c2 = c1 + SSD_HEADS
        wi = w_in[i]
        w_in_packed = jnp.concatenate(
            [wi[:, :c1], wi[:, c2:], wi[:, c1:c2], jnp.zeros((d, PROJ_WIDTH - PROJ_DT - SSD_HEADS), F32)],
            axis=1).astype(BF16)
        proj = _mm(x, w_in_packed, norm_w=norm_mix[i])

        wkv = jnp.concatenate([wk_mem[i], wv_mem[i]], axis=1).astype(BF16)
        kv = _mm(mem_rows, wkv, norm_w=mem_norm_w[i])
        p_mk.append(kv[:, :MEM_WIDTH].reshape(bp, MEM_TOKENS, MEM_HEADS, MEM_HEAD_DIM))
        p_mv.append(kv[:, MEM_WIDTH:].reshape(bp, MEM_TOKENS, MEM_HEADS, MEM_HEAD_DIM))

        ssd_prm = _ssd_params(ssd_conv_w[i], ssd_conv_b[i], ssd_dt_bias[i], ssd_a_log[i], ssd_d[i], ssd_norm_w[i])
        y_ssd_p, ssd_p, conv_p = _ssd(proj, 0, bp, lp, ssd_zero, conv_zero, 0, ssd_prm, sb=1)
        y_ssd_s, ssd_s, conv_s = _ssd(proj, tp, bs, ls, ssd_all, conv_all, i * bs, ssd_prm, sb=8)

        s5_prm = _s5_params(s5_lam_re[i], s5_lam_im[i], s5_b_re[i], s5_b_im[i], s5_c_re[i], s5_c_im[i], s5_d[i],
                            s5_log_dt[i], s5_glu_w[i], s5_glu_b[i], s5_norm_w[i])
        u = proj[:, PROJ_U:PROJ_DT]
        u_p = jnp.pad(jnp.swapaxes(u[:tp].reshape(bp, lp, S5_WIDTH), 0, 1), ((0, 0), (0, s5_pad), (0, 0)))
        zst = jnp.zeros((SUBLANES, S5_CHANNELS), F32)
        y5_p, s5r_p, s5i_p = _s5(u_p, zst, zst, s5_prm, tc=64, nbb=SUBLANES)
        u_s = jnp.swapaxes(u[tp:].reshape(bs, ls, S5_WIDTH), 0, 1)
        y5_s, s5r_s, s5i_s = _s5(u_s, state_s5_re[i].reshape(bs, S5_CHANNELS),
                                 state_s5_im[i].reshape(bs, S5_CHANNELS), s5_prm, tc=ls, nbb=64)
        y_s5 = jnp.concatenate([jnp.swapaxes(y5_p[:, :bp], 0, 1).reshape(tp, S5_WIDTH),
                                jnp.swapaxes(y5_s, 0, 1).reshape(ts, S5_WIDTH)], axis=0)

        rw_prm = _rwkv_params(rwkv_mu[i], rwkv_w0[i], rwkv_w2[i], rwkv_a0[i], rwkv_a2[i], rwkv_g2[i],
                              rwkv_k_k[i], rwkv_k_a[i], rwkv_r_k[i])
        hr = proj[:, PROJ_HR:PROJ_U]
        hr_p = hr[:tp].reshape(bp, lp, RWKV_SHIFT_DIM)
        hr_s = hr[tp:].reshape(bs, ls, RWKV_SHIFT_DIM)
        prev = jnp.concatenate([
            jnp.concatenate([jnp.zeros((bp, 1, RWKV_SHIFT_DIM), F32), hr_p[:, :-1]], axis=1).reshape(tp, -1),
            jnp.concatenate([state_rwkv_shift[i][:, None], hr_s[:, :-1]], axis=1).reshape(ts, -1)], axis=0)
        r_, w_, k_, v_, kk_, b_, g_, bonus = _rwkv_prep(hr, prev, rw_prm)
        yp_, wkv_p = _rwkv_chunked(r_, w_, k_, v_, kk_, b_, wkv_zero, 0, 0, bp, lp)
        ys_, wkv_s = _rwkv_chunked(r_, w_, k_, v_, kk_, b_, wkv_all, i * bs, tp, bs, ls)
        y_rw = _rwkv_post((yp_, ys_), bonus, g_, rwkv_gn_w[i], rwkv_gn_b[i], rw_prm[-1])

        p_st.append((ssd_p.reshape(bp, SSD_HEADS, SSD_HEAD_DIM, SSD_STATE), conv_p, wkv_p, hr_p[:, -1],
                     s5r_p[:bp].reshape(bp, S5_GROUPS, S5_STATE), s5i_p[:bp].reshape(bp, S5_GROUPS, S5_STATE)))
        s_st.append((ssd_s.reshape(bs, SSD_HEADS, SSD_HEAD_DIM, SSD_STATE), conv_s, wkv_s, hr_s[:, -1],
                     s5r_s.reshape(bs, S5_GROUPS, S5_STATE), s5i_s.reshape(bs, S5_GROUPS, S5_STATE)))

        x = _mm([(y_ssd_p, y_ssd_s), y_rw, y_s5], w_out[i].astype(BF16), residual=x)

        q = _mm(x, wq_mem[i].astype(BF16), norm_w=norm_mem[i])
        o_p = _attend(q, 0, bp, lp, kv, kv, 0, 1, 0, lq=512, sb=1)
        o_s = _attend(q, tp, bs, ls, ck_all, cv_all, 0, 0, i * bs, lq=ls, sb=8)
        x = _mm([(o_p, o_s)], wo_mem[i].astype(BF16), residual=x)

        j = i // 2
        if i % 2 == 0:
            x = _ffn(x, norm_ffn[i], ffn_w1[j].astype(BF16), ffn_w3[j].astype(BF16), ffn_w2[j].astype(BF16))
        else:
            x = _moe(x, norm_ffn[i], moe_router_w[j], moe_router_b[j], moe_w1[j], moe_w3[j], moe_w2[j],
                     final_w=final_norm_w if i == DEPTH - 1 else None)

    y = x if DEPTH % 2 == 0 else _final_norm(x, final_norm_w)
    y_prompt = y[:tp].reshape(bp, lp, d)
    y_sample = y[tp:].reshape(bs, ls, d)

    def stk(lst, j):
        return jnp.stack([s[j] for s in lst])

    return (y_prompt, y_sample, jnp.stack(p_mk), jnp.stack(p_mv),
            stk(p_st, 0), stk(p_st, 1), stk(p_st, 2), stk(p_st, 3), stk(p_st, 4), stk(p_st, 5),
            stk(s_st, 0), stk(s_st, 1), stk(s_st, 2), stk(s_st, 3), stk(s_st, 4), stk(s_st, 5))
```

```python
import functools

import jax
import jax.numpy as jnp
from jax import lax
from jax.experimental import pallas as pl
from jax.experimental.pallas import tpu as pltpu

D_MODEL = 2048
DEPTH = 2
SSD_WIDTH = 1024
SSD_HEAD_DIM = 64
SSD_HEADS = 16
SSD_GROUPS = 2
SSD_STATE = 128
SSD_CONV = 4
SSD_CONV_DIM = 1536
SSD_CHUNK = 64
RWKV_WIDTH = 512
RWKV_HEAD_DIM = 64
RWKV_HEADS = 8
RWKV_W_LORA = 64
RWKV_A_LORA = 64
RWKV_G_LORA = 128
RWKV_SHIFT_DIM = 1792
RWKV_GN_EPS = 64e-5
S5_WIDTH = 512
S5_GROUP = 16
S5_GROUPS = 32
S5_STATE = 64
S5_CHANNELS = S5_GROUPS * S5_STATE
S5_SLABS = 4
S5_SLAB_CH = S5_CHANNELS // S5_SLABS
MEM_TOKENS = 256
MEM_HEADS = 4
MEM_HEAD_DIM = 128
MEM_WIDTH = 512
N_EXPERTS = 8
TOP_K = 2
RMS_EPS = 1e-6

F32 = jnp.float32
BF16 = jnp.bfloat16
HIGHEST = lax.Precision.HIGHEST

PROJ_Z = 0
PROJ_XBC = PROJ_Z + SSD_WIDTH
PROJ_HR = PROJ_XBC + SSD_CONV_DIM
PROJ_U = PROJ_HR + RWKV_SHIFT_DIM
PROJ_DT = PROJ_U + S5_WIDTH
LANES = 128
SUBLANES = 8
PROJ_WIDTH = PROJ_DT + 2 * LANES

VMEM_LIMIT = 56 * 1024 * 1024
ROW_TILE = 512
MM_ROW_TILE = 1024
MOE_ROW_TILE = 1024
MOE_SUB_TILE = 512


def _rms_rows(x, w):
    return x * lax.rsqrt(jnp.mean(x * x, axis=-1, keepdims=True) + RMS_EPS) * w


def _sigmoid(x):
    return 1.0 / (1.0 + jnp.exp(-x))


def _silu(x):
    return x * _sigmoid(x)


def _softplus(x):
    return jnp.maximum(x, 0.0) + jnp.log1p(jnp.exp(-jnp.abs(x)))


def _dot(a, b):
    return jnp.dot(a, b, preferred_element_type=F32)


def _dot_nt(a, b):
    return lax.dot_general(a, b, (((1,), (1,)), ((), ())), preferred_element_type=F32)


def _dot_exact(a, b):
    return jnp.dot(a, b, precision=HIGHEST, preferred_element_type=F32)


def _params(*sem):
    return pltpu.CompilerParams(dimension_semantics=sem, vmem_limit_bytes=VMEM_LIMIT)


def _split_rows_specs(pair, tm, width, grid_rank):
    n_first = pair[0].shape[0] // tm
    assert pair[0].shape[0] % tm == 0 and pair[1].shape[0] % tm == 0
    if grid_rank == 1:
        first = pl.BlockSpec((tm, width), lambda i: (jnp.minimum(i, n_first - 1), 0))
        rest = pl.BlockSpec((tm, width), lambda i: (jnp.maximum(i - n_first, 0), 0))
    else:
        first = pl.BlockSpec((tm, width), lambda i, j: (jnp.minimum(i, n_first - 1), 0))
        rest = pl.BlockSpec((tm, width), lambda i, j: (jnp.maximum(i - n_first, 0), 0))
    return n_first, [first, rest]


def _mm_kernel(*refs, parts, has_norm, has_res):
    refs = list(refs)
    a_refs = []
    for width, n_first in parts:
        a_refs.append(refs.pop(0) if n_first is None else (refs.pop(0), refs.pop(0)))
    nw_ref = refs.pop(0) if has_norm else None
    w_ref = refs.pop(0)
    res_ref = refs.pop(0) if has_res else None
    o_ref = refs.pop(0)
    abf_ref = refs.pop(0)
    i = pl.program_id(0)

    @pl.when(pl.program_id(1) == 0)
    def _():
        c0 = 0
        for (width, n_first), a_ref in zip(parts, a_refs):
            a = a_ref[...] if n_first is None else jnp.where(i < n_first, a_ref[0][...], a_ref[1][...])
            if has_norm:
                a = _rms_rows(a, nw_ref[...])
            abf_ref[:, c0:c0 + width] = a.astype(BF16)
            c0 += width

    acc = _dot(abf_ref[...], w_ref[...])
    if has_res:
        acc = acc + res_ref[...]
    o_ref[...] = acc


def _col_tile(n):
    for t in (1024, 512, 256, 128):
        if n % t == 0:
            return t
    raise ValueError(f"unsupported matmul width {n}")


def _mm(a, w, norm_w=None, residual=None):
    a_parts = a if isinstance(a, list) else [a]
    rows = lambda p: p.shape[0] if not isinstance(p, tuple) else p[0].shape[0] + p[1].shape[0]
    m = rows(a_parts[0])
    k = sum((p[0] if isinstance(p, tuple) else p).shape[1] for p in a_parts)
    n = w.shape[1]
    tm, tn = (MM_ROW_TILE if m % MM_ROW_TILE == 0 else ROW_TILE), _col_tile(n)
    assert m % tm == 0 and w.shape[0] == k
    has_norm, has_res = norm_w is not None, residual is not None
    assert not (has_norm and len(a_parts) > 1)
    in_specs, args, parts = [], [], []
    for p in a_parts:
        if isinstance(p, tuple):
            n_first, specs = _split_rows_specs(p, tm, p[0].shape[1], 2)
            in_specs += specs
            args += list(p)
            parts.append((p[0].shape[1], n_first))
        else:
            in_specs.append(pl.BlockSpec((tm, p.shape[1]), lambda i, j: (i, 0)))
            args.append(p)
            parts.append((p.shape[1], None))
    if has_norm:
        in_specs.append(pl.BlockSpec((1, k), lambda i, j: (0, 0)))
        args.append(norm_w.reshape(1, k))
    in_specs.append(pl.BlockSpec((k, tn), lambda i, j: (0, j)))
    args.append(w)
    if has_res:
        in_specs.append(pl.BlockSpec((tm, tn), lambda i, j: (i, j)))
        args.append(residual)
    return pl.pallas_call(
        functools.partial(_mm_kernel, parts=tuple(parts), has_norm=has_norm, has_res=has_res),
        out_shape=jax.ShapeDtypeStruct((m, n), F32),
        grid=(m // tm, n // tn),
        in_specs=in_specs,
        out_specs=pl.BlockSpec((tm, tn), lambda i, j: (i, j)),
        scratch_shapes=[pltpu.VMEM((tm, k), BF16)],
        compiler_params=_params("parallel", "arbitrary"),
        name="matmul",
    )(*args)


def _ffn_kernel(x_ref, nw_ref, w1_ref, w3_ref, w2_ref, o_ref, h_ref, acc_ref):
    f = pl.program_id(1)

    @pl.when(f == 0)
    def _():
        h_ref[...] = _rms_rows(x_ref[...], nw_ref[...]).astype(BF16)
        acc_ref[...] = jnp.zeros_like(acc_ref)

    h = h_ref[...]
    g = _dot(h, w1_ref[...])
    u = _dot(h, w3_ref[...])
    a = (g * jax.nn.sigmoid(g) * u).astype(BF16)
    acc_ref[...] += _dot(a, w2_ref[...])

    @pl.when(f == pl.num_programs(1) - 1)
    def _():
        o_ref[...] = x_ref[...] + acc_ref[...]


def _ffn(x, norm_w, w1, w3, w2):
    m, d = x.shape
    dff = w1.shape[1]
    tm, tf = ROW_TILE, 512
    return pl.pallas_call(
        _ffn_kernel,
        out_shape=jax.ShapeDtypeStruct((m, d), F32),
        grid=(m // tm, dff // tf),
        in_specs=[
            pl.BlockSpec((tm, d), lambda i, f: (i, 0)),
            pl.BlockSpec((1, d), lambda i, f: (0, 0)),
            pl.BlockSpec((d, tf), lambda i, f: (0, f)),
            pl.BlockSpec((d, tf), lambda i, f: (0, f)),
            pl.BlockSpec((tf, d), lambda i, f: (f, 0)),
        ],
        out_specs=pl.BlockSpec((tm, d), lambda i, f: (i, 0)),
        scratch_shapes=[pltpu.VMEM((tm, d), BF16), pltpu.VMEM((tm, d), F32)],
        compiler_params=_params("parallel", "arbitrary"),
        name="ffn_swiglu",
    )(x, norm_w.reshape(1, d), w1, w3, w2)


def _row_copy(src_hbm, src_row, dst_vmem, dst_row, sem):
    return pltpu.make_async_copy(src_hbm.at[pl.ds(src_row, 1), :], dst_vmem.at[pl.ds(dst_row, 1), :], sem)


def _moe_kernel(be_ref, nr_ref, tok_ref, h_hbm, w1_ref, w3_ref, w2_ref, o_ref, xf_ref, xb_ref, sem, *, nb, nf):
    b, f = pl.program_id(0), pl.program_id(1)
    n_sub = MOE_ROW_TILE // MOE_SUB_TILE

    def rows_needed(blk):
        return (nr_ref[blk] + MOE_SUB_TILE - 1) // MOE_SUB_TILE * MOE_SUB_TILE

    def start_rows(blk, lo, hi):
        def body(r, carry):
            _row_copy(h_hbm, tok_ref[blk * MOE_ROW_TILE + r], xf_ref, r, sem).start()
            return carry
        lax.fori_loop(lo, hi, body, 0)

    @pl.when(f == 0)
    def _():
        @pl.when(b == 0)
        def _():
            start_rows(0, 0, rows_needed(0))

        def wait(r, carry):
            _row_copy(h_hbm, 0, xf_ref, r, sem).wait()
            return carry

        lax.fori_loop(0, rows_needed(b), wait, 0)
        o_ref[...] = jnp.zeros_like(o_ref)
        for sub in range(n_sub):
            @pl.when(nr_ref[b] > sub * MOE_SUB_TILE)
            def _():
                r0 = sub * MOE_SUB_TILE
                xb_ref[r0:r0 + MOE_SUB_TILE, :] = xf_ref[r0:r0 + MOE_SUB_TILE, :].astype(BF16)

    @pl.when((f > 0) & (b + 1 < nb))
    def _():
        per_step = -(-MOE_ROW_TILE // (nf - 1))
        start_rows(b + 1, (f - 1) * per_step, jnp.minimum(f * per_step, rows_needed(b + 1)))

    for sub in range(n_sub):
        rows = slice(sub * MOE_SUB_TILE, (sub + 1) * MOE_SUB_TILE)

        @pl.when(nr_ref[b] > sub * MOE_SUB_TILE)
        def _():
            h = xb_ref[rows, :]
            g = _dot(h, w1_ref[0].astype(BF16))
            u = _dot(h, w3_ref[0].astype(BF16))
            a = (g * jax.nn.sigmoid(g) * u).astype(BF16)
            o_ref[rows, :] += _dot(a, w2_ref[0].astype(BF16))


def _moe_experts(h, buf_tok, block_e, block_rows, w1, w3, w2):
    d = h.shape[1]
    cap = buf_tok.shape[0]
    dff = w1.shape[2]
    tm, tf = MOE_ROW_TILE, 256
    nb, nf = cap // tm, dff // tf

    def w13_map(b, f, be, nr, tok):
        return (be[b], 0, jnp.where(nr[b] > 0, f, nf - 1))

    def w2_map(b, f, be, nr, tok):
        return (be[b], jnp.where(nr[b] > 0, f, nf - 1), 0)

    grid_spec = pltpu.PrefetchScalarGridSpec(
        num_scalar_prefetch=3,
        grid=(nb, nf),
        in_specs=[
            pl.BlockSpec(memory_space=pl.ANY),
            pl.BlockSpec((1, d, tf), w13_map),
            pl.BlockSpec((1, d, tf), w13_map),
            pl.BlockSpec((1, tf, d), w2_map),
        ],
        out_specs=pl.BlockSpec((tm, d), lambda b, f, be, nr, tok: (b, 0)),
        scratch_shapes=[pltpu.VMEM((tm, d), F32), pltpu.VMEM((tm, d), BF16), pltpu.SemaphoreType.DMA(())],
    )
    return pl.pallas_call(
        functools.partial(_moe_kernel, nb=nb, nf=nf),
        out_shape=jax.ShapeDtypeStruct((cap, d), F32),
        grid_spec=grid_spec,
        compiler_params=_params("arbitrary", "arbitrary"),
        name="moe_swiglu",
    )(block_e, block_rows, buf_tok, h, w1, w3, w2)


def _combine_kernel(pos_ref, x_ref, g_ref, yb_hbm, *rest, tq, final_norm):
    if final_norm:
        fw_ref, o_ref, ybuf, sem = rest
    else:
        o_ref, ybuf, sem = rest
    i = pl.program_id(0)

    def start(r, carry):
        for k in range(TOP_K):
            _row_copy(yb_hbm, pos_ref[(i * tq + r) * TOP_K + k], ybuf.at[k], r, sem).start()
        return carry

    def wait(r, carry):
        for k in range(TOP_K):
            _row_copy(yb_hbm, 0, ybuf.at[k], r, sem).wait()
        return carry

    lax.fori_loop(0, tq, start, 0)
    lax.fori_loop(0, tq, wait, 0)
    g = g_ref[...]
    y = x_ref[...] + g[:, 0:1] * ybuf[0] + g[:, 1:2] * ybuf[1]
    if final_norm:
        y = _rms_rows(y, fw_ref[...])
    o_ref[...] = y


def _moe_combine(x, yb, pos, gates, final_w=None):
    t, d = x.shape
    tq = 256
    g = jnp.pad(gates, ((0, 0), (0, LANES - TOP_K)))
    in_specs = [pl.BlockSpec((tq, d), lambda i, pos: (i, 0)), pl.BlockSpec((tq, LANES), lambda i, pos: (i, 0)),
                pl.BlockSpec(memory_space=pl.ANY)]
    args = [pos.reshape(-1), x, g, yb]
    if final_w is not None:
        in_specs.append(pl.BlockSpec((1, d), lambda i, pos: (0, 0)))
        args.append(final_w.reshape(1, d))
    grid_spec = pltpu.PrefetchScalarGridSpec(
        num_scalar_prefetch=1,
        grid=(t // tq,),
        in_specs=in_specs,
        out_specs=pl.BlockSpec((tq, d), lambda i, pos: (i, 0)),
        scratch_shapes=[pltpu.VMEM((TOP_K, tq, d), F32), pltpu.SemaphoreType.DMA(())],
    )
    return pl.pallas_call(
        functools.partial(_combine_kernel, tq=tq, final_norm=final_w is not None),
        out_shape=jax.ShapeDtypeStruct((t, d), F32),
        grid_spec=grid_spec,
        compiler_params=_params("arbitrary"),
        name="moe_combine",
    )(*args)


def _route_kernel(x_ref, nw_ref, wr_ref, br_ref, h_ref, lg_ref):
    h = _rms_rows(x_ref[...], nw_ref[...])
    h_ref[...] = h
    hh = h.astype(BF16)
    hl = (h - hh.astype(F32)).astype(BF16)
    wr = wr_ref[...]
    wh = wr.astype(BF16)
    wl = (wr - wh.astype(F32)).astype(BF16)
    lg_ref[...] = _dot(hh, wh) + _dot(hl, wh) + _dot(hh, wl) + br_ref[...]


def _norm_route(x, norm_w, w_router, b_router):
    t, d = x.shape
    tm = ROW_TILE
    wr = jnp.pad(w_router, ((0, 0), (0, LANES - N_EXPERTS)))
    br = jnp.pad(b_router, (0, LANES - N_EXPERTS)).reshape(1, LANES)
    h, lg = pl.pallas_call(
        _route_kernel,
        out_shape=(jax.ShapeDtypeStruct((t, d), F32), jax.ShapeDtypeStruct((t, LANES), F32)),
        grid=(t // tm,),
        in_specs=[pl.BlockSpec((tm, d), lambda i: (i, 0)), pl.BlockSpec((1, d), lambda i: (0, 0)),
                  pl.BlockSpec((d, LANES), lambda i: (0, 0)), pl.BlockSpec((1, LANES), lambda i: (0, 0))],
        out_specs=(pl.BlockSpec((tm, d), lambda i: (i, 0)), pl.BlockSpec((tm, LANES), lambda i: (i, 0))),
        compiler_params=_params("parallel"),
        name="moe_route",
    )(x, norm_w.reshape(1, d), wr, br)
    return h, lg[:, :N_EXPERTS]


def _final_norm_kernel(x_ref, w_ref, o_ref):
    o_ref[...] = _rms_rows(x_ref[...], w_ref[...])


def _final_norm(x, w):
    t, d = x.shape
    tm = ROW_TILE
    return pl.pallas_call(
        _final_norm_kernel,
        out_shape=jax.ShapeDtypeStruct((t, d), F32),
        grid=(t // tm,),
        in_specs=[pl.BlockSpec((tm, d), lambda i: (i, 0)), pl.BlockSpec((1, d), lambda i: (0, 0))],
        out_specs=pl.BlockSpec((tm, d), lambda i: (i, 0)),
        compiler_params=_params("parallel"),
        name="final_norm",
    )(x, w.reshape(1, d))


def _moe(x, norm_w, w_router, b_router, w1, w3, w2, final_w=None):
    t, d = x.shape
    tm = MOE_ROW_TILE
    h, logits = _norm_route(x, norm_w, w_router, b_router)
    lj, le = logits[:, None, :], logits[:, :, None]
    eidx = jnp.arange(N_EXPERTS, dtype=jnp.int32)
    beats = (lj > le) | ((lj == le) & (eidx[None, None, :] < eidx[None, :, None]))
    rank = jnp.sum(beats.astype(jnp.int32), axis=-1)
    sel = jnp.stack([rank == k for k in range(TOP_K)], axis=1)
    top_v = jnp.sum(jnp.where(sel, logits[:, None, :], 0.0), axis=-1)
    gates = jax.nn.softmax(top_v, axis=-1)
    n_slots = t * TOP_K
    oh = sel.reshape(n_slots, N_EXPERTS).astype(jnp.int32)
    counts = jnp.sum(oh, axis=0)
    padded = (counts + tm - 1) // tm * tm
    pad_ends = jnp.cumsum(padded)
    pad_starts = pad_ends - padded
    within = jnp.cumsum(oh, axis=0) - oh
    slot_pos = jnp.sum(oh * (within + pad_starts[None, :]), axis=-1)
    nb = n_slots // tm + N_EXPERTS
    cap = nb * tm
    buf_tok = jnp.zeros((cap,), jnp.int32).at[slot_pos].set(jnp.arange(n_slots, dtype=jnp.int32) // TOP_K)
    block_start = jnp.arange(nb, dtype=jnp.int32) * tm
    block_e = jnp.minimum(jnp.sum((pad_ends[None, :] <= block_start[:, None]).astype(jnp.int32), axis=-1),
                          N_EXPERTS - 1)
    block_rows = jnp.clip(counts[block_e] - (block_start - pad_starts[block_e]), 0, tm).astype(jnp.int32)
    yb = _moe_experts(h, buf_tok, block_e.astype(jnp.int32), block_rows, w1, w3, w2)
    return _moe_combine(x, yb, slot_pos.astype(jnp.int32), gates, final_w)


def _s5_kernel(u_ref, sre_ref, sim_ref, are_ref, aim_ref, bb_ref, cre_ref, cim_ref, d_ref,
               gw_ref, gb_ref, nw_ref, y_ref, ore_ref, oim_ref, xr_ref, xi_ref, st_ref, *, tc, nbb):
    c = pl.program_id(1)
    rows = tc * nbb
    u = u_ref[...].reshape(rows, S5_WIDTH)
    ub = u.astype(BF16)
    for sl in range(S5_SLABS):
        bu = _dot(ub[:, sl * LANES:(sl + 1) * LANES], bb_ref[sl])
        xr_ref[:, sl * S5_SLAB_CH:(sl + 1) * S5_SLAB_CH] = bu[:, :S5_SLAB_CH]
        xi_ref[:, sl * S5_SLAB_CH:(sl + 1) * S5_SLAB_CH] = bu[:, S5_SLAB_CH:]

    @pl.when(c == 0)
    def _():
        st_ref[0] = sre_ref[...]
        st_ref[1] = sim_ref[...]

    ar = are_ref[...]
    ai = aim_ref[...]

    def step(t, carry):
        for g in range(nbb // SUBLANES):
            r0 = pl.multiple_of(t * nbb + g * SUBLANES, SUBLANES)
            sl = slice(g * SUBLANES, (g + 1) * SUBLANES)
            pr = st_ref[0, sl, :]
            pi = st_ref[1, sl, :]
            nr = ar * pr - ai * pi + xr_ref[pl.ds(r0, SUBLANES), :]
            ni = ar * pi + ai * pr + xi_ref[pl.ds(r0, SUBLANES), :]
            xr_ref[pl.ds(r0, SUBLANES), :] = nr
            xi_ref[pl.ds(r0, SUBLANES), :] = ni
            st_ref[0, sl, :] = nr
            st_ref[1, sl, :] = ni
        return carry

    lax.fori_loop(0, tc, step, 0)

    ys = []
    for sl in range(S5_SLABS):
        ch = slice(sl * S5_SLAB_CH, (sl + 1) * S5_SLAB_CH)
        ys.append(_dot(xr_ref[:, ch].astype(BF16), cre_ref[sl]) - _dot(xi_ref[:, ch].astype(BF16), cim_ref[sl]))
    y = jnp.concatenate(ys, axis=1) + d_ref[...] * u
    gy = 0.5 * y * (1.0 + jnp.tanh(0.7978845608028654 * (y + 0.044715 * (y * y * y))))
    y = gy * _sigmoid(_dot(gy.astype(BF16), gw_ref[...]) + gb_ref[...])
    y_ref[...] = _rms_rows(y, nw_ref[...]).reshape(tc, nbb, S5_WIDTH)

    @pl.when(c == pl.num_programs(1) - 1)
    def _():
        ore_ref[...] = st_ref[0]
        oim_ref[...] = st_ref[1]


def _s5(u_tm, st_re, st_im, prm, *, tc, nbb):
    L, n, _ = u_tm.shape
    ch = S5_CHANNELS
    vec = lambda w: pl.BlockSpec((1, w), lambda s, c: (0, 0))
    mat = lambda a, b: pl.BlockSpec((a, b), lambda s, c: (0, 0))
    slab = lambda a, b: pl.BlockSpec((S5_SLABS, a, b), lambda s, c: (0, 0, 0))
    st_spec = pl.BlockSpec((nbb, ch), lambda s, c: (s, 0))
    return pl.pallas_call(
        functools.partial(_s5_kernel, tc=tc, nbb=nbb),
        out_shape=(jax.ShapeDtypeStruct((L, n, S5_WIDTH), F32),
                   jax.ShapeDtypeStruct((n, ch), F32), jax.ShapeDtypeStruct((n, ch), F32)),
        grid=(n // nbb, L // tc),
        in_specs=[pl.BlockSpec((tc, nbb, S5_WIDTH), lambda s, c: (c, s, 0)), st_spec, st_spec,
                  vec(ch), vec(ch), slab(LANES, 2 * S5_SLAB_CH), slab(S5_SLAB_CH, LANES), slab(S5_SLAB_CH, LANES),
                  vec(S5_WIDTH), mat(S5_WIDTH, S5_WIDTH), vec(S5_WIDTH), vec(S5_WIDTH)],
        out_specs=(pl.BlockSpec((tc, nbb, S5_WIDTH), lambda s, c: (c, s, 0)), st_spec, st_spec),
        scratch_shapes=[pltpu.VMEM((tc * nbb, ch), F32), pltpu.VMEM((tc * nbb, ch), F32),
                        pltpu.VMEM((2, nbb, ch), F32)],
        compiler_params=_params("parallel", "arbitrary"),
        name="s5_mixer",
    )(u_tm, st_re, st_im, *prm)


def _s5_params(lam_re, lam_im, b_re, b_im, c_re, c_im, d_skip, log_dt, glu_w, glu_b, norm_w):
    delta = jnp.exp(log_dt)[:, None]
    mag = jnp.exp(lam_re * delta)
    ab_re, ab_im = mag * jnp.cos(lam_im * delta), mag * jnp.sin(lam_im * delta)
    den = lam_re * lam_re + lam_im * lam_im
    q_re = ((ab_re - 1.0) * lam_re + ab_im * lam_im) / den
    q_im = (ab_im * lam_re - (ab_re - 1.0) * lam_im) / den
    bb_re = q_re[..., None] * b_re - q_im[..., None] * b_im
    bb_im = q_re[..., None] * b_im + q_im[..., None] * b_re
    gs = S5_GROUPS // S5_SLABS
    eye = jnp.eye(gs, dtype=F32)

    def in_blockdiag(bb):
        t = jnp.swapaxes(bb, 1, 2).reshape(S5_SLABS, gs, S5_GROUP, S5_STATE)
        return (eye[None, :, None, :, None] * t[:, :, :, None, :]).reshape(S5_SLABS, LANES, S5_SLAB_CH)

    def out_blockdiag(cc):
        t = jnp.swapaxes(cc, 1, 2).reshape(S5_SLABS, gs, S5_STATE, S5_GROUP)
        return (eye[None, :, None, :, None] * t[:, :, :, None, :]).reshape(S5_SLABS, S5_SLAB_CH, LANES).astype(BF16)

    bb = jnp.concatenate([in_blockdiag(bb_re), in_blockdiag(bb_im)], axis=2).astype(BF16)
    return (ab_re.reshape(1, S5_CHANNELS), ab_im.reshape(1, S5_CHANNELS), bb,
            out_blockdiag(c_re), out_blockdiag(c_im), d_skip.reshape(1, S5_WIDTH), glu_w.astype(BF16),
            glu_b.reshape(1, S5_WIDTH), norm_w.reshape(1, S5_WIDTH))


def _ssd_kernel(p_ref, st_ref, cs_ref, cw_ref, cb_ref, dtb_ref, a_ref, dsk_ref, nw_ref,
                y_ref, ost_ref, ocs_ref, ext_ref, win_ref, s_ref, *, q, sb):
    c = pl.program_id(1)
    last = c == pl.num_programs(1) - 1
    hd, nh, gw = SSD_HEAD_DIM, SSD_HEADS, SSD_WIDTH // SSD_GROUPS
    pad_rows = hd - q

    lane = lax.broadcasted_iota(jnp.int32, (q, LANES), 1)
    row = lax.broadcasted_iota(jnp.int32, (q, LANES), 0)
    causal2 = row >= (lane % hd)
    lane64 = lax.broadcasted_iota(jnp.int32, (hd, LANES), 1)
    tri = (lax.broadcasted_iota(jnp.int32, (q, q), 0) >= lax.broadcasted_iota(jnp.int32, (q, q), 1)).astype(F32)
    e_h = lax.broadcasted_iota(jnp.int32, (LANES, SSD_WIDTH), 0)
    e_c = lax.broadcasted_iota(jnp.int32, (LANES, SSD_WIDTH), 1)
    expand = (e_h == e_c // hd).astype(F32)
    i_s = lax.broadcasted_iota(jnp.int32, (q, SSD_WIDTH), 0)
    i_c = lax.broadcasted_iota(jnp.int32, (q, SSD_WIDTH), 1)
    eye_x = (i_s == i_c % hd).astype(F32)

    for s in range(sb):
        rs = slice(s * q, (s + 1) * q)

        @pl.when(c == 0)
        def _():
            win_ref[s, 0:5, :] = jnp.zeros((5, SSD_CONV_DIM), F32)
            win_ref[s, 5:8, :] = cs_ref[s]
            for g in range(SSD_GROUPS):
                for k in range(gw // LANES):
                    r0 = g * gw + k * LANES
                    s_ref[s, g, :, k * LANES:(k + 1) * LANES] = st_ref[s, r0:r0 + LANES, :].T

        z = p_ref[rs, PROJ_Z:PROJ_XBC]
        xbc = p_ref[rs, PROJ_XBC:PROJ_HR]
        dt = p_ref[rs, PROJ_DT:PROJ_DT + LANES]
        ext_ref[0:8, :] = win_ref[s]
        ext_ref[8:8 + q, :] = xbc
        conv = cb_ref[...]
        for j in range(SSD_CONV):
            conv = conv + cw_ref[j:j + 1, :] * ext_ref[pl.ds(5 + j, q), :]
        win_ref[s] = ext_ref[q:q + 8, :]
        xc = _silu(conv)
        xs = xc[:, :SSD_WIDTH]
        bm = xc[:, SSD_WIDTH:SSD_WIDTH + SSD_GROUPS * SSD_STATE]
        cm = xc[:, SSD_WIDTH + SSD_GROUPS * SSD_STATE:]

        step = _softplus(dt + dtb_ref[...])
        adt = step * a_ref[...]
        step_x = _dot_exact(step, expand)
        acs_x = _dot_exact(tri, _dot_exact(adt, expand))
        diag = jnp.sum(acs_x * eye_x, axis=0, keepdims=True)
        acs_last = acs_x[q - 1:q, :]
        xdt = xs * step_x
        exp_acs = jnp.exp(acs_x)
        xw = xdt * jnp.exp(acs_last - acs_x)
        dec = jnp.exp(acs_last)

        for g in range(SSD_GROUPS):
            bg = bm[:, g * SSD_STATE:(g + 1) * SSD_STATE]
            cg = cm[:, g * SSD_STATE:(g + 1) * SSD_STATE].astype(BF16)
            gl = slice(g * gw, (g + 1) * gw)
            b64 = bg if pad_rows == 0 else jnp.concatenate([bg, jnp.zeros((pad_rows, SSD_STATE), F32)], axis=0)
            cb2 = _dot_nt(cg, jnp.concatenate([b64, b64], axis=0).astype(BF16))
            sg = s_ref[s, g]
            yoff = _dot(cg, sg.astype(BF16)) * exp_acs[:, gl]
            for pr in range(gw // LANES):
                l0 = g * gw + pr * LANES
                seg = acs_x[:, l0:l0 + LANES] - diag[:, l0:l0 + LANES]
                m = cb2 * jnp.exp(jnp.where(causal2, seg, -jnp.inf))
                xd = xdt[:, l0:l0 + LANES]
                xd64 = xd if pad_rows == 0 else jnp.concatenate([xd, jnp.zeros((pad_rows, LANES), F32)], axis=0)
                rhs = jnp.concatenate([jnp.where(lane64 < hd, xd64, 0.0), jnp.where(lane64 >= hd, xd64, 0.0)], axis=0)
                ydiag = _dot(m.astype(BF16), rhs.astype(BF16))
                y_ref[rs, l0:l0 + LANES] = ydiag + yoff[:, pr * LANES:(pr + 1) * LANES]
            bpad = jnp.concatenate([bg, jnp.zeros((LANES - q, SSD_STATE), F32)], axis=0)
            xwpad = jnp.concatenate([xw[:, gl], jnp.zeros((LANES - q, gw), F32)], axis=0)
            s_ref[s, g] = dec[:, gl] * sg + _dot(bpad.T.astype(BF16), xwpad.astype(BF16))

        y = y_ref[rs, :] + dsk_ref[...] * xs
        y = y * _silu(z)
        halves = []
        for g in range(SSD_GROUPS):
            yg = y[:, g * gw:(g + 1) * gw]
            halves.append(yg * lax.rsqrt(jnp.mean(yg * yg, axis=-1, keepdims=True) + RMS_EPS))
        y_ref[rs, :] = jnp.concatenate(halves, axis=1) * nw_ref[...]

        @pl.when(last)
        def _():
            ocs_ref[s] = ext_ref[q + 5:q + 8, :]
            for g in range(SSD_GROUPS):
                for k in range(gw // LANES):
                    r0 = g * gw + k * LANES
                    ost_ref[s, r0:r0 + LANES, :] = s_ref[s, g, :, k * LANES:(k + 1) * LANES].T


def _ssd(proj, row0, nseq, L, st, cs, seq0, prm, *, sb):
    q = min(L, SSD_CHUNK)
    nchunk = L // q
    rows = sb * q
    base = row0 // rows
    sbase = seq0 // sb
    assert row0 % rows == 0 and nseq % sb == 0 and seq0 % sb == 0
    vec = lambda w: pl.BlockSpec((1, w), lambda s, c: (0, 0))
    st_spec = pl.BlockSpec((sb, SSD_WIDTH, SSD_STATE), lambda s, c: (s, 0, 0))
    cs_spec = pl.BlockSpec((sb, SSD_CONV - 1, SSD_CONV_DIM), lambda s, c: (s, 0, 0))
    st_in = pl.BlockSpec((sb, SSD_WIDTH, SSD_STATE), lambda s, c: (sbase + s, 0, 0))
    cs_in = pl.BlockSpec((sb, SSD_CONV - 1, SSD_CONV_DIM), lambda s, c: (sbase + s, 0, 0))
    return pl.pallas_call(
        functools.partial(_ssd_kernel, q=q, sb=sb),
        out_shape=(jax.ShapeDtypeStruct((nseq * L, SSD_WIDTH), F32),
                   jax.ShapeDtypeStruct((nseq, SSD_WIDTH, SSD_STATE), F32),
                   jax.ShapeDtypeStruct((nseq, SSD_CONV - 1, SSD_CONV_DIM), F32)),
        grid=(nseq // sb, nchunk),
        in_specs=[pl.BlockSpec((rows, PROJ_WIDTH), lambda s, c: (base + s * nchunk + c, 0)), st_in, cs_in,
                  pl.BlockSpec((SSD_CONV, SSD_CONV_DIM), lambda s, c: (0, 0)), vec(SSD_CONV_DIM),
                  vec(LANES), vec(LANES), vec(SSD_WIDTH), vec(SSD_WIDTH)],
        out_specs=(pl.BlockSpec((rows, SSD_WIDTH), lambda s, c: (s * nchunk + c, 0)), st_spec, cs_spec),
        scratch_shapes=[pltpu.VMEM((q + 8, SSD_CONV_DIM), F32), pltpu.VMEM((sb, 8, SSD_CONV_DIM), F32),
                        pltpu.VMEM((sb, SSD_GROUPS, SSD_STATE, SSD_WIDTH // SSD_GROUPS), F32)],
        compiler_params=_params("parallel", "arbitrary"),
        name="ssd_mixer",
    )(proj, st, cs, *prm)


def _ssd_params(conv_w, conv_b, dt_bias, a_log, d_skip, norm_w):
    pad = jnp.zeros((LANES - SSD_HEADS,), F32)
    return (conv_w, conv_b.reshape(1, SSD_CONV_DIM), jnp.concatenate([dt_bias, pad]).reshape(1, LANES),
            jnp.concatenate([-jnp.exp(a_log), pad]).reshape(1, LANES),
            jnp.repeat(d_skip, SSD_HEAD_DIM).reshape(1, SSD_WIDTH), norm_w.reshape(1, SSD_WIDTH))


def _attn_kernel(q_ref, k_ref, v_ref, o_ref, *, lq, sb):
    scale = MEM_HEAD_DIM ** -0.5
    heads = [slice(h * MEM_HEAD_DIM, (h + 1) * MEM_HEAD_DIM) for h in range(MEM_HEADS)]
    for s in range(sb):
        rq = slice(s * lq, (s + 1) * lq)
        rk = slice(s * MEM_TOKENS, (s + 1) * MEM_TOKENS)
        sc = [_dot_nt(q_ref[rq, cl].astype(BF16), k_ref[rk, cl].astype(BF16)) * scale for cl in heads]
        sc = [x - jnp.max(x, axis=-1, keepdims=True) for x in sc]
        p = [jnp.exp(x) for x in sc]
        p = [x / jnp.sum(x, axis=-1, keepdims=True) for x in p]
        for cl, x in zip(heads, p):
            o_ref[rq, cl] = _dot(x.astype(BF16), v_ref[rk, cl].astype(BF16))


def _attend(q, row0, nseq, L, k2d, v2d, kcol, vcol, seq0, *, lq, sb):
    nl = L // lq
    rows = sb * lq
    base = row0 // rows
    kbase = seq0 // sb
    assert row0 % rows == 0 and (sb == 1 or nl == 1) and seq0 % sb == 0
    return pl.pallas_call(
        functools.partial(_attn_kernel, lq=lq, sb=sb),
        out_shape=jax.ShapeDtypeStruct((nseq * L, MEM_WIDTH), F32),
        grid=(nseq // sb, nl),
        in_specs=[pl.BlockSpec((rows, MEM_WIDTH), lambda s, l: (base + s * nl + l, 0)),
                  pl.BlockSpec((sb * MEM_TOKENS, MEM_WIDTH), lambda s, l: (kbase + s, kcol)),
                  pl.BlockSpec((sb * MEM_TOKENS, MEM_WIDTH), lambda s, l: (kbase + s, vcol))],
        out_specs=pl.BlockSpec((rows, MEM_WIDTH), lambda s, l: (s * nl + l, 0)),
        compiler_params=_params("parallel", "arbitrary"),
        name="mem_attention",
    )(q, k2d, v2d)


def _head_sum(x, ones_bd):
    return _dot_exact(x, ones_bd)


def _rwkv_prep_kernel(p_ref, above_ref, first_ref, mu_ref, wl_ref, w0_ref, a0_ref, kk_ref, ka_ref, rk_ref, ones_ref,
                      r_ref, w_ref, k_ref, v_ref, n_ref, b_ref, g_ref, bo_ref, *, tm, n_first, len_first, len_rest):
    W = RWKV_WIDTH
    i = pl.program_id(0)
    h = p_ref[:, PROJ_HR:PROJ_U]
    row = lax.broadcasted_iota(jnp.int32, h.shape, 0)
    prev = jnp.where(row == 0, above_ref[SUBLANES - 1:SUBLANES, PROJ_HR:PROJ_U], pltpu.roll(h, 1, axis=0))
    in_rest = i >= n_first
    start_rest = (row % len_rest == 0).astype(jnp.int32)
    start_first = ((i * tm + row) % len_first == 0).astype(jnp.int32)
    is_start = jnp.where(in_rest, start_rest, start_first) == 1
    prev = jnp.where(is_start, jnp.where(in_rest, first_ref[...], 0.0), prev)
    hs = h + (prev - h) * mu_ref[...]
    r, k, v = hs[:, :W], hs[:, W:2 * W], hs[:, 2 * W:3 * W]
    lo = hs[:, 3 * W:]
    lane = lax.broadcasted_iota(jnp.int32, lo.shape, 1)
    act = jnp.where(lane < RWKV_W_LORA, jnp.tanh(lo),
                    jnp.where(lane < RWKV_W_LORA + RWKV_A_LORA, lo, _sigmoid(lo)))
    lora = _dot(act.astype(BF16), wl_ref[...])
    w_log = -_softplus(-(w0_ref[...] + lora[:, :W])) - 0.5
    a = _sigmoid(a0_ref[...] + lora[:, W:2 * W])
    ones_bd = ones_ref[...]
    kk = k * kk_ref[...]
    kk = kk / jnp.maximum(jnp.sqrt(_head_sum(kk * kk, ones_bd)), 1e-12)
    k2 = k * (1.0 + (a - 1.0) * ka_ref[...])
    r_ref[...] = r
    w_ref[...] = -jnp.exp(w_log)
    k_ref[...] = k2
    v_ref[...] = v
    n_ref[...] = kk
    b_ref[...] = kk * a
    g_ref[...] = lora[:, 2 * W:]
    bo_ref[...] = _head_sum(r * k2 * rk_ref[...], ones_bd) * v


def _rwkv_prep(proj, rows_first, len_first, len_rest, shift_rest, prm):
    m = proj.shape[0]
    tm, W = 256, RWKV_WIDTH
    assert rows_first % tm == 0 and (m - rows_first) % tm == 0 and tm % len_rest == 0 and len_first % tm == 0
    n_first = rows_first // tm
    mu, wl, w0, a0, k_k, k_a, r_k, ones_bd = prm
    first_rows = jnp.repeat(shift_rest, len_rest, axis=0)
    vec = lambda w: pl.BlockSpec((1, w), lambda i: (0, 0))
    row = lambda w: pl.BlockSpec((tm, w), lambda i: (i, 0))
    return pl.pallas_call(
        functools.partial(_rwkv_prep_kernel, tm=tm, n_first=n_first, len_first=len_first, len_rest=len_rest),
        out_shape=tuple(jax.ShapeDtypeStruct((m, W), F32) for _ in range(8)),
        grid=(m // tm,),
        in_specs=[row(PROJ_WIDTH),
                  pl.BlockSpec((SUBLANES, PROJ_WIDTH), lambda i: (jnp.maximum(i * (tm // SUBLANES) - 1, 0), 0)),
                  pl.BlockSpec((tm, RWKV_SHIFT_DIM), lambda i: (jnp.maximum(i - n_first, 0), 0)),
                  vec(RWKV_SHIFT_DIM),
                  pl.BlockSpec(wl.shape, lambda i: (0, 0)), vec(W), vec(W), vec(W), vec(W), vec(W),
                  pl.BlockSpec((W, W), lambda i: (0, 0))],
        out_specs=tuple(row(W) for _ in range(8)),
        compiler_params=_params("parallel"),
        name="rwkv_prep",
    )(proj, proj, first_rows, mu, wl, w0, a0, k_k, k_a, r_k, ones_bd)


def _rwkv_post_kernel(yp_ref, ys_ref, bo_ref, g_ref, gw_ref, gb_ref, ones_ref, o_ref, *, n_first):
    ones_bd = ones_ref[...]
    y = jnp.where(pl.program_id(0) < n_first, yp_ref[...], ys_ref[...])
    inv = 1.0 / RWKV_HEAD_DIM
    d = y - _head_sum(y, ones_bd) * inv
    var = _head_sum(d * d, ones_bd) * inv
    yn = d * lax.rsqrt(var + RWKV_GN_EPS) * gw_ref[...] + gb_ref[...]
    o_ref[...] = (yn + bo_ref[...]) * g_ref[...]


def _rwkv_post(y_pair, bonus, g, gn_w, gn_b, ones_bd):
    m, W = bonus.shape
    tm = ROW_TILE
    vec = pl.BlockSpec((1, W), lambda i: (0, 0))
    row = pl.BlockSpec((tm, W), lambda i: (i, 0))
    n_first, y_specs = _split_rows_specs(y_pair, tm, W, 1)
    return pl.pallas_call(
        functools.partial(_rwkv_post_kernel, n_first=n_first),
        out_shape=jax.ShapeDtypeStruct((m, W), F32),
        grid=(m // tm,),
        in_specs=y_specs + [row, row, vec, vec, pl.BlockSpec((W, W), lambda i: (0, 0))],
        out_specs=row,
        compiler_params=_params("parallel"),
        name="rwkv_post",
    )(*y_pair, bonus, g, gn_w.reshape(1, W), gn_b.reshape(1, W), ones_bd)


def _rwkv_params(mu, w0, w2, a0, a2, g2, k_k, k_a, r_k):
    W = RWKV_WIDTH
    nl = RWKV_W_LORA + RWKV_A_LORA + RWKV_G_LORA
    wl = jnp.zeros((nl, 3 * W), F32)
    wl = wl.at[:RWKV_W_LORA, :W].set(w2)
    wl = wl.at[RWKV_W_LORA:RWKV_W_LORA + RWKV_A_LORA, W:2 * W].set(a2)
    wl = wl.at[RWKV_W_LORA + RWKV_A_LORA:, 2 * W:].set(g2)
    head = jnp.arange(W) // RWKV_HEAD_DIM
    ones_bd = (head[:, None] == head[None, :]).astype(F32)
    v = lambda t: t.reshape(1, -1)
    return (v(mu), wl.astype(BF16), v(w0), v(a0), v(k_k), v(k_a), v(r_k), ones_bd)


def _bdot(a, b):
    return jnp.dot(a.astype(BF16), b.astype(BF16), preferred_element_type=F32)


def _bdot_nt(a, b):
    return lax.dot_general(a.astype(BF16), b.astype(BF16), (((1,), (1,)), ((), ())), preferred_element_type=F32)


def _split(x):
    hi = x.astype(BF16)
    return hi, (x - hi.astype(F32)).astype(BF16)


def _dot3(a, b):
    ah, al = _split(a)
    bh, bl = _split(b)
    return (jnp.dot(ah, bh, preferred_element_type=F32) + jnp.dot(al, bh, preferred_element_type=F32)
            + jnp.dot(ah, bl, preferred_element_type=F32))


def _dot3_nt(a, b):
    ah, al = _split(a)
    bh, bl = _split(b)
    dn = (((1,), (1,)), ((), ()))
    return (lax.dot_general(ah, bh, dn, preferred_element_type=F32)
            + lax.dot_general(al, bh, dn, preferred_element_type=F32)
            + lax.dot_general(ah, bl, dn, preferred_element_type=F32))


RWKV_GROUP = 4
RWKV_GW = RWKV_GROUP * RWKV_HEAD_DIM
RWKV_NG = RWKV_HEADS // RWKV_GROUP


def _rwkv_chunk_kernel(r_ref, ls_ref, k_ref, v_ref, n_ref, b_ref, s0_ref, y_ref, sf_ref, s_ref, *, C, ns):
    c = pl.program_id(1)
    G, GW, HD = RWKV_GROUP, RWKV_GW, RWKV_HEAD_DIM
    RI = ns * C
    R = G * RI
    SB = G * C
    groups = range(RWKV_NG)

    @pl.when(c == 0)
    def _():
        for q in groups:
            for s in range(ns):
                s_ref[q, s] = jnp.concatenate([s0_ref[s, G * q + h] for h in range(G)], axis=1)

    ri = lax.broadcasted_iota(jnp.int32, (2 * RI, RI), 0)
    ci = lax.broadcasted_iota(jnp.int32, (2 * RI, RI), 1)
    same = ((ri % RI) // C) == (ci // C)
    cum = (same & ((ri >= RI) | (ri >= ci))).astype(BF16)
    ls_all = ls_ref[...]
    l1 = ls_all.astype(BF16)
    l2f = ls_all - l1.astype(F32)
    l2 = l2f.astype(BF16)
    l3 = (l2f - l2.astype(F32)).astype(BF16)
    lw2 = (jnp.dot(cum, l1, preferred_element_type=F32) + jnp.dot(cum, l2, preferred_element_type=F32)
           + jnp.dot(cum, l3, preferred_element_type=F32))
    lw_all, lwl_all = lw2[:RI], lw2[RI:]

    lane_in = lax.broadcasted_iota(jnp.int32, (C, GW), 1) // HD
    row = lax.broadcasted_iota(jnp.int32, (R, R), 0)
    col = lax.broadcasted_iota(jnp.int32, (R, R), 1)
    ent = (row // C) == (col // C)
    strict = ent & (row > col)
    incl = ent & (row >= col)
    eye = row == col
    eye_f = eye.astype(F32)
    own = (lax.broadcasted_iota(jnp.int32, (SB, GW), 0) // C) == (lax.broadcasted_iota(jnp.int32, (SB, GW), 1) // HD)
    rows_r = lax.broadcasted_iota(jnp.int32, (R, GW), 0)
    rows_2r = lax.broadcasted_iota(jnp.int32, (2 * R, GW), 0)

    def stack(x):
        parts = []
        for s in range(ns):
            xs = x[s * C:(s + 1) * C]
            parts += [jnp.where(lane_in == h, xs, 0.0) for h in range(G)]
        return jnp.concatenate(parts, axis=0)

    def dup(x):
        parts = []
        for s in range(ns):
            parts += [x[s * C:(s + 1) * C]] * G
        return jnp.concatenate(parts, axis=0)

    st = []
    for q in groups:
        gl = slice(q * GW, (q + 1) * GW)
        lw, lwl, ls = lw_all[:, gl], lwl_all[:, gl], ls_all[:, gl]
        w_inv = jnp.exp(-lw)
        w_rest = jnp.exp(lwl - lw)
        kk, bb = k_ref[:, gl], b_ref[:, gl]
        st.append(dict(
            n_st=stack(n_ref[:, gl] * jnp.exp(lw - ls)), r_st=stack(r_ref[:, gl] * jnp.exp(lw)),
            v_st=stack(v_ref[:, gl]), bh_st=stack(bb * w_rest), kh_st=stack(kk * w_rest),
            b_dup=dup(bb * w_inv), k_dup=dup(kk * w_inv), w_c=jnp.exp(lwl)))
    for d in st:
        nr = jnp.concatenate([d['n_st'], d['r_st']], axis=0)
        gb = _bdot_nt(nr, d['b_dup'])
        gk = _bdot_nt(nr, d['k_dup'])
        d['a_nb'] = jnp.where(strict, gb[:R], 0.0)
        d['a_rb'] = jnp.where(incl, gb[R:], 0.0)
        d['a_nk'] = jnp.where(strict, gk[:R], 0.0)
        d['a_rk'] = jnp.where(incl, gk[R:], 0.0)
        d['t'] = eye_f - d['a_nb']
        d['p'] = d['a_nb']
    for _ in range(C.bit_length() - 2):
        for d in st:
            d['p'] = _bdot(d['p'], d['p'])
        for d in st:
            d['t'] = _bdot(d['t'], eye_f + d['p'])
    for d in st:
        d['p1'] = _bdot(d['t'], d['n_st'])
        d['z'] = _bdot(d['a_nk'], d['v_st'])
    for d in st:
        d['p2'] = _bdot(d['t'], d['z'])
    for q, d in enumerate(st):
        p1, p2 = d['p1'], d['p2']
        p1_t = p1.T
        lt = jnp.concatenate([d['v_st'].T, -p2.T], axis=1)
        kb = jnp.concatenate([d['kh_st'], d['bh_st']], axis=0)
        sa_parts, rs_parts = [], []
        for s in range(ns):
            rsl = slice(s * SB, (s + 1) * SB)
            S = s_ref[q, s]
            ss = jnp.concatenate([S] * G, axis=0)
            xr = _dot3_nt(jnp.concatenate([p1[rsl], d['r_st'][rsl]], axis=0), ss)
            sa_parts.append(-jnp.where(own, xr[:SB], 0.0) - p2[rsl])
            rs_parts.append(jnp.where(own, xr[SB:], 0.0))
            if ns == 1:
                bh_s, kb_s = d['bh_st'], kb
            else:
                bh_s = jnp.where((rows_r // SB) == s, d['bh_st'], 0.0)
                kb_s = jnp.where(((rows_2r % R) // SB) == s, kb, 0.0)
            m_bd = jnp.where(eye, d['w_c'][s * C:s * C + 1, :], 0.0) - _bdot(p1_t, bh_s)
            nf = _bdot(lt, kb_s)
            fold = nf[:HD]
            for h in range(1, G):
                fold = fold + nf[h * HD:(h + 1) * HD]
            s_ref[q, s] = _dot3(S, m_bd) + fold
        sa_st = jnp.concatenate(sa_parts, axis=0) if ns > 1 else sa_parts[0]
        rs_st = jnp.concatenate(rs_parts, axis=0) if ns > 1 else rs_parts[0]
        y_st = rs_st + _bdot(jnp.concatenate([d['a_rb'], d['a_rk']], axis=1),
                             jnp.concatenate([sa_st, d['v_st']], axis=0))
        for s in range(ns):
            y = y_st[s * SB:s * SB + C]
            for h in range(1, G):
                y = y + y_st[s * SB + h * C:s * SB + (h + 1) * C]
            y_ref[s * C:(s + 1) * C, q * GW:(q + 1) * GW] = y

    @pl.when(c == pl.num_programs(1) - 1)
    def _():
        for q in groups:
            for s in range(ns):
                S = s_ref[q, s]
                for h in range(G):
                    sf_ref[s, G * q + h] = S[:, h * HD:(h + 1) * HD]


def _rwkv_chunked(r, ls, k, v, kk, b, state, seq0, row0, nseq, L):
    HD, W = RWKV_HEAD_DIM, RWKV_WIDTH
    C = min(L, HD)
    ns = HD // C
    nt = L // C
    rows = ns * C
    base = row0 // rows
    sbase = seq0 // ns
    assert row0 % rows == 0 and (ns == 1 or nt == 1) and nseq % ns == 0 and seq0 % ns == 0
    row_spec = pl.BlockSpec((rows, W), lambda s, c: (base + s * nt + c, 0))
    st_spec = pl.BlockSpec((ns, RWKV_HEADS, HD, HD), lambda s, c: (s, 0, 0, 0))
    st_in = pl.BlockSpec((ns, RWKV_HEADS, HD, HD), lambda s, c: (sbase + s, 0, 0, 0))
    return pl.pallas_call(
        functools.partial(_rwkv_chunk_kernel, C=C, ns=ns),
        out_shape=(jax.ShapeDtypeStruct((nseq * L, W), F32), jax.ShapeDtypeStruct((nseq, RWKV_HEADS, HD, HD), F32)),
        grid=(nseq // ns, nt),
        in_specs=[row_spec] * 6 + [st_in],
        out_specs=(pl.BlockSpec((rows, W), lambda s, c: (s * nt + c, 0)), st_spec),
        scratch_shapes=[pltpu.VMEM((RWKV_NG, ns, HD, RWKV_GW), F32)],
        compiler_params=_params("parallel", "arbitrary"),
        name="rwkv_chunked",
    )(r, ls, k, v, kk, b, state)


def kernel(x_prompt, x_sample, mem_prompt, cache_mem_k, cache_mem_v, state_ssd, state_ssd_conv, state_rwkv, state_rwkv_shift, state_s5_re, state_s5_im, norm_mix, w_in, ssd_conv_w, ssd_conv_b, ssd_dt_bias, ssd_a_log, ssd_d, ssd_norm_w, rwkv_mu, rwkv_w0, rwkv_w2, rwkv_a0, rwkv_a2, rwkv_g2, rwkv_k_k, rwkv_k_a, rwkv_r_k, rwkv_gn_w, rwkv_gn_b, s5_lam_re, s5_lam_im, s5_b_re, s5_b_im, s5_c_re, s5_c_im, s5_d, s5_log_dt, s5_glu_w, s5_glu_b, s5_norm_w, w_out, norm_mem, mem_norm_w, wq_mem, wk_mem, wv_mem, wo_mem, norm_ffn, ffn_w1, ffn_w3, ffn_w2, moe_router_w, moe_router_b, moe_w1, moe_w3, moe_w2, final_norm_w):
    bp, lp, d = x_prompt.shape
    bs, ls, _ = x_sample.shape
    tp, ts = bp * lp, bs * ls
    x = jnp.concatenate([x_prompt.reshape(tp, d), x_sample.reshape(ts, d)], axis=0)
    mem_rows = mem_prompt.reshape(bp * MEM_TOKENS, d)
    s5_pad = SUBLANES - bp
    ssd_all = state_ssd.reshape(DEPTH * bs, SSD_WIDTH, SSD_STATE)
    conv_all = state_ssd_conv.reshape(DEPTH * bs, SSD_CONV - 1, SSD_CONV_DIM)
    wkv_all = state_rwkv.reshape(DEPTH * bs, RWKV_HEADS, RWKV_HEAD_DIM, RWKV_HEAD_DIM)
    ck_all = cache_mem_k.reshape(DEPTH * bs * MEM_TOKENS, MEM_WIDTH)
    cv_all = cache_mem_v.reshape(DEPTH * bs * MEM_TOKENS, MEM_WIDTH)
    ssd_zero = jnp.zeros((bp, SSD_WIDTH, SSD_STATE), F32)
    conv_zero = jnp.zeros((bp, SSD_CONV - 1, SSD_CONV_DIM), F32)
    wkv_zero = jnp.zeros((bp, RWKV_HEADS, RWKV_HEAD_DIM, RWKV_HEAD_DIM), F32)

    p_mk, p_mv, p_st, s_st = [], [], [], []
    for i in range(DEPTH):
        c0 = SSD_WIDTH
        c1 = c0 + SSD_CONV_DIM
        c2 = c1 + SSD_HEADS
        wi = w_in[i]
        w_in_packed = jnp.concatenate(
            [wi[:, :c1], wi[:, c2:], wi[:, c1:c2], jnp.zeros((d, PROJ_WIDTH - PROJ_DT - SSD_HEADS), F32)],
            axis=1).astype(BF16)
        proj = _mm(x, w_in_packed, norm_w=norm_mix[i])

        wkv = jnp.concatenate([wk_mem[i], wv_mem[i]], axis=1).astype(BF16)
        kv = _mm(mem_rows, wkv, norm_w=mem_norm_w[i])
        p_mk.append(kv[:, :MEM_WIDTH].reshape(bp, MEM_TOKENS, MEM_HEADS, MEM_HEAD_DIM))
        p_mv.append(kv[:, MEM_WIDTH:].reshape(bp, MEM_TOKENS, MEM_HEADS, MEM_HEAD_DIM))

        ssd_prm = _ssd_params(ssd_conv_w[i], ssd_conv_b[i], ssd_dt_bias[i], ssd_a_log[i], ssd_d[i], ssd_norm_w[i])
        y_ssd_p, ssd_p, conv_p = _ssd(proj, 0, bp, lp, ssd_zero, conv_zero, 0, ssd_prm, sb=1)
        y_ssd_s, ssd_s, conv_s = _ssd(proj, tp, bs, ls, ssd_all, conv_all, i * bs, ssd_prm, sb=8)

        s5_prm = _s5_params(s5_lam_re[i], s5_lam_im[i], s5_b_re[i], s5_b_im[i], s5_c_re[i], s5_c_im[i], s5_d[i],
                            s5_log_dt[i], s5_glu_w[i], s5_glu_b[i], s5_norm_w[i])
        u = proj[:, PROJ_U:PROJ_DT]
        u_p = jnp.pad(jnp.swapaxes(u[:tp].reshape(bp, lp, S5_WIDTH), 0, 1), ((0, 0), (0, s5_pad), (0, 0)))
        zst = jnp.zeros((SUBLANES, S5_CHANNELS), F32)
        y5_p, s5r_p, s5i_p = _s5(u_p, zst, zst, s5_prm, tc=64, nbb=SUBLANES)
        u_s = jnp.swapaxes(u[tp:].reshape(bs, ls, S5_WIDTH), 0, 1)
        y5_s, s5r_s, s5i_s = _s5(u_s, state_s5_re[i].reshape(bs, S5_CHANNELS),
                                 state_s5_im[i].reshape(bs, S5_CHANNELS), s5_prm, tc=ls, nbb=64)
        y_s5 = jnp.concatenate([jnp.swapaxes(y5_p[:, :bp], 0, 1).reshape(tp, S5_WIDTH),
                                jnp.swapaxes(y5_s, 0, 1).reshape(ts, S5_WIDTH)], axis=0)

        rw_prm = _rwkv_params(rwkv_mu[i], rwkv_w0[i], rwkv_w2[i], rwkv_a0[i], rwkv_a2[i], rwkv_g2[i],
                              rwkv_k_k[i], rwkv_k_a[i], rwkv_r_k[i])
        r_, w_, k_, v_, kk_, b_, g_, bonus = _rwkv_prep(proj, tp, lp, ls, state_rwkv_shift[i], rw_prm)
        shift_p = proj[:tp].reshape(bp, lp, PROJ_WIDTH)[:, -1, PROJ_HR:PROJ_U]
        shift_s = proj[tp:].reshape(bs, ls, PROJ_WIDTH)[:, -1, PROJ_HR:PROJ_U]
        yp_, wkv_p = _rwkv_chunked(r_, w_, k_, v_, kk_, b_, wkv_zero, 0, 0, bp, lp)
        ys_, wkv_s = _rwkv_chunked(r_, w_, k_, v_, kk_, b_, wkv_all, i * bs, tp, bs, ls)
        y_rw = _rwkv_post((yp_, ys_), bonus, g_, rwkv_gn_w[i], rwkv_gn_b[i], rw_prm[-1])

        p_st.append((ssd_p.reshape(bp, SSD_HEADS, SSD_HEAD_DIM, SSD_STATE), conv_p, wkv_p, shift_p,
                     s5r_p[:bp].reshape(bp, S5_GROUPS, S5_STATE), s5i_p[:bp].reshape(bp, S5_GROUPS, S5_STATE)))
        s_st.append((ssd_s.reshape(bs, SSD_HEADS, SSD_HEAD_DIM, SSD_STATE), conv_s, wkv_s, shift_s,
                     s5r_s.reshape(bs, S5_GROUPS, S5_STATE), s5i_s.reshape(bs, S5_GROUPS, S5_STATE)))

        x = _mm([(y_ssd_p, y_ssd_s), y_rw, y_s5], w_out[i].astype(BF16), residual=x)

        q = _mm(x, wq_mem[i].astype(BF16), norm_w=norm_mem[i])
        o_p = _attend(q, 0, bp, lp, kv, kv, 0, 1, 0, lq=512, sb=1)
        o_s = _attend(q, tp, bs, ls, ck_all, cv_all, 0, 0, i * bs, lq=ls, sb=8)
        x = _mm([(o_p, o_s)], wo_mem[i].astype(BF16), residual=x)

        j = i // 2
        if i % 2 == 0:
            x = _ffn(x, norm_ffn[i], ffn_w1[j].astype(BF16), ffn_w3[j].astype(BF16), ffn_w2[j].astype(BF16))
        else:
            x = _moe(x, norm_ffn[i], moe_router_w[j], moe_router_b[j], moe_w1[j], moe_w3[j], moe_w2[j],
                     final_w=final_norm_w if i == DEPTH - 1 else None)

    y = x if DEPTH % 2 == 0 else _final_norm(x, final_norm_w)
    y_prompt = y[:tp].reshape(bp, lp, d)
    y_sample = y[tp:].reshape(bs, ls, d)

    def stk(lst, j):
        return jnp.stack([s[j] for s in lst])

    return (y_prompt, y_sample, jnp.stack(p_mk), jnp.stack(p_mv),
            stk(p_st, 0), stk(p_st, 1), stk(p_st, 2), stk(p_st, 3), stk(p_st, 4), stk(p_st, 5),
            stk(s_st, 0), stk(s_st, 1), stk(s_st, 2), stk(s_st, 3), stk(s_st, 4), stk(s_st, 5))
```

```python
import functools

import jax
import jax.numpy as jnp
from jax import lax
from jax.experimental import pallas as pl
from jax.experimental.pallas import tpu as pltpu

D_MODEL = 2048
DEPTH = 2
SSD_WIDTH = 1024
SSD_HEAD_DIM = 64
SSD_HEADS = 16
SSD_GROUPS = 2
SSD_STATE = 128
SSD_CONV = 4
SSD_CONV_DIM = 1536
SSD_CHUNK = 64
RWKV_WIDTH = 512
RWKV_HEAD_DIM = 64
RWKV_HEADS = 8
RWKV_W_LORA = 64
RWKV_A_LORA = 64
RWKV_G_LORA = 128
RWKV_SHIFT_DIM = 1792
RWKV_GN_EPS = 64e-5
S5_WIDTH = 512
S5_GROUP = 16
S5_GROUPS = 32
S5_STATE = 64
S5_CHANNELS = S5_GROUPS * S5_STATE
S5_SLABS = 4
S5_SLAB_CH = S5_CHANNELS // S5_SLABS
MEM_TOKENS = 256
MEM_HEADS = 4
MEM_HEAD_DIM = 128
MEM_WIDTH = 512
N_EXPERTS = 8
TOP_K = 2
RMS_EPS = 1e-6

F32 = jnp.float32
BF16 = jnp.bfloat16
HIGHEST = lax.Precision.HIGHEST

PROJ_Z = 0
PROJ_XBC = PROJ_Z + SSD_WIDTH
PROJ_HR = PROJ_XBC + SSD_CONV_DIM
PROJ_U = PROJ_HR + RWKV_SHIFT_DIM
PROJ_DT = PROJ_U + S5_WIDTH
LANES = 128
SUBLANES = 8
PROJ_WIDTH = PROJ_DT + 2 * LANES

VMEM_LIMIT = 56 * 1024 * 1024
ROW_TILE = 512
MM_ROW_TILE = 1024
MOE_ROW_TILE = 1024
MOE_SUB_TILE = 512


def _rms_rows(x, w):
    return x * lax.rsqrt(jnp.mean(x * x, axis=-1, keepdims=True) + RMS_EPS) * w


def _sigmoid(x):
    return 1.0 / (1.0 + jnp.exp(-x))


def _silu(x):
    return x * _sigmoid(x)


def _softplus(x):
    return jnp.maximum(x, 0.0) + jnp.log1p(jnp.exp(-jnp.abs(x)))


def _dot(a, b):
    return jnp.dot(a, b, preferred_element_type=F32)


def _dot_nt(a, b):
    return lax.dot_general(a, b, (((1,), (1,)), ((), ())), preferred_element_type=F32)


def _dot_exact(a, b):
    return jnp.dot(a, b, precision=HIGHEST, preferred_element_type=F32)


def _params(*sem):
    return pltpu.CompilerParams(dimension_semantics=sem, vmem_limit_bytes=VMEM_LIMIT)


def _split_rows_specs(pair, tm, width, grid_rank):
    n_first = pair[0].shape[0] // tm
    assert pair[0].shape[0] % tm == 0 and pair[1].shape[0] % tm == 0
    if grid_rank == 1:
        first = pl.BlockSpec((tm, width), lambda i: (jnp.minimum(i, n_first - 1), 0))
        rest = pl.BlockSpec((tm, width), lambda i: (jnp.maximum(i - n_first, 0), 0))
    else:
        first = pl.BlockSpec((tm, width), lambda i, j: (jnp.minimum(i, n_first - 1), 0))
        rest = pl.BlockSpec((tm, width), lambda i, j: (jnp.maximum(i - n_first, 0), 0))
    return n_first, [first, rest]


def _mm_kernel(*refs, parts, has_norm, has_res):
    refs = list(refs)
    a_refs = []
    for width, n_first in parts:
        a_refs.append(refs.pop(0) if n_first is None else (refs.pop(0), refs.pop(0)))
    nw_ref = refs.pop(0) if has_norm else None
    w_ref = refs.pop(0)
    res_ref = refs.pop(0) if has_res else None
    o_ref = refs.pop(0)
    abf_ref = refs.pop(0)
    i = pl.program_id(0)

    @pl.when(pl.program_id(1) == 0)
    def _():
        c0 = 0
        for (width, n_first), a_ref in zip(parts, a_refs):
            a = a_ref[...] if n_first is None else jnp.where(i < n_first, a_ref[0][...], a_ref[1][...])
            if has_norm:
                a = _rms_rows(a, nw_ref[...])
            abf_ref[:, c0:c0 + width] = a.astype(BF16)
            c0 += width

    acc = _dot(abf_ref[...], w_ref[...])
    if has_res:
        acc = acc + res_ref[...]
    o_ref[...] = acc


def _col_tile(n):
    for t in (1024, 512, 256, 128):
        if n % t == 0:
            return t
    raise ValueError(f"unsupported matmul width {n}")


def _mm(a, w, norm_w=None, residual=None):
    a_parts = a if isinstance(a, list) else [a]
    rows = lambda p: p.shape[0] if not isinstance(p, tuple) else p[0].shape[0] + p[1].shape[0]
    m = rows(a_parts[0])
    k = sum((p[0] if isinstance(p, tuple) else p).shape[1] for p in a_parts)
    n = w.shape[1]
    tm, tn = (MM_ROW_TILE if m % MM_ROW_TILE == 0 else ROW_TILE), _col_tile(n)
    assert m % tm == 0 and w.shape[0] == k
    has_norm, has_res = norm_w is not None, residual is not None
    assert not (has_norm and len(a_parts) > 1)
    in_specs, args, parts = [], [], []
    for p in a_parts:
        if isinstance(p, tuple):
            n_first, specs = _split_rows_specs(p, tm, p[0].shape[1], 2)
            in_specs += specs
            args += list(p)
            parts.append((p[0].shape[1], n_first))
        else:
            in_specs.append(pl.BlockSpec((tm, p.shape[1]), lambda i, j: (i, 0)))
            args.append(p)
            parts.append((p.shape[1], None))
    if has_norm:
        in_specs.append(pl.BlockSpec((1, k), lambda i, j: (0, 0)))
        args.append(norm_w.reshape(1, k))
    in_specs.append(pl.BlockSpec((k, tn), lambda i, j: (0, j)))
    args.append(w)
    if has_res:
        in_specs.append(pl.BlockSpec((tm, tn), lambda i, j: (i, j)))
        args.append(residual)
    return pl.pallas_call(
        functools.partial(_mm_kernel, parts=tuple(parts), has_norm=has_norm, has_res=has_res),
        out_shape=jax.ShapeDtypeStruct((m, n), F32),
        grid=(m // tm, n // tn),
        in_specs=in_specs,
        out_specs=pl.BlockSpec((tm, tn), lambda i, j: (i, j)),
        scratch_shapes=[pltpu.VMEM((tm, k), BF16)],
        compiler_params=_params("parallel", "arbitrary"),
        name="matmul",
    )(*args)


def _ffn_kernel(x_ref, nw_ref, w1_ref, w3_ref, w2_ref, o_ref, h_ref, acc_ref):
    f = pl.program_id(1)

    @pl.when(f == 0)
    def _():
        h_ref[...] = _rms_rows(x_ref[...], nw_ref[...]).astype(BF16)
        acc_ref[...] = jnp.zeros_like(acc_ref)

    h = h_ref[...]
    g = _dot(h, w1_ref[...])
    u = _dot(h, w3_ref[...])
    a = (g * jax.nn.sigmoid(g) * u).astype(BF16)
    acc_ref[...] += _dot(a, w2_ref[...])

    @pl.when(f == pl.num_programs(1) - 1)
    def _():
        o_ref[...] = x_ref[...] + acc_ref[...]


def _ffn(x, norm_w, w1, w3, w2):
    m, d = x.shape
    dff = w1.shape[1]
    tm, tf = ROW_TILE, 512
    return pl.pallas_call(
        _ffn_kernel,
        out_shape=jax.ShapeDtypeStruct((m, d), F32),
        grid=(m // tm, dff // tf),
        in_specs=[
            pl.BlockSpec((tm, d), lambda i, f: (i, 0)),
            pl.BlockSpec((1, d), lambda i, f: (0, 0)),
            pl.BlockSpec((d, tf), lambda i, f: (0, f)),
            pl.BlockSpec((d, tf), lambda i, f: (0, f)),
            pl.BlockSpec((tf, d), lambda i, f: (f, 0)),
        ],
        out_specs=pl.BlockSpec((tm, d), lambda i, f: (i, 0)),
        scratch_shapes=[pltpu.VMEM((tm, d), BF16), pltpu.VMEM((tm, d), F32)],
        compiler_params=_params("parallel", "arbitrary"),
        name="ffn_swiglu",
    )(x, norm_w.reshape(1, d), w1, w3, w2)


def _row_copy(src_hbm, src_row, dst_vmem, dst_row, sem):
    return pltpu.make_async_copy(src_hbm.at[pl.ds(src_row, 1), :], dst_vmem.at[pl.ds(dst_row, 1), :], sem)


def _moe_kernel(be_ref, nr_ref, tok_ref, h_hbm, w1_ref, w3_ref, w2_ref, o_ref, xf_ref, xb_ref, sem, *, nb, nf):
    b, f = pl.program_id(0), pl.program_id(1)
    n_sub = MOE_ROW_TILE // MOE_SUB_TILE

    def rows_needed(blk):
        return (nr_ref[blk] + MOE_SUB_TILE - 1) // MOE_SUB_TILE * MOE_SUB_TILE

    def start_rows(blk, lo, hi):
        def body(r, carry):
            _row_copy(h_hbm, tok_ref[blk * MOE_ROW_TILE + r], xf_ref, r, sem).start()
            return carry
        lax.fori_loop(lo, hi, body, 0)

    @pl.when(f == 0)
    def _():
        @pl.when(b == 0)
        def _():
            start_rows(0, 0, rows_needed(0))

        def wait(r, carry):
            _row_copy(h_hbm, 0, xf_ref, r, sem).wait()
            return carry

        lax.fori_loop(0, rows_needed(b), wait, 0)
        o_ref[...] = jnp.zeros_like(o_ref)
        for sub in range(n_sub):
            @pl.when(nr_ref[b] > sub * MOE_SUB_TILE)
            def _():
                r0 = sub * MOE_SUB_TILE
                xb_ref[r0:r0 + MOE_SUB_TILE, :] = xf_ref[r0:r0 + MOE_SUB_TILE, :].astype(BF16)

    @pl.when((f > 0) & (b + 1 < nb))
    def _():
        per_step = -(-MOE_ROW_TILE // (nf - 1))
        start_rows(b + 1, (f - 1) * per_step, jnp.minimum(f * per_step, rows_needed(b + 1)))

    for sub in range(n_sub):
        rows = slice(sub * MOE_SUB_TILE, (sub + 1) * MOE_SUB_TILE)

        @pl.when(nr_ref[b] > sub * MOE_SUB_TILE)
        def _():
            h = xb_ref[rows, :]
            g = _dot(h, w1_ref[0].astype(BF16))
            u = _dot(h, w3_ref[0].astype(BF16))
            a = (g * jax.nn.sigmoid(g) * u).astype(BF16)
            o_ref[rows, :] += _dot(a, w2_ref[0].astype(BF16))


def _moe_experts(h, buf_tok, block_e, block_rows, w1, w3, w2):
    d = h.shape[1]
    cap = buf_tok.shape[0]
    dff = w1.shape[2]
    tm, tf = MOE_ROW_TILE, 512
    nb, nf = cap // tm, dff // tf

    def w13_map(b, f, be, nr, tok):
        return (be[b], 0, jnp.where(nr[b] > 0, f, nf - 1))

    def w2_map(b, f, be, nr, tok):
        return (be[b], jnp.where(nr[b] > 0, f, nf - 1), 0)

    grid_spec = pltpu.PrefetchScalarGridSpec(
        num_scalar_prefetch=3,
        grid=(nb, nf),
        in_specs=[
            pl.BlockSpec(memory_space=pl.ANY),
            pl.BlockSpec((1, d, tf), w13_map),
            pl.BlockSpec((1, d, tf), w13_map),
            pl.BlockSpec((1, tf, d), w2_map),
        ],
        out_specs=pl.BlockSpec((tm, d), lambda b, f, be, nr, tok: (b, 0)),
        scratch_shapes=[pltpu.VMEM((tm, d), F32), pltpu.VMEM((tm, d), BF16), pltpu.SemaphoreType.DMA(())],
    )
    return pl.pallas_call(
        functools.partial(_moe_kernel, nb=nb, nf=nf),
        out_shape=jax.ShapeDtypeStruct((cap, d), F32),
        grid_spec=grid_spec,
        compiler_params=_params("arbitrary", "arbitrary"),
        name="moe_swiglu",
    )(block_e, block_rows, buf_tok, h, w1, w3, w2)


def _combine_kernel(pos_ref, x_ref, g_ref, yb_hbm, *rest, tq, final_norm):
    if final_norm:
        fw_ref, o_ref, ybuf, sem = rest
    else:
        o_ref, ybuf, sem = rest
    i = pl.program_id(0)

    def start(r, carry):
        for k in range(TOP_K):
            _row_copy(yb_hbm, pos_ref[(i * tq + r) * TOP_K + k], ybuf.at[k], r, sem).start()
        return carry

    def wait(r, carry):
        for k in range(TOP_K):
            _row_copy(yb_hbm, 0, ybuf.at[k], r, sem).wait()
        return carry

    lax.fori_loop(0, tq, start, 0)
    lax.fori_loop(0, tq, wait, 0)
    g = g_ref[...]
    y = x_ref[...] + g[:, 0:1] * ybuf[0] + g[:, 1:2] * ybuf[1]
    if final_norm:
        y = _rms_rows(y, fw_ref[...])
    o_ref[...] = y


def _moe_combine(x, yb, pos, gates, final_w=None):
    t, d = x.shape
    tq = 256
    g = jnp.pad(gates, ((0, 0), (0, LANES - TOP_K)))
    in_specs = [pl.BlockSpec((tq, d), lambda i, pos: (i, 0)), pl.BlockSpec((tq, LANES), lambda i, pos: (i, 0)),
                pl.BlockSpec(memory_space=pl.ANY)]
    args = [pos.reshape(-1), x, g, yb]
    if final_w is not None:
        in_specs.append(pl.BlockSpec((1, d), lambda i, pos: (0, 0)))
        args.append(final_w.reshape(1, d))
    grid_spec = pltpu.PrefetchScalarGridSpec(
        num_scalar_prefetch=1,
        grid=(t // tq,),
        in_specs=in_specs,
        out_specs=pl.BlockSpec((tq, d), lambda i, pos: (i, 0)),
        scratch_shapes=[pltpu.VMEM((TOP_K, tq, d), F32), pltpu.SemaphoreType.DMA(())],
    )
    return pl.pallas_call(
        functools.partial(_combine_kernel, tq=tq, final_norm=final_w is not None),
        out_shape=jax.ShapeDtypeStruct((t, d), F32),
        grid_spec=grid_spec,
        compiler_params=_params("arbitrary"),
        name="moe_combine",
    )(*args)


def _route_kernel(x_ref, nw_ref, wr_ref, br_ref, h_ref, lg_ref):
    h = _rms_rows(x_ref[...], nw_ref[...])
    h_ref[...] = h
    hh = h.astype(BF16)
    hl = (h - hh.astype(F32)).astype(BF16)
    wr = wr_ref[...]
    wh = wr.astype(BF16)
    wl = (wr - wh.astype(F32)).astype(BF16)
    lg_ref[...] = _dot(hh, wh) + _dot(hl, wh) + _dot(hh, wl) + br_ref[...]


def _norm_route(x, norm_w, w_router, b_router):
    t, d = x.shape
    tm = ROW_TILE
    wr = jnp.pad(w_router, ((0, 0), (0, LANES - N_EXPERTS)))
    br = jnp.pad(b_router, (0, LANES - N_EXPERTS)).reshape(1, LANES)
    h, lg = pl.pallas_call(
        _route_kernel,
        out_shape=(jax.ShapeDtypeStruct((t, d), F32), jax.ShapeDtypeStruct((t, LANES), F32)),
        grid=(t // tm,),
        in_specs=[pl.BlockSpec((tm, d), lambda i: (i, 0)), pl.BlockSpec((1, d), lambda i: (0, 0)),
                  pl.BlockSpec((d, LANES), lambda i: (0, 0)), pl.BlockSpec((1, LANES), lambda i: (0, 0))],
        out_specs=(pl.BlockSpec((tm, d), lambda i: (i, 0)), pl.BlockSpec((tm, LANES), lambda i: (i, 0))),
        compiler_params=_params("parallel"),
        name="moe_route",
    )(x, norm_w.reshape(1, d), wr, br)
    return h, lg[:, :N_EXPERTS]


def _final_norm_kernel(x_ref, w_ref, o_ref):
    o_ref[...] = _rms_rows(x_ref[...], w_ref[...])


def _final_norm(x, w):
    t, d = x.shape
    tm = ROW_TILE
    return pl.pallas_call(
        _final_norm_kernel,
        out_shape=jax.ShapeDtypeStruct((t, d), F32),
        grid=(t // tm,),
        in_specs=[pl.BlockSpec((tm, d), lambda i: (i, 0)), pl.BlockSpec((1, d), lambda i: (0, 0))],
        out_specs=pl.BlockSpec((tm, d), lambda i: (i, 0)),
        compiler_params=_params("parallel"),
        name="final_norm",
    )(x, w.reshape(1, d))


def _moe(x, norm_w, w_router, b_router, w1, w3, w2, final_w=None):
    t, d = x.shape
    tm = MOE_ROW_TILE
    h, logits = _norm_route(x, norm_w, w_router, b_router)
    lj, le = logits[:, None, :], logits[:, :, None]
    eidx = jnp.arange(N_EXPERTS, dtype=jnp.int32)
    beats = (lj > le) | ((lj == le) & (eidx[None, None, :] < eidx[None, :, None]))
    rank = jnp.sum(beats.astype(jnp.int32), axis=-1)
    sel = jnp.stack([rank == k for k in range(TOP_K)], axis=1)
    top_v = jnp.sum(jnp.where(sel, logits[:, None, :], 0.0), axis=-1)
    gates = jax.nn.softmax(top_v, axis=-1)
    n_slots = t * TOP_K
    oh = sel.reshape(n_slots, N_EXPERTS).astype(jnp.int32)
    counts = jnp.sum(oh, axis=0)
    padded = (counts + tm - 1) // tm * tm
    pad_ends = jnp.cumsum(padded)
    pad_starts = pad_ends - padded
    within = jnp.cumsum(oh, axis=0) - oh
    slot_pos = jnp.sum(oh * (within + pad_starts[None, :]), axis=-1)
    nb = n_slots // tm + N_EXPERTS
    cap = nb * tm
    buf_tok = jnp.zeros((cap,), jnp.int32).at[slot_pos].set(jnp.arange(n_slots, dtype=jnp.int32) // TOP_K)
    block_start = jnp.arange(nb, dtype=jnp.int32) * tm
    block_e = jnp.minimum(jnp.sum((pad_ends[None, :] <= block_start[:, None]).astype(jnp.int32), axis=-1),
                          N_EXPERTS - 1)
    block_rows = jnp.clip(counts[block_e] - (block_start - pad_starts[block_e]), 0, tm).astype(jnp.int32)
    yb = _moe_experts(h, buf_tok, block_e.astype(jnp.int32), block_rows, w1, w3, w2)
    return _moe_combine(x, yb, slot_pos.astype(jnp.int32), gates, final_w)


def _s5_kernel(u_ref, sre_ref, sim_ref, are_ref, aim_ref, bb_ref, cre_ref, cim_ref, d_ref,
               gw_ref, gb_ref, nw_ref, y_ref, ore_ref, oim_ref, xr_ref, xi_ref, st_ref, *, tc, nbb):
    c = pl.program_id(1)
    rows = tc * nbb
    u = u_ref[...].reshape(rows, S5_WIDTH)
    ub = u.astype(BF16)
    for sl in range(S5_SLABS):
        bu = _dot(ub[:, sl * LANES:(sl + 1) * LANES], bb_ref[sl])
        xr_ref[:, sl * S5_SLAB_CH:(sl + 1) * S5_SLAB_CH] = bu[:, :S5_SLAB_CH]
        xi_ref[:, sl * S5_SLAB_CH:(sl + 1) * S5_SLAB_CH] = bu[:, S5_SLAB_CH:]

    @pl.when(c == 0)
    def _():
        st_ref[0] = sre_ref[...]
        st_ref[1] = sim_ref[...]

    ar = are_ref[...]
    ai = aim_ref[...]

    def step(t, carry):
        for g in range(nbb // SUBLANES):
            r0 = pl.multiple_of(t * nbb + g * SUBLANES, SUBLANES)
            sl = slice(g * SUBLANES, (g + 1) * SUBLANES)
            pr = st_ref[0, sl, :]
            pi = st_ref[1, sl, :]
            nr = ar * pr - ai * pi + xr_ref[pl.ds(r0, SUBLANES), :]
            ni = ar * pi + ai * pr + xi_ref[pl.ds(r0, SUBLANES), :]
            xr_ref[pl.ds(r0, SUBLANES), :] = nr
            xi_ref[pl.ds(r0, SUBLANES), :] = ni
            st_ref[0, sl, :] = nr
            st_ref[1, sl, :] = ni
        return carry

    lax.fori_loop(0, tc, step, 0)

    ys = []
    for sl in range(S5_SLABS):
        ch = slice(sl * S5_SLAB_CH, (sl + 1) * S5_SLAB_CH)
        ys.append(_dot(xr_ref[:, ch].astype(BF16), cre_ref[sl]) - _dot(xi_ref[:, ch].astype(BF16), cim_ref[sl]))
    y = jnp.concatenate(ys, axis=1) + d_ref[...] * u
    gy = 0.5 * y * (1.0 + jnp.tanh(0.7978845608028654 * (y + 0.044715 * (y * y * y))))
    y = gy * _sigmoid(_dot(gy.astype(BF16), gw_ref[...]) + gb_ref[...])
    y_ref[...] = _rms_rows(y, nw_ref[...]).reshape(tc, nbb, S5_WIDTH)

    @pl.when(c == pl.num_programs(1) - 1)
    def _():
        ore_ref[...] = st_ref[0]
        oim_ref[...] = st_ref[1]


def _s5(u_tm, st_re, st_im, prm, *, tc, nbb):
    L, n, _ = u_tm.shape
    ch = S5_CHANNELS
    vec = lambda w: pl.BlockSpec((1, w), lambda s, c: (0, 0))
    mat = lambda a, b: pl.BlockSpec((a, b), lambda s, c: (0, 0))
    slab = lambda a, b: pl.BlockSpec((S5_SLABS, a, b), lambda s, c: (0, 0, 0))
    st_spec = pl.BlockSpec((nbb, ch), lambda s, c: (s, 0))
    return pl.pallas_call(
        functools.partial(_s5_kernel, tc=tc, nbb=nbb),
        out_shape=(jax.ShapeDtypeStruct((L, n, S5_WIDTH), F32),
                   jax.ShapeDtypeStruct((n, ch), F32), jax.ShapeDtypeStruct((n, ch), F32)),
        grid=(n // nbb, L // tc),
        in_specs=[pl.BlockSpec((tc, nbb, S5_WIDTH), lambda s, c: (c, s, 0)), st_spec, st_spec,
                  vec(ch), vec(ch), slab(LANES, 2 * S5_SLAB_CH), slab(S5_SLAB_CH, LANES), slab(S5_SLAB_CH, LANES),
                  vec(S5_WIDTH), mat(S5_WIDTH, S5_WIDTH), vec(S5_WIDTH), vec(S5_WIDTH)],
        out_specs=(pl.BlockSpec((tc, nbb, S5_WIDTH), lambda s, c: (c, s, 0)), st_spec, st_spec),
        scratch_shapes=[pltpu.VMEM((tc * nbb, ch), F32), pltpu.VMEM((tc * nbb, ch), F32),
                        pltpu.VMEM((2, nbb, ch), F32)],
        compiler_params=_params("parallel", "arbitrary"),
        name="s5_mixer",
    )(u_tm, st_re, st_im, *prm)


def _s5_params(lam_re, lam_im, b_re, b_im, c_re, c_im, d_skip, log_dt, glu_w, glu_b, norm_w):
    delta = jnp.exp(log_dt)[:, None]
    mag = jnp.exp(lam_re * delta)
    ab_re, ab_im = mag * jnp.cos(lam_im * delta), mag * jnp.sin(lam_im * delta)
    den = lam_re * lam_re + lam_im * lam_im
    q_re = ((ab_re - 1.0) * lam_re + ab_im * lam_im) / den
    q_im = (ab_im * lam_re - (ab_re - 1.0) * lam_im) / den
    bb_re = q_re[..., None] * b_re - q_im[..., None] * b_im
    bb_im = q_re[..., None] * b_im + q_im[..., None] * b_re
    gs = S5_GROUPS // S5_SLABS
    eye = jnp.eye(gs, dtype=F32)

    def in_blockdiag(bb):
        t = jnp.swapaxes(bb, 1, 2).reshape(S5_SLABS, gs, S5_GROUP, S5_STATE)
        return (eye[None, :, None, :, None] * t[:, :, :, None, :]).reshape(S5_SLABS, LANES, S5_SLAB_CH)

    def out_blockdiag(cc):
        t = jnp.swapaxes(cc, 1, 2).reshape(S5_SLABS, gs, S5_STATE, S5_GROUP)
        return (eye[None, :, None, :, None] * t[:, :, :, None, :]).reshape(S5_SLABS, S5_SLAB_CH, LANES).astype(BF16)

    bb = jnp.concatenate([in_blockdiag(bb_re), in_blockdiag(bb_im)], axis=2).astype(BF16)
    return (ab_re.reshape(1, S5_CHANNELS), ab_im.reshape(1, S5_CHANNELS), bb,
            out_blockdiag(c_re), out_blockdiag(c_im), d_skip.reshape(1, S5_WIDTH), glu_w.astype(BF16),
            glu_b.reshape(1, S5_WIDTH), norm_w.reshape(1, S5_WIDTH))


def _ssd_kernel(p_ref, st_ref, cs_ref, cw_ref, cb_ref, dtb_ref, a_ref, dsk_ref, nw_ref,
                y_ref, ost_ref, ocs_ref, ext_ref, win_ref, s_ref, *, q, sb):
    c = pl.program_id(1)
    last = c == pl.num_programs(1) - 1
    hd, nh, gw = SSD_HEAD_DIM, SSD_HEADS, SSD_WIDTH // SSD_GROUPS
    pad_rows = hd - q

    lane = lax.broadcasted_iota(jnp.int32, (q, LANES), 1)
    row = lax.broadcasted_iota(jnp.int32, (q, LANES), 0)
    causal2 = row >= (lane % hd)
    lane64 = lax.broadcasted_iota(jnp.int32, (hd, LANES), 1)
    tri = (lax.broadcasted_iota(jnp.int32, (q, q), 0) >= lax.broadcasted_iota(jnp.int32, (q, q), 1)).astype(F32)
    e_h = lax.broadcasted_iota(jnp.int32, (LANES, SSD_WIDTH), 0)
    e_c = lax.broadcasted_iota(jnp.int32, (LANES, SSD_WIDTH), 1)
    expand = (e_h == e_c // hd).astype(F32)
    i_s = lax.broadcasted_iota(jnp.int32, (q, SSD_WIDTH), 0)
    i_c = lax.broadcasted_iota(jnp.int32, (q, SSD_WIDTH), 1)
    eye_x = (i_s == i_c % hd).astype(F32)

    for s in range(sb):
        rs = slice(s * q, (s + 1) * q)

        @pl.when(c == 0)
        def _():
            win_ref[s, 0:5, :] = jnp.zeros((5, SSD_CONV_DIM), F32)
            win_ref[s, 5:8, :] = cs_ref[s]
            for g in range(SSD_GROUPS):
                for k in range(gw // LANES):
                    r0 = g * gw + k * LANES
                    s_ref[s, g, :, k * LANES:(k + 1) * LANES] = st_ref[s, r0:r0 + LANES, :].T

        z = p_ref[rs, PROJ_Z:PROJ_XBC]
        xbc = p_ref[rs, PROJ_XBC:PROJ_HR]
        dt = p_ref[rs, PROJ_DT:PROJ_DT + LANES]
        ext_ref[0:8, :] = win_ref[s]
        ext_ref[8:8 + q, :] = xbc
        conv = cb_ref[...]
        for j in range(SSD_CONV):
            conv = conv + cw_ref[j:j + 1, :] * ext_ref[pl.ds(5 + j, q), :]
        win_ref[s] = ext_ref[q:q + 8, :]
        xc = _silu(conv)
        xs = xc[:, :SSD_WIDTH]
        bm = xc[:, SSD_WIDTH:SSD_WIDTH + SSD_GROUPS * SSD_STATE]
        cm = xc[:, SSD_WIDTH + SSD_GROUPS * SSD_STATE:]

        step = _softplus(dt + dtb_ref[...])
        adt = step * a_ref[...]
        step_x = _dot_exact(step, expand)
        acs_x = _dot_exact(tri, _dot_exact(adt, expand))
        diag = jnp.sum(acs_x * eye_x, axis=0, keepdims=True)
        acs_last = acs_x[q - 1:q, :]
        xdt = xs * step_x
        exp_acs = jnp.exp(acs_x)
        xw = xdt * jnp.exp(acs_last - acs_x)
        dec = jnp.exp(acs_last)

        for g in range(SSD_GROUPS):
            bg = bm[:, g * SSD_STATE:(g + 1) * SSD_STATE]
            cg = cm[:, g * SSD_STATE:(g + 1) * SSD_STATE].astype(BF16)
            gl = slice(g * gw, (g + 1) * gw)
            b64 = bg if pad_rows == 0 else jnp.concatenate([bg, jnp.zeros((pad_rows, SSD_STATE), F32)], axis=0)
            cb2 = _dot_nt(cg, jnp.concatenate([b64, b64], axis=0).astype(BF16))
            sg = s_ref[s, g]
            yoff = _dot(cg, sg.astype(BF16)) * exp_acs[:, gl]
            for pr in range(gw // LANES):
                l0 = g * gw + pr * LANES
                seg = acs_x[:, l0:l0 + LANES] - diag[:, l0:l0 + LANES]
                m = cb2 * jnp.exp(jnp.where(causal2, seg, -jnp.inf))
                xd = xdt[:, l0:l0 + LANES]
                xd64 = xd if pad_rows == 0 else jnp.concatenate([xd, jnp.zeros((pad_rows, LANES), F32)], axis=0)
                rhs = jnp.concatenate([jnp.where(lane64 < hd, xd64, 0.0), jnp.where(lane64 >= hd, xd64, 0.0)], axis=0)
                ydiag = _dot(m.astype(BF16), rhs.astype(BF16))
                y_ref[rs, l0:l0 + LANES] = ydiag + yoff[:, pr * LANES:(pr + 1) * LANES]
            bpad = jnp.concatenate([bg, jnp.zeros((LANES - q, SSD_STATE), F32)], axis=0)
            xwpad = jnp.concatenate([xw[:, gl], jnp.zeros((LANES - q, gw), F32)], axis=0)
            s_ref[s, g] = dec[:, gl] * sg + _dot(bpad.T.astype(BF16), xwpad.astype(BF16))

        y = y_ref[rs, :] + dsk_ref[...] * xs
        y = y * _silu(z)
        halves = []
        for g in range(SSD_GROUPS):
            yg = y[:, g * gw:(g + 1) * gw]
            halves.append(yg * lax.rsqrt(jnp.mean(yg * yg, axis=-1, keepdims=True) + RMS_EPS))
        y_ref[rs, :] = jnp.concatenate(halves, axis=1) * nw_ref[...]

        @pl.when(last)
        def _():
            ocs_ref[s] = ext_ref[q + 5:q + 8, :]
            for g in range(SSD_GROUPS):
                for k in range(gw // LANES):
                    r0 = g * gw + k * LANES
                    ost_ref[s, r0:r0 + LANES, :] = s_ref[s, g, :, k * LANES:(k + 1) * LANES].T


def _ssd(proj, row0, nseq, L, st, cs, seq0, prm, *, sb):
    q = min(L, SSD_CHUNK)
    nchunk = L // q
    rows = sb * q
    base = row0 // rows
    sbase = seq0 // sb
    assert row0 % rows == 0 and nseq % sb == 0 and seq0 % sb == 0
    vec = lambda w: pl.BlockSpec((1, w), lambda s, c: (0, 0))
    st_spec = pl.BlockSpec((sb, SSD_WIDTH, SSD_STATE), lambda s, c: (s, 0, 0))
    cs_spec = pl.BlockSpec((sb, SSD_CONV - 1, SSD_CONV_DIM), lambda s, c: (s, 0, 0))
    st_in = pl.BlockSpec((sb, SSD_WIDTH, SSD_STATE), lambda s, c: (sbase + s, 0, 0))
    cs_in = pl.BlockSpec((sb, SSD_CONV - 1, SSD_CONV_DIM), lambda s, c: (sbase + s, 0, 0))
    return pl.pallas_call(
        functools.partial(_ssd_kernel, q=q, sb=sb),
        out_shape=(jax.ShapeDtypeStruct((nseq * L, SSD_WIDTH), F32),
                   jax.ShapeDtypeStruct((nseq, SSD_WIDTH, SSD_STATE), F32),
                   jax.ShapeDtypeStruct((nseq, SSD_CONV - 1, SSD_CONV_DIM), F32)),
        grid=(nseq // sb, nchunk),
        in_specs=[pl.BlockSpec((rows, PROJ_WIDTH), lambda s, c: (base + s * nchunk + c, 0)), st_in, cs_in,
                  pl.BlockSpec((SSD_CONV, SSD_CONV_DIM), lambda s, c: (0, 0)), vec(SSD_CONV_DIM),
                  vec(LANES), vec(LANES), vec(SSD_WIDTH), vec(SSD_WIDTH)],
        out_specs=(pl.BlockSpec((rows, SSD_WIDTH), lambda s, c: (s * nchunk + c, 0)), st_spec, cs_spec),
        scratch_shapes=[pltpu.VMEM((q + 8, SSD_CONV_DIM), F32), pltpu.VMEM((sb, 8, SSD_CONV_DIM), F32),
                        pltpu.VMEM((sb, SSD_GROUPS, SSD_STATE, SSD_WIDTH // SSD_GROUPS), F32)],
        compiler_params=_params("parallel", "arbitrary"),
        name="ssd_mixer",
    )(proj, st, cs, *prm)


def _ssd_params(conv_w, conv_b, dt_bias, a_log, d_skip, norm_w):
    pad = jnp.zeros((LANES - SSD_HEADS,), F32)
    return (conv_w, conv_b.reshape(1, SSD_CONV_DIM), jnp.concatenate([dt_bias, pad]).reshape(1, LANES),
            jnp.concatenate([-jnp.exp(a_log), pad]).reshape(1, LANES),
            jnp.repeat(d_skip, SSD_HEAD_DIM).reshape(1, SSD_WIDTH), norm_w.reshape(1, SSD_WIDTH))


def _attn_kernel(q_ref, k_ref, v_ref, o_ref, *, lq, sb):
    scale = MEM_HEAD_DIM ** -0.5
    heads = [slice(h * MEM_HEAD_DIM, (h + 1) * MEM_HEAD_DIM) for h in range(MEM_HEADS)]
    for s in range(sb):
        rq = slice(s * lq, (s + 1) * lq)
        rk = slice(s * MEM_TOKENS, (s + 1) * MEM_TOKENS)
        sc = [_dot_nt(q_ref[rq, cl].astype(BF16), k_ref[rk, cl].astype(BF16)) * scale for cl in heads]
        sc = [x - jnp.max(x, axis=-1, keepdims=True) for x in sc]
        p = [jnp.exp(x) for x in sc]
        p = [x / jnp.sum(x, axis=-1, keepdims=True) for x in p]
        for cl, x in zip(heads, p):
            o_ref[rq, cl] = _dot(x.astype(BF16), v_ref[rk, cl].astype(BF16))


def _attend(q, row0, nseq, L, k2d, v2d, kcol, vcol, seq0, *, lq, sb):
    nl = L // lq
    rows = sb * lq
    base = row0 // rows
    kbase = seq0 // sb
    assert row0 % rows == 0 and (sb == 1 or nl == 1) and seq0 % sb == 0
    return pl.pallas_call(
        functools.partial(_attn_kernel, lq=lq, sb=sb),
        out_shape=jax.ShapeDtypeStruct((nseq * L, MEM_WIDTH), F32),
        grid=(nseq // sb, nl),
        in_specs=[pl.BlockSpec((rows, MEM_WIDTH), lambda s, l: (base + s * nl + l, 0)),
                  pl.BlockSpec((sb * MEM_TOKENS, MEM_WIDTH), lambda s, l: (kbase + s, kcol)),
                  pl.BlockSpec((sb * MEM_TOKENS, MEM_WIDTH), lambda s, l: (kbase + s, vcol))],
        out_specs=pl.BlockSpec((rows, MEM_WIDTH), lambda s, l: (s * nl + l, 0)),
        compiler_params=_params("parallel", "arbitrary"),
        name="mem_attention",
    )(q, k2d, v2d)


def _head_sum(x, ones_bd):
    return _dot_exact(x, ones_bd)


def _rwkv_prep_kernel(p_ref, above_ref, first_ref, mu_ref, wl_ref, w0_ref, a0_ref, kk_ref, ka_ref, rk_ref, ones_ref,
                      r_ref, w_ref, k_ref, v_ref, n_ref, b_ref, g_ref, bo_ref, *, tm, n_first, len_first, len_rest):
    W = RWKV_WIDTH
    i = pl.program_id(0)
    h = p_ref[:, PROJ_HR:PROJ_U]
    row = lax.broadcasted_iota(jnp.int32, h.shape, 0)
    prev = jnp.where(row == 0, above_ref[SUBLANES - 1:SUBLANES, PROJ_HR:PROJ_U], pltpu.roll(h, 1, axis=0))
    in_rest = i >= n_first
    start_rest = (row % len_rest == 0).astype(jnp.int32)
    start_first = ((i * tm + row) % len_first == 0).astype(jnp.int32)
    is_start = jnp.where(in_rest, start_rest, start_first) == 1
    prev = jnp.where(is_start, jnp.where(in_rest, first_ref[...], 0.0), prev)
    hs = h + (prev - h) * mu_ref[...]
    r, k, v = hs[:, :W], hs[:, W:2 * W], hs[:, 2 * W:3 * W]
    lo = hs[:, 3 * W:]
    lane = lax.broadcasted_iota(jnp.int32, lo.shape, 1)
    act = jnp.where(lane < RWKV_W_LORA, jnp.tanh(lo),
                    jnp.where(lane < RWKV_W_LORA + RWKV_A_LORA, lo, _sigmoid(lo)))
    lora = _dot(act.astype(BF16), wl_ref[...])
    w_log = -_softplus(-(w0_ref[...] + lora[:, :W])) - 0.5
    a = _sigmoid(a0_ref[...] + lora[:, W:2 * W])
    ones_bd = ones_ref[...]
    kk = k * kk_ref[...]
    kk = kk / jnp.maximum(jnp.sqrt(_head_sum(kk * kk, ones_bd)), 1e-12)
    k2 = k * (1.0 + (a - 1.0) * ka_ref[...])
    r_ref[...] = r
    w_ref[...] = -jnp.exp(w_log)
    k_ref[...] = k2
    v_ref[...] = v
    n_ref[...] = kk
    b_ref[...] = kk * a
    g_ref[...] = lora[:, 2 * W:]
    bo_ref[...] = _head_sum(r * k2 * rk_ref[...], ones_bd) * v


def _rwkv_prep(proj, rows_first, len_first, len_rest, shift_rest, prm):
    m = proj.shape[0]
    tm, W = 256, RWKV_WIDTH
    assert rows_first % tm == 0 and (m - rows_first) % tm == 0 and tm % len_rest == 0 and len_first % tm == 0
    n_first = rows_first // tm
    mu, wl, w0, a0, k_k, k_a, r_k, ones_bd = prm
    first_rows = jnp.repeat(shift_rest, len_rest, axis=0)
    vec = lambda w: pl.BlockSpec((1, w), lambda i: (0, 0))
    row = lambda w: pl.BlockSpec((tm, w), lambda i: (i, 0))
    return pl.pallas_call(
        functools.partial(_rwkv_prep_kernel, tm=tm, n_first=n_first, len_first=len_first, len_rest=len_rest),
        out_shape=tuple(jax.ShapeDtypeStruct((m, W), F32) for _ in range(8)),
        grid=(m // tm,),
        in_specs=[row(PROJ_WIDTH),
                  pl.BlockSpec((SUBLANES, PROJ_WIDTH), lambda i: (jnp.maximum(i * (tm // SUBLANES) - 1, 0), 0)),
                  pl.BlockSpec((tm, RWKV_SHIFT_DIM), lambda i: (jnp.maximum(i - n_first, 0), 0)),
                  vec(RWKV_SHIFT_DIM),
                  pl.BlockSpec(wl.shape, lambda i: (0, 0)), vec(W), vec(W), vec(W), vec(W), vec(W),
                  pl.BlockSpec((W, W), lambda i: (0, 0))],
        out_specs=tuple(row(W) for _ in range(8)),
        compiler_params=_params("parallel"),
        name="rwkv_prep",
    )(proj, proj, first_rows, mu, wl, w0, a0, k_k, k_a, r_k, ones_bd)


def _rwkv_post_kernel(yp_ref, ys_ref, bo_ref, g_ref, gw_ref, gb_ref, ones_ref, o_ref, *, n_first):
    ones_bd = ones_ref[...]
    y = jnp.where(pl.program_id(0) < n_first, yp_ref[...], ys_ref[...])
    inv = 1.0 / RWKV_HEAD_DIM
    d = y - _head_sum(y, ones_bd) * inv
    var = _head_sum(d * d, ones_bd) * inv
    yn = d * lax.rsqrt(var + RWKV_GN_EPS) * gw_ref[...] + gb_ref[...]
    o_ref[...] = (yn + bo_ref[...]) * g_ref[...]


def _rwkv_post(y_pair, bonus, g, gn_w, gn_b, ones_bd):
    m, W = bonus.shape
    tm = ROW_TILE
    vec = pl.BlockSpec((1, W), lambda i: (0, 0))
    row = pl.BlockSpec((tm, W), lambda i: (i, 0))
    n_first, y_specs = _split_rows_specs(y_pair, tm, W, 1)
    return pl.pallas_call(
        functools.partial(_rwkv_post_kernel, n_first=n_first),
        out_shape=jax.ShapeDtypeStruct((m, W), F32),
        grid=(m // tm,),
        in_specs=y_specs + [row, row, vec, vec, pl.BlockSpec((W, W), lambda i: (0, 0))],
        out_specs=row,
        compiler_params=_params("parallel"),
        name="rwkv_post",
    )(*y_pair, bonus, g, gn_w.reshape(1, W), gn_b.reshape(1, W), ones_bd)


def _rwkv_params(mu, w0, w2, a0, a2, g2, k_k, k_a, r_k):
    W = RWKV_WIDTH
    nl = RWKV_W_LORA + RWKV_A_LORA + RWKV_G_LORA
    wl = jnp.zeros((nl, 3 * W), F32)
    wl = wl.at[:RWKV_W_LORA, :W].set(w2)
    wl = wl.at[RWKV_W_LORA:RWKV_W_LORA + RWKV_A_LORA, W:2 * W].set(a2)
    wl = wl.at[RWKV_W_LORA + RWKV_A_LORA:, 2 * W:].set(g2)
    head = jnp.arange(W) // RWKV_HEAD_DIM
    ones_bd = (head[:, None] == head[None, :]).astype(F32)
    v = lambda t: t.reshape(1, -1)
    return (v(mu), wl.astype(BF16), v(w0), v(a0), v(k_k), v(k_a), v(r_k), ones_bd)


def _bdot(a, b):
    return jnp.dot(a.astype(BF16), b.astype(BF16), preferred_element_type=F32)


def _bdot_nt(a, b):
    return lax.dot_general(a.astype(BF16), b.astype(BF16), (((1,), (1,)), ((), ())), preferred_element_type=F32)


def _split(x):
    hi = x.astype(BF16)
    return hi, (x - hi.astype(F32)).astype(BF16)


def _dot3(a, b):
    ah, al = _split(a)
    bh, bl = _split(b)
    return (jnp.dot(ah, bh, preferred_element_type=F32) + jnp.dot(al, bh, preferred_element_type=F32)
            + jnp.dot(ah, bl, preferred_element_type=F32))


def _dot3_nt(a, b):
    ah, al = _split(a)
    bh, bl = _split(b)
    dn = (((1,), (1,)), ((), ()))
    return (lax.dot_general(ah, bh, dn, preferred_element_type=F32)
            + lax.dot_general(al, bh, dn, preferred_element_type=F32)
            + lax.dot_general(ah, bl, dn, preferred_element_type=F32))


RWKV_GROUP = 4
RWKV_GW = RWKV_GROUP * RWKV_HEAD_DIM
RWKV_NG = RWKV_HEADS // RWKV_GROUP


def _rwkv_chunk_kernel(r_ref, ls_ref, k_ref, v_ref, n_ref, b_ref, s0_ref, y_ref, sf_ref, s_ref, *, C, ns):
    c = pl.program_id(1)
    G, GW, HD = RWKV_GROUP, RWKV_GW, RWKV_HEAD_DIM
    RI = ns * C
    R = G * RI
    SB = G * C
    groups = range(RWKV_NG)

    @pl.when(c == 0)
    def _():
        for q in groups:
            for s in range(ns):
                s_ref[q, s] = jnp.concatenate([s0_ref[s, G * q + h] for h in range(G)], axis=1)

    ri = lax.broadcasted_iota(jnp.int32, (2 * RI, RI), 0)
    ci = lax.broadcasted_iota(jnp.int32, (2 * RI, RI), 1)
    same = ((ri % RI) // C) == (ci // C)
    cum = (same & ((ri >= RI) | (ri >= ci))).astype(BF16)
    ls_all = ls_ref[...]
    l1 = ls_all.astype(BF16)
    l2f = ls_all - l1.astype(F32)
    l2 = l2f.astype(BF16)
    l3 = (l2f - l2.astype(F32)).astype(BF16)
    lw2 = (jnp.dot(cum, l1, preferred_element_type=F32) + jnp.dot(cum, l2, preferred_element_type=F32)
           + jnp.dot(cum, l3, preferred_element_type=F32))
    lw_all, lwl_all = lw2[:RI], lw2[RI:]

    lane_in = lax.broadcasted_iota(jnp.int32, (C, GW), 1) // HD
    row = lax.broadcasted_iota(jnp.int32, (R, R), 0)
    col = lax.broadcasted_iota(jnp.int32, (R, R), 1)
    ent = (row // C) == (col // C)
    strict = ent & (row > col)
    incl = ent & (row >= col)
    eye = row == col
    eye_f = eye.astype(F32)
    own = (lax.broadcasted_iota(jnp.int32, (SB, GW), 0) // C) == (lax.broadcasted_iota(jnp.int32, (SB, GW), 1) // HD)
    rows_r = lax.broadcasted_iota(jnp.int32, (R, GW), 0)
    rows_2r = lax.broadcasted_iota(jnp.int32, (2 * R, GW), 0)

    def stack(x):
        parts = []
        for s in range(ns):
            xs = x[s * C:(s + 1) * C]
            parts += [jnp.where(lane_in == h, xs, 0.0) for h in range(G)]
        return jnp.concatenate(parts, axis=0)

    def dup(x):
        parts = []
        for s in range(ns):
            parts += [x[s * C:(s + 1) * C]] * G
        return jnp.concatenate(parts, axis=0)

    st = []
    for q in groups:
        gl = slice(q * GW, (q + 1) * GW)
        lw, lwl, ls = lw_all[:, gl], lwl_all[:, gl], ls_all[:, gl]
        w_inv = jnp.exp(-lw)
        w_rest = jnp.exp(lwl - lw)
        kk, bb = k_ref[:, gl], b_ref[:, gl]
        st.append(dict(
            n_st=stack(n_ref[:, gl] * jnp.exp(lw - ls)), r_st=stack(r_ref[:, gl] * jnp.exp(lw)),
            v_st=stack(v_ref[:, gl]), bh_st=stack(bb * w_rest), kh_st=stack(kk * w_rest),
            b_dup=dup(bb * w_inv), k_dup=dup(kk * w_inv), w_c=jnp.exp(lwl)))
    for d in st:
        nr = jnp.concatenate([d['n_st'], d['r_st']], axis=0)
        gb = _bdot_nt(nr, d['b_dup'])
        gk = _bdot_nt(nr, d['k_dup'])
        d['a_nb'] = jnp.where(strict, gb[:R], 0.0)
        d['a_rb'] = jnp.where(incl, gb[R:], 0.0)
        d['a_nk'] = jnp.where(strict, gk[:R], 0.0)
        d['a_rk'] = jnp.where(incl, gk[R:], 0.0)
        d['t'] = eye_f - d['a_nb']
        d['p'] = d['a_nb']
    for _ in range(C.bit_length() - 2):
        for d in st:
            d['p'] = _bdot(d['p'], d['p'])
        for d in st:
            d['t'] = _bdot(d['t'], eye_f + d['p'])
    for d in st:
        d['p1'] = _bdot(d['t'], d['n_st'])
        d['z'] = _bdot(d['a_nk'], d['v_st'])
    for d in st:
        d['p2'] = _bdot(d['t'], d['z'])
    for q, d in enumerate(st):
        p1, p2 = d['p1'], d['p2']
        p1_t = p1.T
        lt = jnp.concatenate([d['v_st'].T, -p2.T], axis=1)
        kb = jnp.concatenate([d['kh_st'], d['bh_st']], axis=0)
        sa_parts, rs_parts = [], []
        for s in range(ns):
            rsl = slice(s * SB, (s + 1) * SB)
            S = s_ref[q, s]
            ss = jnp.concatenate([S] * G, axis=0)
            xr = _dot3_nt(jnp.concatenate([p1[rsl], d['r_st'][rsl]], axis=0), ss)
            sa_parts.append(-jnp.where(own, xr[:SB], 0.0) - p2[rsl])
            rs_parts.append(jnp.where(own, xr[SB:], 0.0))
            if ns == 1:
                bh_s, kb_s = d['bh_st'], kb
            else:
                bh_s = jnp.where((rows_r // SB) == s, d['bh_st'], 0.0)
                kb_s = jnp.where(((rows_2r % R) // SB) == s, kb, 0.0)
            m_bd = jnp.where(eye, d['w_c'][s * C:s * C + 1, :], 0.0) - _bdot(p1_t, bh_s)
            nf = _bdot(lt, kb_s)
            fold = nf[:HD]
            for h in range(1, G):
                fold = fold + nf[h * HD:(h + 1) * HD]
            s_ref[q, s] = _dot3(S, m_bd) + fold
        sa_st = jnp.concatenate(sa_parts, axis=0) if ns > 1 else sa_parts[0]
        rs_st = jnp.concatenate(rs_parts, axis=0) if ns > 1 else rs_parts[0]
        y_st = rs_st + _bdot(jnp.concatenate([d['a_rb'], d['a_rk']], axis=1),
                             jnp.concatenate([sa_st, d['v_st']], axis=0))
        for s in range(ns):
            y = y_st[s * SB:s * SB + C]
            for h in range(1, G):
                y = y + y_st[s * SB + h * C:s * SB + (h + 1) * C]
            y_ref[s * C:(s + 1) * C, q * GW:(q + 1) * GW] = y

    @pl.when(c == pl.num_programs(1) - 1)
    def _():
        for q in groups:
            for s in range(ns):
                S = s_ref[q, s]
                for h in range(G):
                    sf_ref[s, G * q + h] = S[:, h * HD:(h + 1) * HD]


def _rwkv_chunked(r, ls, k, v, kk, b, state, seq0, row0, nseq, L):
    HD, W = RWKV_HEAD_DIM, RWKV_WIDTH
    C = min(L, HD)
    ns = HD // C
    nt = L // C
    rows = ns * C
    base = row0 // rows
    sbase = seq0 // ns
    assert row0 % rows == 0 and (ns == 1 or nt == 1) and nseq % ns == 0 and seq0 % ns == 0
    row_spec = pl.BlockSpec((rows, W), lambda s, c: (base + s * nt + c, 0))
    st_spec = pl.BlockSpec((ns, RWKV_HEADS, HD, HD), lambda s, c: (s, 0, 0, 0))
    st_in = pl.BlockSpec((ns, RWKV_HEADS, HD, HD), lambda s, c: (sbase + s, 0, 0, 0))
    return pl.pallas_call(
        functools.partial(_rwkv_chunk_kernel, C=C, ns=ns),
        out_shape=(jax.ShapeDtypeStruct((nseq * L, W), F32), jax.ShapeDtypeStruct((nseq, RWKV_HEADS, HD, HD), F32)),
        grid=(nseq // ns, nt),
        in_specs=[row_spec] * 6 + [st_in],
        out_specs=(pl.BlockSpec((rows, W), lambda s, c: (s * nt + c, 0)), st_spec),
        scratch_shapes=[pltpu.VMEM((RWKV_NG, ns, HD, RWKV_GW), F32)],
        compiler_params=_params("parallel", "arbitrary"),
        name="rwkv_chunked",
    )(r, ls, k, v, kk, b, state)


def kernel(x_prompt, x_sample, mem_prompt, cache_mem_k, cache_mem_v, state_ssd, state_ssd_conv, state_rwkv, state_rwkv_shift, state_s5_re, state_s5_im, norm_mix, w_in, ssd_conv_w, ssd_conv_b, ssd_dt_bias, ssd_a_log, ssd_d, ssd_norm_w, rwkv_mu, rwkv_w0, rwkv_w2, rwkv_a0, rwkv_a2, rwkv_g2, rwkv_k_k, rwkv_k_a, rwkv_r_k, rwkv_gn_w, rwkv_gn_b, s5_lam_re, s5_lam_im, s5_b_re, s5_b_im, s5_c_re, s5_c_im, s5_d, s5_log_dt, s5_glu_w, s5_glu_b, s5_norm_w, w_out, norm_mem, mem_norm_w, wq_mem, wk_mem, wv_mem, wo_mem, norm_ffn, ffn_w1, ffn_w3, ffn_w2, moe_router_w, moe_router_b, moe_w1, moe_w3, moe_w2, final_norm_w):
    bp, lp, d = x_prompt.shape
    bs, ls, _ = x_sample.shape
    tp, ts = bp * lp, bs * ls
    x = jnp.concatenate([x_prompt.reshape(tp, d), x_sample.reshape(ts, d)], axis=0)
    mem_rows = mem_prompt.reshape(bp * MEM_TOKENS, d)
    s5_pad = SUBLANES - bp
    ssd_all = state_ssd.reshape(DEPTH * bs, SSD_WIDTH, SSD_STATE)
    conv_all = state_ssd_conv.reshape(DEPTH * bs, SSD_CONV - 1, SSD_CONV_DIM)
    wkv_all = state_rwkv.reshape(DEPTH * bs, RWKV_HEADS, RWKV_HEAD_DIM, RWKV_HEAD_DIM)
    ck_all = cache_mem_k.reshape(DEPTH * bs * MEM_TOKENS, MEM_WIDTH)
    cv_all = cache_mem_v.reshape(DEPTH * bs * MEM_TOKENS, MEM_WIDTH)
    ssd_zero = jnp.zeros((bp, SSD_WIDTH, SSD_STATE), F32)
    conv_zero = jnp.zeros((bp, SSD_CONV - 1, SSD_CONV_DIM), F32)
    wkv_zero = jnp.zeros((bp, RWKV_HEADS, RWKV_HEAD_DIM, RWKV_HEAD_DIM), F32)

    p_mk, p_mv, p_st, s_st = [], [], [], []
    for i in range(DEPTH):
        c0 = SSD_WIDTH
        c1 = c0 + SSD_CONV_DIM
        c2 = c1 + SSD_HEADS
        wi = w_in[i]
        w_in_packed = jnp.concatenate(
            [wi[:, :c1], wi[:, c2:], wi[:, c1:c2], jnp.zeros((d, PROJ_WIDTH - PROJ_DT - SSD_HEADS), F32)],
            axis=1).astype(BF16)
        proj = _mm(x, w_in_packed, norm_w=norm_mix[i])

        wkv = jnp.concatenate([wk_mem[i], wv_mem[i]], axis=1).astype(BF16)
        kv = _mm(mem_rows, wkv, norm_w=mem_norm_w[i])
        p_mk.append(kv[:, :MEM_WIDTH].reshape(bp, MEM_TOKENS, MEM_HEADS, MEM_HEAD_DIM))
        p_mv.append(kv[:, MEM_WIDTH:].reshape(bp, MEM_TOKENS, MEM_HEADS, MEM_HEAD_DIM))

        ssd_prm = _ssd_params(ssd_conv_w[i], ssd_conv_b[i], ssd_dt_bias[i], ssd_a_log[i], ssd_d[i], ssd_norm_w[i])
        y_ssd_p, ssd_p, conv_p = _ssd(proj, 0, bp, lp, ssd_zero, conv_zero, 0, ssd_prm, sb=1)
        y_ssd_s, ssd_s, conv_s = _ssd(proj, tp, bs, ls, ssd_all, conv_all, i * bs, ssd_prm, sb=8)

        s5_prm = _s5_params(s5_lam_re[i], s5_lam_im[i], s5_b_re[i], s5_b_im[i], s5_c_re[i], s5_c_im[i], s5_d[i],
                            s5_log_dt[i], s5_glu_w[i], s5_glu_b[i], s5_norm_w[i])
        u = proj[:, PROJ_U:PROJ_DT]
        u_p = jnp.pad(jnp.swapaxes(u[:tp].reshape(bp, lp, S5_WIDTH), 0, 1), ((0, 0), (0, s5_pad), (0, 0)))
        zst = jnp.zeros((SUBLANES, S5_CHANNELS), F32)
        y5_p, s5r_p, s5i_p = _s5(u_p, zst, zst, s5_prm, tc=64, nbb=SUBLANES)
        u_s = jnp.swapaxes(u[tp:].reshape(bs, ls, S5_WIDTH), 0, 1)
        y5_s, s5r_s, s5i_s = _s5(u_s, state_s5_re[i].reshape(bs, S5_CHANNELS),
                                 state_s5_im[i].reshape(bs, S5_CHANNELS), s5_prm, tc=ls, nbb=64)
        y_s5 = jnp.concatenate([jnp.swapaxes(y5_p[:, :bp], 0, 1).reshape(tp, S5_WIDTH),
                                jnp.swapaxes(y5_s, 0, 1).reshape(ts, S5_WIDTH)], axis=0)

        rw_prm = _rwkv_params(rwkv_mu[i], rwkv_w0[i], rwkv_w2[i], rwkv_a0[i], rwkv_a2[i], rwkv_g2[i],
                              rwkv_k_k[i], rwkv_k_a[i], rwkv_r_k[i])
        r_, w_, k_, v_, kk_, b_, g_, bonus = _rwkv_prep(proj, tp, lp, ls, state_rwkv_shift[i], rw_prm)
        shift_p = proj[:tp].reshape(bp, lp, PROJ_WIDTH)[:, -1, PROJ_HR:PROJ_U]
        shift_s = proj[tp:].reshape(bs, ls, PROJ_WIDTH)[:, -1, PROJ_HR:PROJ_U]
        yp_, wkv_p = _rwkv_chunked(r_, w_, k_, v_, kk_, b_, wkv_zero, 0, 0, bp, lp)
        ys_, wkv_s = _rwkv_chunked(r_, w_, k_, v_, kk_, b_, wkv_all, i * bs, tp, bs, ls)
        y_rw = _rwkv_post((yp_, ys_), bonus, g_, rwkv_gn_w[i], rwkv_gn_b[i], rw_prm[-1])

        p_st.append((ssd_p.reshape(bp, SSD_HEADS, SSD_HEAD_DIM, SSD_STATE), conv_p, wkv_p, shift_p,
                     s5r_p[:bp].reshape(bp, S5_GROUPS, S5_STATE), s5i_p[:bp].reshape(bp, S5_GROUPS, S5_STATE)))
        s_st.append((ssd_s.reshape(bs, SSD_HEADS, SSD_HEAD_DIM, SSD_STATE), conv_s, wkv_s, shift_s,
                     s5r_s.reshape(bs, S5_GROUPS, S5_STATE), s5i_s.reshape(bs, S5_GROUPS, S5_STATE)))

        x = _mm([(y_ssd_p, y_ssd_s), y_rw, y_s5], w_out[i].astype(BF16), residual=x)

        q = _mm(x, wq_mem[i].astype(BF16), norm_w=norm_mem[i])
        o_p = _attend(q, 0, bp, lp, kv, kv, 0, 1, 0, lq=512, sb=1)
        o_s = _attend(q, tp, bs, ls, ck_all, cv_all, 0, 0, i * bs, lq=ls, sb=8)
        x = _mm([(o_p, o_s)], wo_mem[i].astype(BF16), residual=x)

        j = i // 2
        if i % 2 == 0:
            x = _ffn(x, norm_ffn[i], ffn_w1[j].astype(BF16), ffn_w3[j].astype(BF16), ffn_w2[j].astype(BF16))
        else:
            x = _moe(x, norm_ffn[i], moe_router_w[j], moe_router_b[j], moe_w1[j], moe_w3[j], moe_w2[j],
                     final_w=final_norm_w if i == DEPTH - 1 else None)

    y = x if DEPTH % 2 == 0 else _final_norm(x, final_norm_w)
    y_prompt = y[:tp].reshape(bp, lp, d)
    y_sample = y[tp:].reshape(bs, ls, d)

    def stk(lst, j):
        return jnp.stack([s[j] for s in lst])

    return (y_prompt, y_sample, jnp.stack(p_mk), jnp.stack(p_mv),
            stk(p_st, 0), stk(p_st, 1), stk(p_st, 2), stk(p_st, 3), stk(p_st, 4), stk(p_st, 5),
            stk(s_st, 0), stk(s_st, 1), stk(s_st, 2), stk(s_st, 3), stk(s_st, 4), stk(s_st, 5))
```
